```python
import math
import jax, jax.numpy as jnp
from jax import lax
import numpy as np

D_MODEL = 1024
BATCH = 8
SEQ = 4096
DEPTH = 1

N_META = 16
CHUNK = 128
Q_BLOCK = 128
PAD = CHUNK - N_META
D_SSD = 2 * D_MODEL
SSD_HEAD_DIM = 64
H_SSD = D_SSD // SSD_HEAD_DIM
SSD_GROUPS = 4
D_STATE = 128
CONV_K = 4
CONV_DIM = D_SSD + 2 * SSD_GROUPS * D_STATE
H_ATT = 16
ATT_HEAD_DIM = 64
D_ATT = H_ATT * ATT_HEAD_DIM
N_COLS = D_SSD + CONV_DIM + H_SSD + D_ATT + 3 * D_ATT + H_ATT + 2 * D_MODEL
EPS = 1e-6

kernel_name = "hybrid_ssd_fox_gated_merge"


def rmsnorm(x, g):
    xf = x.astype(jnp.float32)
    y = xf * lax.rsqrt(jnp.mean(xf * xf, axis=-1, keepdims=True) + EPS)
    return (y * g.astype(jnp.float32)).astype(x.dtype)


def gated_group_rmsnorm(y, z, g):
    u = (y * jax.nn.silu(z)).astype(jnp.float32)
    shp = u.shape
    u = u.reshape(shp[:-1] + (SSD_GROUPS, shp[-1] // SSD_GROUPS))
    u = u * lax.rsqrt(jnp.mean(u * u, axis=-1, keepdims=True) + EPS)
    return (u.reshape(shp) * g.astype(jnp.float32)).astype(y.dtype)


def causal_depthwise_conv(u, w, b):
    C = u.shape[-1]
    out = lax.conv_general_dilated(u, w[:, None, :].astype(u.dtype), window_strides=(1,),
                                   padding=[(CONV_K - 1, 0)],
                                   dimension_numbers=("NWC", "WIO", "NWC"),
                                   feature_group_count=C)
    return out + b


def ssd_chunked(xh, dt, a, bmat, cmat):
    Bsz, Lp, H, P = xh.shape
    G, N = bmat.shape[-2:]
    R = H // G
    nc = Lp // CHUNK
    xdt = (xh.astype(jnp.float32) * dt[..., None]).reshape(Bsz, nc, CHUNK, G, R, P)
    adt = (dt * a).reshape(Bsz, nc, CHUNK, G, R).transpose(0, 1, 3, 4, 2)
    a_cs = jnp.cumsum(adt, axis=-1)
    bm = bmat.astype(jnp.float32).reshape(Bsz, nc, CHUNK, G, N)
    cm = cmat.astype(jnp.float32).reshape(Bsz, nc, CHUNK, G, N)
    causal = jnp.tril(jnp.ones((CHUNK, CHUNK), dtype=bool))
    seg = a_cs[..., :, None] - a_cs[..., None, :]
    decay = jnp.exp(jnp.where(causal, seg, -jnp.inf))
    cb = jnp.einsum("bclgn,bcsgn->bcgls", cm, bm)
    y_diag = jnp.einsum("bcgls,bcgrls,bcsgrp->bclgrp", cb, decay, xdt)
    decay_states = jnp.exp(a_cs[..., -1:] - a_cs)
    states = jnp.einsum("bclgn,bcgrl,bclgrp->bcgrpn", bm, decay_states, xdt)
    chunk_decay = jnp.exp(a_cs[..., -1])

    def step(h, inp):
        s, d = inp
        return h * d[..., None, None] + s, h

    h0 = jnp.zeros((Bsz, G, R, P, N), jnp.float32)
    _, h_in = lax.scan(step, h0, (jnp.swapaxes(states, 0, 1), jnp.swapaxes(chunk_decay, 0, 1)))
    h_in = jnp.swapaxes(h_in, 0, 1)
    y_off = jnp.einsum("bclgn,bcgrpn,bcgrl->bclgrp", cm, h_in, jnp.exp(a_cs))
    return (y_diag + y_off).reshape(Bsz, Lp, H, P)


def forgetting_attention(q, k, v, logf):
    Bsz, Lp, H, Dh = q.shape
    scale = 1.0 / math.sqrt(Dh)
    c = jnp.cumsum(logf, axis=1)
    c_k = jnp.transpose(c, (0, 2, 1))
    nb = Lp // Q_BLOCK
    qb = q.reshape(Bsz, nb, Q_BLOCK, H, Dh).transpose(1, 0, 2, 3, 4)
    cqb = c.reshape(Bsz, nb, Q_BLOCK, H).transpose(1, 0, 3, 2)
    kpos = jnp.arange(Lp)

    def block(args):
        i, qi, cqi = args
        qpos = i * Q_BLOCK + jnp.arange(Q_BLOCK)
        s = jnp.einsum("bthd,bshd->bhts", qi, k).astype(jnp.float32) * scale
        s = s + (cqi[..., :, None] - c_k[..., None, :])
        s = jnp.where(kpos[None, :] <= qpos[:, None], s, -jnp.inf)
        p = jax.nn.softmax(s, axis=-1).astype(v.dtype)
        return jnp.einsum("bhts,bshd->bthd", p, v)

    out = lax.map(block, (jnp.arange(nb), qb, cqb))
    return out.transpose(1, 0, 2, 3, 4).reshape(Bsz, Lp, H, Dh)


def hybrid_layer(h, norm_pre, w_in, conv_w, conv_b, dt_bias, a_log, d_skip, ssd_norm,
                 fgate_bias, gate_bias, w_proj_ssd, w_proj_att, w_out, norm_post):
    Bsz, L, _ = h.shape
    Lp = L + PAD
    u = rmsnorm(h, norm_pre)
    proj = u @ w_in
    cuts = [D_SSD, D_SSD + CONV_DIM, D_SSD + CONV_DIM + H_SSD,
            D_SSD + CONV_DIM + H_SSD + D_ATT,
            D_SSD + CONV_DIM + H_SSD + 2 * D_ATT,
            D_SSD + CONV_DIM + H_SSD + 3 * D_ATT,
            D_SSD + CONV_DIM + H_SSD + 4 * D_ATT,
            D_SSD + CONV_DIM + H_SSD + 4 * D_ATT + H_ATT]
    z_ssd, xbc, dt_raw, z_att, q, k, v, f_raw, g_raw = jnp.split(proj, cuts, axis=-1)

    xbc = jax.nn.silu(causal_depthwise_conv(xbc, conv_w, conv_b))
    xs, bm, cm = jnp.split(xbc, [D_SSD, D_SSD + SSD_GROUPS * D_STATE], axis=-1)
    dt = jax.nn.softplus(dt_raw.astype(jnp.float32) + dt_bias.astype(jnp.float32))
    a = -jnp.exp(a_log.astype(jnp.float32))
    front = ((0, 0), (PAD, 0), (0, 0))
    xs_p = jnp.pad(xs, front).reshape(Bsz, Lp, H_SSD, SSD_HEAD_DIM)
    bm_p = jnp.pad(bm, front).reshape(Bsz, Lp, SSD_GROUPS, D_STATE)
    cm_p = jnp.pad(cm, front).reshape(Bsz, Lp, SSD_GROUPS, D_STATE)
    dt_p = jnp.pad(dt, front)
    y = ssd_chunked(xs_p, dt_p, a, bm_p, cm_p)
    y = y + d_skip.astype(jnp.float32)[:, None] * xs_p.astype(jnp.float32)
    y = y[:, PAD:].reshape(Bsz, L, D_SSD).astype(h.dtype)
    y_ssd = gated_group_rmsnorm(y, z_ssd, ssd_norm)

    logf = jax.nn.log_sigmoid(f_raw.astype(jnp.float32) + fgate_bias.astype(jnp.float32))
    back = ((0, 0), (0, PAD), (0, 0))
    qh = jnp.pad(q, back).reshape(Bsz, Lp, H_ATT, ATT_HEAD_DIM)
    kh = jnp.pad(k, back).reshape(Bsz, Lp, H_ATT, ATT_HEAD_DIM)
    vh = jnp.pad(v, back).reshape(Bsz, Lp, H_ATT, ATT_HEAD_DIM)
    o = forgetting_attention(qh, kh, vh, jnp.pad(logf, back))
    o = o[:, :L].reshape(Bsz, L, D_ATT)
    y_att = o * jax.nn.silu(z_att)

    gates = jax.nn.sigmoid(g_raw + gate_bias)
    g_ssd, g_att = jnp.split(gates, 2, axis=-1)
    merged = g_ssd * (y_ssd @ w_proj_ssd) + g_att * (y_att @ w_proj_att)
    return h + rmsnorm(merged @ w_out, norm_post)


def setup_inputs(seed: int = 0) -> dict:
    key = jax.random.key(seed)
    ks = jax.random.split(key, 16)
    D = D_MODEL
    nrm = jax.random.normal
    x = nrm(ks[0], (BATCH, SEQ, D), jnp.float32)
    meta_tokens = nrm(ks[1], (N_META, D), jnp.float32)
    norm_pre = 1.0 + 0.05 * nrm(ks[2], (DEPTH, D), jnp.float32)
    w_in = nrm(ks[3], (DEPTH, D, N_COLS), jnp.float32) * D ** -0.5
    conv_w = jax.random.uniform(ks[4], (DEPTH, CONV_K, CONV_DIM), jnp.float32, -0.5, 0.5)
    conv_b = 0.05 * nrm(ks[5], (DEPTH, CONV_DIM), jnp.float32)
    dt0 = jnp.exp(jax.random.uniform(ks[6], (DEPTH, H_SSD), jnp.float32,
                                     math.log(1e-3), math.log(1e-1)))
    dt_bias = dt0 + jnp.log(-jnp.expm1(-dt0))
    a_log = jnp.log(jax.random.uniform(ks[7], (DEPTH, H_SSD), jnp.float32, 1.0, 16.0))
    d_skip = 1.0 + 0.1 * nrm(ks[8], (DEPTH, H_SSD), jnp.float32)
    ssd_norm = 1.0 + 0.05 * nrm(ks[9], (DEPTH, D_SSD), jnp.float32)
    fgate_bias = jax.random.uniform(ks[10], (DEPTH, H_ATT), jnp.float32, 1.0, 6.0)
    gate_bias = 0.1 * nrm(ks[11], (DEPTH, 2 * D), jnp.float32)
    w_proj_ssd = nrm(ks[12], (DEPTH, D_SSD, D), jnp.float32) * D_SSD ** -0.5
    w_proj_att = nrm(ks[13], (DEPTH, D_ATT, D), jnp.float32) * D_ATT ** -0.5
    w_out = nrm(ks[14], (DEPTH, D, D), jnp.float32) * D ** -0.5
    norm_post = 1.0 + 0.05 * nrm(ks[15], (DEPTH, D), jnp.float32)
    return {"x": x, "meta_tokens": meta_tokens, "norm_pre": norm_pre, "w_in": w_in,
            "conv_w": conv_w, "conv_b": conv_b, "dt_bias": dt_bias, "a_log": a_log,
            "d_skip": d_skip, "ssd_norm": ssd_norm, "fgate_bias": fgate_bias,
            "gate_bias": gate_bias, "w_proj_ssd": w_proj_ssd, "w_proj_att": w_proj_att,
            "w_out": w_out, "norm_post": norm_post}


def reference(x, meta_tokens, norm_pre, w_in, conv_w, conv_b, dt_bias, a_log, d_skip,
              ssd_norm, fgate_bias, gate_bias, w_proj_ssd, w_proj_att, w_out, norm_post):
    Bsz = x.shape[0]
    meta = jnp.broadcast_to(meta_tokens[None].astype(x.dtype), (Bsz, N_META, D_MODEL))
    h = jnp.concatenate([meta, x], axis=1)
    for i in range(DEPTH):
        h = hybrid_layer(h, norm_pre[i], w_in[i], conv_w[i], conv_b[i], dt_bias[i], a_log[i],
                         d_skip[i], ssd_norm[i], fgate_bias[i], gate_bias[i], w_proj_ssd[i],
                         w_proj_att[i], w_out[i], norm_post[i])
    return h[:, N_META:]
```

```python
import functools

import jax
import jax.numpy as jnp
from jax import lax
from jax.experimental import pallas as pl
from jax.experimental.pallas import tpu as pltpu

F32 = jnp.float32
BF16 = jnp.bfloat16

D_MODEL = 1024
N_META = 16
CHUNK = 128
LANES = 128
D_SSD = 2 * D_MODEL
SSD_HEAD_DIM = 64
H_SSD = D_SSD // SSD_HEAD_DIM
SSD_GROUPS = 4
HEADS_PER_GROUP = H_SSD // SSD_GROUPS
D_STATE = 128
GROUP_WIDTH = HEADS_PER_GROUP * SSD_HEAD_DIM
CONV_K = 4
CONV_DIM = D_SSD + 2 * SSD_GROUPS * D_STATE
H_ATT = 16
ATT_HEAD_DIM = 64
D_ATT = H_ATT * ATT_HEAD_DIM
EPS = 1e-6
TAIL_ROWS = 8
F_LANE0 = H_SSD
P1_COLS = 8 * 1024
VMEM_LIMIT = 56 * 1024 * 1024

_NT = (((1,), (1,)), ((), ()))


def _split3(v):
    hi = v.astype(BF16)
    r1 = v - hi.astype(F32)
    mid = r1.astype(BF16)
    lo = (r1 - mid.astype(F32)).astype(BF16)
    return hi, mid, lo


def _dot_exact_lhs(a_bf16, v):
    hi, mid, lo = _split3(v)
    d = functools.partial(jnp.dot, preferred_element_type=F32)
    return d(a_bf16, hi) + d(a_bf16, mid) + d(a_bf16, lo)


def _dot_exact_rhs(v, b_bf16):
    hi, mid, lo = _split3(v)
    d = functools.partial(jnp.dot, preferred_element_type=F32)
    return d(hi, b_bf16) + d(mid, b_bf16) + d(lo, b_bf16)


def _in_proj_kernel(x_ref, g_ref, w_ref, *rest, with_small):
    if with_small:
        ws_ref, o_ref, os_ref, u_ref = rest
    else:
        o_ref, u_ref = rest
    j = pl.program_id(1)

    @pl.when(j == 0)
    def _():
        x = x_ref[...]
        ms = jnp.mean(x * x, axis=-1, keepdims=True)
        u = (x * lax.rsqrt(ms + EPS) * g_ref[...]).astype(BF16)
        u_ref[...] = u
        if with_small:
            os_ref[...] = jnp.dot(u, ws_ref[...], preferred_element_type=F32)

    o_ref[...] = jnp.dot(u_ref[...], w_ref[...],
                         preferred_element_type=F32).astype(o_ref.dtype)


def _in_proj(x2d, g_row, w, w_small, out_dtype, tm, tn, name):
    m, d = x2d.shape
    n = w.shape[1]
    with_small = w_small is not None
    in_specs = [pl.BlockSpec((tm, d), lambda i, j: (i, 0)),
                pl.BlockSpec((1, d), lambda i, j: (0, 0)),
                pl.BlockSpec((d, tn), lambda i, j: (0, j))]
    out_shape = [jax.ShapeDtypeStruct((m, n), out_dtype)]
    out_specs = [pl.BlockSpec((tm, tn), lambda i, j: (i, j))]
    args = [x2d, g_row, w]
    if with_small:
        in_specs.append(pl.BlockSpec((d, LANES), lambda i, j: (0, 0)))
        out_shape.append(jax.ShapeDtypeStruct((m, LANES), F32))
        out_specs.append(pl.BlockSpec((tm, LANES), lambda i, j: (i, 0)))
        args.append(w_small)
    return pl.pallas_call(
        functools.partial(_in_proj_kernel, with_small=with_small),
        grid=(m // tm, n // tn),
        in_specs=in_specs,
        out_specs=out_specs,
        out_shape=out_shape,
        scratch_shapes=[pltpu.VMEM((tm, d), BF16)],
        compiler_params=pltpu.CompilerParams(
            dimension_semantics=("parallel", "arbitrary"),
            vmem_limit_bytes=VMEM_LIMIT),
        name=name,
    )(*args)


def _ssd_kernel(x0_ref, x1_ref, bc_ref, s_ref, convw_ref, convb_ref, brow_ref, alog_ref,
                dskip_ref, e_ref, st0_ref, tail0_ref, c0_ref,
                y_ref, c_ref, ct_ref, st_ref, tail_ref, cc_ref,
                ext_ref, act_ref, *, first_valid):
    ci = pl.program_id(1)

    @pl.when(ci == 0)
    def _():
        st_ref[0] = st0_ref[...]
        tail_ref[0] = tail0_ref[...]
        cc_ref[0] = c0_ref[...]

    ext_ref[0:TAIL_ROWS, :] = tail_ref[0]
    ext_ref[TAIL_ROWS:TAIL_ROWS + CHUNK, 0:1024] = x0_ref[0]
    ext_ref[TAIL_ROWS:TAIL_ROWS + CHUNK, 1024:2048] = x1_ref[0]
    ext_ref[TAIL_ROWS:TAIL_ROWS + CHUNK, 2048:3072] = bc_ref[0]
    tail_ref[0] = ext_ref[CHUNK:CHUNK + TAIL_ROWS, :]

    row = lax.broadcasted_iota(jnp.int32, (CHUNK, 1), 0)
    valid_row = row >= first_valid
    base = TAIL_ROWS - (CONV_K - 1)
    for j in range(CONV_DIM // GROUP_WIDTH):
        sl = slice(j * GROUP_WIDTH, (j + 1) * GROUP_WIDTH)
        acc = convb_ref[:, sl] + convw_ref[0:1, sl] * ext_ref[base:base + CHUNK, sl]
        for k in range(1, CONV_K):
            acc = acc + convw_ref[k:k + 1, sl] * ext_ref[base + k:base + k + CHUNK, sl]
        a = acc * jax.nn.sigmoid(acc)
        if first_valid:
            a = jnp.where(valid_row, a, 0.0)
        act_ref[:, sl] = a

    lane = lax.broadcasted_iota(jnp.int32, (1, LANES), 1)
    is_dt = lane < H_SSD
    is_f = (lane >= F_LANE0) & (lane < F_LANE0 + H_ATT)
    sv = s_ref[0] + brow_ref[...]
    z = jnp.where(is_dt, sv, -sv)
    sp = jnp.maximum(z, 0.0) + jnp.log1p(jnp.exp(-jnp.abs(z)))
    a_row = -jnp.exp(alog_ref[...])
    vals = jnp.where(is_dt, sp * a_row, jnp.where(is_f, -sp, 0.0))
    dt = jnp.where(is_dt, sp, 0.0)
    if first_valid:
        vals = jnp.where(valid_row, vals, 0.0)
        dt = jnp.where(valid_row, dt, 0.0)

    ri = lax.broadcasted_iota(jnp.int32, (CHUNK, CHUNK), 0)
    cj = lax.broadcasted_iota(jnp.int32, (CHUNK, CHUNK), 1)
    tri = cj <= ri
    cum = _dot_exact_lhs(tri.astype(BF16), vals)
    cfull = cum + cc_ref[0]
    c_ref[0] = cfull
    cum_t = cfull.T
    ct_ref[0, 0] = cum_t[F_LANE0:F_LANE0 + H_ATT, :]
    cc_ref[0] = jnp.where(is_f, cfull[CHUNK - 1:CHUNK, :], 0.0)
    dt_t = dt.T

    a_last = cum[CHUNK - 1:CHUNK, :]
    w = dt * jnp.exp(a_last - cum)
    ea = jnp.where(is_dt, jnp.exp(cum), 0.0)
    cd = jnp.where(is_dt, jnp.exp(a_last), 0.0)
    stacked = jnp.concatenate([w, ea, jnp.broadcast_to(cd, (8, LANES))], axis=0)
    expanded = _dot_exact_rhs(stacked, e_ref[...])
    w_e = expanded[0:CHUNK]
    ea_e = expanded[CHUNK:2 * CHUNK]
    cd_e = expanded[2 * CHUNK:2 * CHUNK + 1]

    lo_half = lane < SSD_HEAD_DIM
    neg_inf = jnp.float32(-jnp.inf)
    for g in range(SSD_GROUPS):
        gsl = slice(g * GROUP_WIDTH, (g + 1) * GROUP_WIDTH)
        bm = act_ref[:, D_SSD + g * D_STATE:D_SSD + (g + 1) * D_STATE]
        cm = act_ref[:, D_SSD + SSD_GROUPS * D_STATE + g * D_STATE:
                     D_SSD + SSD_GROUPS * D_STATE + (g + 1) * D_STATE]
        bm16 = bm.astype(BF16)
        cm16 = cm.astype(BF16)
        cb = lax.dot_general(cm16, bm16, _NT, preferred_element_type=F32)
        hin = st_ref[0, g]
        xw = (act_ref[:, gsl] * w_e[:, gsl]).astype(BF16)
        st_ref[0, g] = hin * cd_e[:, gsl] + jnp.dot(bm.T.astype(BF16), xw,
                                                    preferred_element_type=F32)
        yoff = jnp.dot(cm16, hin.astype(BF16), preferred_element_type=F32) * ea_e[:, gsl]
        for pr in range(HEADS_PER_GROUP // 2):
            psl = slice(g * GROUP_WIDTH + pr * LANES, g * GROUP_WIDTH + (pr + 1) * LANES)
            xs_p = act_ref[:, psl]
            ms = []
            for h in (g * HEADS_PER_GROUP + 2 * pr, g * HEADS_PER_GROUP + 2 * pr + 1):
                seg = cum[:, h:h + 1] - cum_t[h:h + 1, :]
                dec = jnp.exp(jnp.where(tri, seg, neg_inf))
                ms.append((cb * dec * dt_t[h:h + 1, :]).astype(BF16))
            m2 = jnp.concatenate(ms, axis=1)
            xs16 = xs_p.astype(BF16)
            zero16 = jnp.zeros_like(xs16)
            rhs = jnp.concatenate([jnp.where(lo_half, xs16, zero16),
                                   jnp.where(lo_half, zero16, xs16)], axis=0)
            ydiag = jnp.dot(m2, rhs, preferred_element_type=F32)
            y_ref[0, :, psl] = (ydiag + yoff[:, pr * LANES:(pr + 1) * LANES]
                                + dskip_ref[:, psl] * xs_p)


def _ssd(p1, small, convw, convb, brow, alog_row, dskip_row, expand_mat, st0, tail0, c0,
         first_valid, name):
    b, s, _ = p1.shape
    nc = s // CHUNK
    const = lambda shape: pl.BlockSpec(shape, lambda bi, ci: (0,) * len(shape))
    in_specs = [
        pl.BlockSpec((1, CHUNK, 1024), lambda bi, ci: (bi, ci, 2)),
        pl.BlockSpec((1, CHUNK, 1024), lambda bi, ci: (bi, ci, 3)),
        pl.BlockSpec((1, CHUNK, 1024), lambda bi, ci: (bi, ci, 4)),
        pl.BlockSpec((1, CHUNK, LANES), lambda bi, ci: (bi, ci, 0)),
        const((CONV_K, CONV_DIM)), const((1, CONV_DIM)), const((1, LANES)), const((1, LANES)),
        const((1, D_SSD)), const((LANES, D_SSD)),
        const((SSD_GROUPS, D_STATE, GROUP_WIDTH)), const((TAIL_ROWS, CONV_DIM)), const((1, LANES)),
    ]
    out_shape = [
        jax.ShapeDtypeStruct((b, s, D_SSD), F32),
        jax.ShapeDtypeStruct((b, s, LANES), F32),
        jax.ShapeDtypeStruct((b, nc, H_ATT, CHUNK), F32),
        jax.ShapeDtypeStruct((b, SSD_GROUPS, D_STATE, GROUP_WIDTH), F32),
        jax.ShapeDtypeStruct((b, TAIL_ROWS, CONV_DIM), F32),
        jax.ShapeDtypeStruct((b, 1, LANES), F32),
    ]
    out_specs = [
        pl.BlockSpec((1, CHUNK, D_SSD), lambda bi, ci: (bi, ci, 0)),
        pl.BlockSpec((1, CHUNK, LANES), lambda bi, ci: (bi, ci, 0)),
        pl.BlockSpec((1, 1, H_ATT, CHUNK), lambda bi, ci: (bi, ci, 0, 0)),
        pl.BlockSpec((1, SSD_GROUPS, D_STATE, GROUP_WIDTH), lambda bi, ci: (bi, 0, 0, 0)),
        pl.BlockSpec((1, TAIL_ROWS, CONV_DIM), lambda bi, ci: (bi, 0, 0)),
        pl.BlockSpec((1, 1, LANES), lambda bi, ci: (bi, 0, 0)),
    ]
    return pl.pallas_call(
        functools.partial(_ssd_kernel, first_valid=first_valid),
        grid=(b, nc),
        in_specs=in_specs,
        out_specs=out_specs,
        out_shape=out_shape,
        scratch_shapes=[pltpu.VMEM((TAIL_ROWS + CHUNK, CONV_DIM), F32),
                        pltpu.VMEM((CHUNK, CONV_DIM), F32)],
        compiler_params=pltpu.CompilerParams(
            dimension_semantics=("parallel", "arbitrary"),
            vmem_limit_bytes=VMEM_LIMIT),
        name=name,
    )(p1, p1, p1, small, convw, convb, brow, alog_row, dskip_row, expand_mat, st0, tail0, c0)


def _attn_kernel(q_ref, k_ref, v_ref, cq_ref, ct_ref, km_ref, vm_ref, ctm_ref, o_ref,
                 m_scr, l_scr, acc_scr, *, tq, tk, meta_first_valid):
    hp = pl.program_id(1)
    qi = pl.program_id(2)
    lane = lax.broadcasted_iota(jnp.int32, (1, LANES), 1)
    lo_half = lane < ATT_HEAD_DIM
    neg_inf = jnp.float32(-jnp.inf)

    q = q_ref[0]
    zq = jnp.zeros_like(q)
    q2 = jnp.concatenate([jnp.where(lo_half, q, zq), jnp.where(lo_half, zq, q)], axis=0)
    cq = cq_ref[0]
    cqs = [jnp.sum(jnp.where(lane == F_LANE0 + 2 * hp + hh, cq, 0.0), axis=1, keepdims=True)
           for hh in range(2)]

    m_scr[...] = jnp.full(m_scr.shape, neg_inf, F32)
    l_scr[...] = jnp.zeros(l_scr.shape, F32)
    acc_scr[...] = jnp.zeros(acc_scr.shape, F32)

    def step(kj, vj, cks, mask):
        n = kj.shape[0]
        s = lax.dot_general(q2, kj, _NT, preferred_element_type=F32)
        ps, alphas = [], []
        for hh in range(2):
            sh = s[hh * tq:(hh + 1) * tq] + (cqs[hh] - cks[hh])
            if mask is not None:
                sh = jnp.where(mask, sh, neg_inf)
            m_prev = m_scr[hh]
            m_next = jnp.maximum(m_prev, jnp.max(sh, axis=1, keepdims=True))
            alpha = jnp.exp(m_prev - m_next)
            p = jnp.exp(sh - jnp.concatenate([m_next] * (n // LANES), axis=1))
            l_scr[hh] = alpha * l_scr[hh] + jnp.sum(p, axis=1, keepdims=True)
            m_scr[hh] = m_next
            ps.append(p.astype(BF16))
            alphas.append(alpha)
        p2 = jnp.concatenate(ps, axis=1)
        zv = jnp.zeros_like(vj)
        v2 = jnp.concatenate([jnp.where(lo_half, vj, zv), jnp.where(lo_half, zv, vj)], axis=0)
        pv = jnp.dot(p2, v2, preferred_element_type=F32)
        acc_scr[...] = acc_scr[...] * jnp.where(lo_half, alphas[0], alphas[1]) + pv

    kpos = lax.broadcasted_iota(jnp.int32, (1, CHUNK), 1)
    cks_m = [ctm_ref[pl.ds(2 * hp + hh, 1), :] for hh in range(2)]
    step(km_ref[...], vm_ref[...], cks_m, kpos >= meta_first_valid)

    def load_block(j):
        start = pl.multiple_of(j * tk, tk)
        kj = k_ref[0, pl.ds(start, tk), :]
        vj = v_ref[0, pl.ds(start, tk), :]
        cks = [jnp.concatenate([ct_ref[0, (tk // CHUNK) * j + t, pl.ds(2 * hp + hh, 1), :]
                                for t in range(tk // CHUNK)], axis=1) for hh in range(2)]
        return kj, vj, cks

    def body(j, carry):
        kj, vj, cks = load_block(j)
        step(kj, vj, cks, None)
        return carry

    lax.fori_loop(0, qi, body, 0)

    kj, vj, cks = load_block(qi)
    rr = lax.broadcasted_iota(jnp.int32, (tq, tk), 0)
    cc = lax.broadcasted_iota(jnp.int32, (tq, tk), 1)
    step(kj, vj, cks, cc <= rr)

    o_ref[0] = acc_scr[...] / jnp.where(lo_half, l_scr[0], l_scr[1])


def _attention(qkv, c, ct, qkv_meta, ct_meta, tq, tk, name):
    b, s, _ = qkv.shape
    nc = s // CHUNK
    npairs = H_ATT // 2
    in_specs = [
        pl.BlockSpec((1, tq, LANES), lambda bi, hp, qi: (bi, qi, hp)),
        pl.BlockSpec((1, s, LANES), lambda bi, hp, qi: (bi, 0, npairs + hp)),
        pl.BlockSpec((1, s, LANES), lambda bi, hp, qi: (bi, 0, 2 * npairs + hp)),
        pl.BlockSpec((1, tq, LANES), lambda bi, hp, qi: (bi, qi, 0)),
        pl.BlockSpec((1, nc, H_ATT, CHUNK), lambda bi, hp, qi: (bi, 0, 0, 0)),
        pl.BlockSpec((CHUNK, LANES), lambda bi, hp, qi: (0, npairs + hp)),
        pl.BlockSpec((CHUNK, LANES), lambda bi, hp, qi: (0, 2 * npairs + hp)),
        pl.BlockSpec((H_ATT, CHUNK), lambda bi, hp, qi: (0, 0)),
    ]
    return pl.pallas_call(
        functools.partial(_attn_kernel, tq=tq, tk=tk, meta_first_valid=CHUNK - N_META),
        grid=(b, npairs, s // tq),
        in_specs=in_specs,
        out_specs=pl.BlockSpec((1, tq, LANES), lambda bi, hp, qi: (bi, qi, hp)),
        out_shape=jax.ShapeDtypeStruct((b, s, D_ATT), F32),
        scratch_shapes=[pltpu.VMEM((2, tq, LANES), F32),
                        pltpu.VMEM((2, tq, LANES), F32),
                        pltpu.VMEM((tq, LANES), F32)],
        compiler_params=pltpu.CompilerParams(
            dimension_semantics=("parallel", "parallel", "arbitrary"),
            vmem_limit_bytes=VMEM_LIMIT),
        name=name,
    )(qkv, qkv, qkv, c, ct, qkv_meta, qkv_meta, ct_meta)


def _merge_kernel(y_ref, zs_ref, o_ref, za_ref, gs_ref, ga_ref, x_ref, wps_ref, wpa_ref,
                  wout_ref, snorm_ref, gbias_ref, npost_ref, out_ref):
    zs = zs_ref[...]
    u = y_ref[...] * (zs * jax.nn.sigmoid(zs))
    parts = []
    for g in range(SSD_GROUPS):
        ug = u[:, g * GROUP_WIDTH:(g + 1) * GROUP_WIDTH]
        ms = jnp.mean(ug * ug, axis=-1, keepdims=True)
        parts.append((ug * lax.rsqrt(ms + EPS)
                      * snorm_ref[:, g * GROUP_WIDTH:(g + 1) * GROUP_WIDTH]).astype(BF16))
    y_ssd = jnp.concatenate(parts, axis=1)
    t_ssd = jnp.dot(y_ssd, wps_ref[...], preferred_element_type=F32)
    za = za_ref[...]
    y_att = (o_ref[...] * (za * jax.nn.sigmoid(za))).astype(BF16)
    t_att = jnp.dot(y_att, wpa_ref[...], preferred_element_type=F32)
    g_ssd = jax.nn.sigmoid(gs_ref[...] + gbias_ref[:, 0:D_MODEL])
    g_att = jax.nn.sigmoid(ga_ref[...] + gbias_ref[:, D_MODEL:2 * D_MODEL])
    merged = (g_ssd * t_ssd + g_att * t_att).astype(BF16)
    t = jnp.dot(merged, wout_ref[...], preferred_element_type=F32)
    ms = jnp.mean(t * t, axis=-1, keepdims=True)
    out_ref[...] = x_ref[...] + t * lax.rsqrt(ms + EPS) * npost_ref[...]


def _merge(y2d, p1, o2d, x2d, wps, wpa, wout, snorm, gbias, npost, tm, name):
    m = x2d.shape[0]
    row_blk = lambda width, col: pl.BlockSpec((tm, width), lambda i: (i, col))
    const = lambda shape: pl.BlockSpec(shape, lambda i: (0, 0))
    in_specs = [
        row_blk(D_SSD, 0),
        row_blk(D_SSD, 0),
        row_blk(D_ATT, 0),
        row_blk(D_ATT, 5),
        row_blk(D_MODEL, 6),
        row_blk(D_MODEL, 7),
        row_blk(D_MODEL, 0),
        const((D_SSD, D_MODEL)), const((D_ATT, D_MODEL)), const((D_MODEL, D_MODEL)),
        const((1, D_SSD)), const((1, 2 * D_MODEL)), const((1, D_MODEL)),
    ]
    return pl.pallas_call(
        _merge_kernel,
        grid=(m // tm,),
        in_specs=in_specs,
        out_specs=pl.BlockSpec((tm, D_MODEL), lambda i: (i, 0)),
        out_shape=jax.ShapeDtypeStruct((m, D_MODEL), F32),
        compiler_params=pltpu.CompilerParams(
            dimension_semantics=("parallel",),
            vmem_limit_bytes=VMEM_LIMIT),
        name=name,
    )(y2d, p1, o2d, p1, p1, p1, x2d, wps, wpa, wout, snorm, gbias, npost)


def kernel(x, meta_tokens, norm_pre, w_in, conv_w, conv_b, dt_bias, a_log, d_skip, ssd_norm,
           fgate_bias, gate_bias, w_proj_ssd, w_proj_att, w_out, norm_post):
    bsz, seq, d = x.shape
    assert d == D_MODEL and seq % 1024 == 0 and norm_pre.shape[0] == 1

    w = w_in[0]
    o_xbc = D_SSD
    o_dt = o_xbc + CONV_DIM
    o_za = o_dt + H_SSD
    o_q = o_za + D_ATT
    o_k = o_q + D_ATT
    o_v = o_k + D_ATT
    o_f = o_v + D_ATT
    o_g = o_f + H_ATT
    w1 = jnp.concatenate([w[:, 0:o_xbc], w[:, o_xbc:o_dt], w[:, o_za:o_q], w[:, o_g:]],
                         axis=1).astype(BF16)
    att_scale = 1.0 / (ATT_HEAD_DIM ** 0.5)
    w_qkv = jnp.concatenate([w[:, o_q:o_k] * att_scale, w[:, o_k:o_v], w[:, o_v:o_f]],
                            axis=1).astype(BF16)
    w_small = jnp.concatenate(
        [w[:, o_dt:o_za], w[:, o_f:o_g],
         jnp.zeros((d, LANES - H_SSD - H_ATT), F32)], axis=1).astype(BF16)
    g_pre = norm_pre[0][None, :]
    pad_lanes = jnp.zeros((LANES - H_SSD - H_ATT,), F32)
    brow = jnp.concatenate([dt_bias[0], fgate_bias[0], pad_lanes])[None, :]
    alog_row = jnp.concatenate([a_log[0], jnp.zeros((LANES - H_SSD,), F32)])[None, :]
    dskip_row = jnp.repeat(d_skip[0], SSD_HEAD_DIM)[None, :]
    head_of_lane = jnp.arange(D_SSD, dtype=jnp.int32) // SSD_HEAD_DIM
    expand_mat = (jnp.arange(LANES, dtype=jnp.int32)[:, None] == head_of_lane[None, :]).astype(BF16)
    convw = conv_w[0]
    convb = conv_b[0][None, :]
    wps = w_proj_ssd[0].astype(BF16)
    wpa = w_proj_att[0].astype(BF16)
    wout = w_out[0].astype(BF16)
    snorm = ssd_norm[0][None, :]
    gbias = gate_bias[0][None, :]
    npost = norm_post[0][None, :]

    meta_blk = jnp.concatenate([jnp.zeros((CHUNK - N_META, d), F32), meta_tokens.astype(F32)], axis=0)
    p1_m, small_m = _in_proj(meta_blk, g_pre, w1, w_small, F32, CHUNK, 2048, "in_proj_meta")
    (qkv_m,) = _in_proj(meta_blk, g_pre, w_qkv, None, BF16, CHUNK, 3 * D_ATT, "in_proj_qkv_meta")
    st_zero = jnp.zeros((SSD_GROUPS, D_STATE, GROUP_WIDTH), F32)
    tail_zero = jnp.zeros((TAIL_ROWS, CONV_DIM), F32)
    c_zero = jnp.zeros((1, LANES), F32)
    _, _, ct_m, st_m, tail_m, cc_m = _ssd(
        p1_m[None], small_m[None], convw, convb, brow, alog_row, dskip_row, expand_mat,
        st_zero, tail_zero, c_zero, CHUNK - N_META, "ssd_meta")

    x2d = x.reshape(bsz * seq, d)
    p1, small = _in_proj(x2d, g_pre, w1, w_small, F32, 1024, 2048, "in_proj")
    (qkv,) = _in_proj(x2d, g_pre, w_qkv, None, BF16, 1024, 3 * D_ATT, "in_proj_qkv")
    y, c, ct, _, _, _ = _ssd(
        p1.reshape(bsz, seq, P1_COLS), small.reshape(bsz, seq, LANES), convw, convb, brow,
        alog_row, dskip_row, expand_mat, st_m[0], tail_m[0], cc_m[0], 0, "ssd")
    o = _attention(qkv.reshape(bsz, seq, 3 * D_ATT), c, ct, qkv_m, ct_m[0, 0], 256, 256, "fox_attention")
    out = _merge(y.reshape(bsz * seq, D_SSD), p1, o.reshape(bsz * seq, D_ATT), x2d,
                 wps, wpa, wout, snorm, gbias, npost, 256, "merge_out")
    return out.reshape(bsz, seq, d)
```

```python
import functools

import jax
import jax.numpy as jnp
from jax import lax
from jax.experimental import pallas as pl
from jax.experimental.pallas import tpu as pltpu

F32 = jnp.float32
BF16 = jnp.bfloat16

D_MODEL = 1024
N_META = 16
CHUNK = 128
LANES = 128
D_SSD = 2 * D_MODEL
SSD_HEAD_DIM = 64
H_SSD = D_SSD // SSD_HEAD_DIM
SSD_GROUPS = 4
HEADS_PER_GROUP = H_SSD // SSD_GROUPS
D_STATE = 128
GROUP_WIDTH = HEADS_PER_GROUP * SSD_HEAD_DIM
CONV_K = 4
CONV_DIM = D_SSD + 2 * SSD_GROUPS * D_STATE
H_ATT = 16
ATT_HEAD_DIM = 64
D_ATT = H_ATT * ATT_HEAD_DIM
EPS = 1e-6
LOG2E = 1.4426950408889634
TAIL_ROWS = 8
F_LANE0 = H_SSD
P1_COLS = 8 * 1024
VMEM_LIMIT = 56 * 1024 * 1024

_NT = (((1,), (1,)), ((), ()))


def _split3(v):
    hi = v.astype(BF16)
    r1 = v - hi.astype(F32)
    mid = r1.astype(BF16)
    lo = (r1 - mid.astype(F32)).astype(BF16)
    return hi, mid, lo


def _dot_exact_lhs(a_bf16, v):
    hi, mid, lo = _split3(v)
    d = functools.partial(jnp.dot, preferred_element_type=F32)
    return d(a_bf16, hi) + d(a_bf16, mid) + d(a_bf16, lo)


def _dot_exact_rhs(v, b_bf16):
    hi, mid, lo = _split3(v)
    d = functools.partial(jnp.dot, preferred_element_type=F32)
    return d(hi, b_bf16) + d(mid, b_bf16) + d(lo, b_bf16)


def _in_proj_kernel(x_ref, g_ref, w_ref, *rest, with_small):
    if with_small:
        ws_ref, o_ref, os_ref, u_ref = rest
    else:
        o_ref, u_ref = rest
    j = pl.program_id(1)

    @pl.when(j == 0)
    def _():
        x = x_ref[...]
        ms = jnp.mean(x * x, axis=-1, keepdims=True)
        u = (x * lax.rsqrt(ms + EPS) * g_ref[...]).astype(BF16)
        u_ref[...] = u
        if with_small:
            os_ref[...] = jnp.dot(u, ws_ref[...], preferred_element_type=F32)

    o_ref[...] = jnp.dot(u_ref[...], w_ref[...],
                         preferred_element_type=F32).astype(o_ref.dtype)


def _in_proj(x2d, g_row, w, w_small, out_dtype, tm, tn, name):
    m, d = x2d.shape
    n = w.shape[1]
    with_small = w_small is not None
    in_specs = [pl.BlockSpec((tm, d), lambda i, j: (i, 0)),
                pl.BlockSpec((1, d), lambda i, j: (0, 0)),
                pl.BlockSpec((d, tn), lambda i, j: (0, j))]
    out_shape = [jax.ShapeDtypeStruct((m, n), out_dtype)]
    out_specs = [pl.BlockSpec((tm, tn), lambda i, j: (i, j))]
    args = [x2d, g_row, w]
    if with_small:
        in_specs.append(pl.BlockSpec((d, LANES), lambda i, j: (0, 0)))
        out_shape.append(jax.ShapeDtypeStruct((m, LANES), F32))
        out_specs.append(pl.BlockSpec((tm, LANES), lambda i, j: (i, 0)))
        args.append(w_small)
    return pl.pallas_call(
        functools.partial(_in_proj_kernel, with_small=with_small),
        grid=(m // tm, n // tn),
        in_specs=in_specs,
        out_specs=out_specs,
        out_shape=out_shape,
        scratch_shapes=[pltpu.VMEM((tm, d), BF16)],
        compiler_params=pltpu.CompilerParams(
            dimension_semantics=("parallel", "arbitrary"),
            vmem_limit_bytes=VMEM_LIMIT),
        name=name,
    )(*args)


def _ssd_kernel(x0_ref, x1_ref, bc_ref, s_ref, convw_ref, convb_ref, brow_ref, alog_ref,
                dskip_ref, e_ref, st0_ref, tail0_ref, c0_ref,
                y_ref, c_ref, st_ref, tail_ref, cc_ref,
                ext_ref, act_ref, *, first_valid):
    ci = pl.program_id(1)

    @pl.when(ci == 0)
    def _():
        st_ref[0] = st0_ref[...]
        tail_ref[0] = tail0_ref[...]
        cc_ref[0] = c0_ref[...]

    ext_ref[0:TAIL_ROWS, :] = tail_ref[0]
    ext_ref[TAIL_ROWS:TAIL_ROWS + CHUNK, 0:1024] = x0_ref[0]
    ext_ref[TAIL_ROWS:TAIL_ROWS + CHUNK, 1024:2048] = x1_ref[0]
    ext_ref[TAIL_ROWS:TAIL_ROWS + CHUNK, 2048:3072] = bc_ref[0]
    tail_ref[0] = ext_ref[CHUNK:CHUNK + TAIL_ROWS, :]

    row = lax.broadcasted_iota(jnp.int32, (CHUNK, 1), 0)
    valid_row = row >= first_valid
    base = TAIL_ROWS - (CONV_K - 1)
    for j in range(CONV_DIM // GROUP_WIDTH):
        sl = slice(j * GROUP_WIDTH, (j + 1) * GROUP_WIDTH)
        acc = convb_ref[:, sl] + convw_ref[0:1, sl] * ext_ref[base:base + CHUNK, sl]
        for k in range(1, CONV_K):
            acc = acc + convw_ref[k:k + 1, sl] * ext_ref[base + k:base + k + CHUNK, sl]
        a = acc * jax.nn.sigmoid(acc)
        if first_valid:
            a = jnp.where(valid_row, a, 0.0)
        act_ref[:, sl] = a

    lane = lax.broadcasted_iota(jnp.int32, (1, LANES), 1)
    is_dt = lane < H_SSD
    is_f = (lane >= F_LANE0) & (lane < F_LANE0 + H_ATT)
    sv = s_ref[0] + brow_ref[...]
    z = jnp.where(is_dt, sv, -sv)
    sp = jnp.maximum(z, 0.0) + jnp.log1p(jnp.exp(-jnp.abs(z)))
    a_row = -jnp.exp(alog_ref[...])
    vals = jnp.where(is_dt, sp * a_row, jnp.where(is_f, -sp, 0.0))
    dt = jnp.where(is_dt, sp, 0.0)
    if first_valid:
        vals = jnp.where(valid_row, vals, 0.0)
        dt = jnp.where(valid_row, dt, 0.0)

    ri = lax.broadcasted_iota(jnp.int32, (CHUNK, CHUNK), 0)
    cj = lax.broadcasted_iota(jnp.int32, (CHUNK, CHUNK), 1)
    tri = cj <= ri
    cum = _dot_exact_lhs(tri.astype(BF16), vals)
    cfull = cum + cc_ref[0]
    c_ref[0] = cfull
    cum_t = cfull.T
    cc_ref[0] = jnp.where(is_f, cfull[CHUNK - 1:CHUNK, :], 0.0)
    dt_t = dt.T

    a_last = cum[CHUNK - 1:CHUNK, :]
    w = dt * jnp.exp(a_last - cum)
    ea = jnp.where(is_dt, jnp.exp(cum), 0.0)
    cd = jnp.where(is_dt, jnp.exp(a_last), 0.0)
    stacked = jnp.concatenate([w, ea, jnp.broadcast_to(cd, (8, LANES))], axis=0)
    expanded = _dot_exact_rhs(stacked, e_ref[...])
    w_e = expanded[0:CHUNK]
    ea_e = expanded[CHUNK:2 * CHUNK]
    cd_e = expanded[2 * CHUNK:2 * CHUNK + 1]

    lo_half = lane < SSD_HEAD_DIM
    neg_inf = jnp.float32(-jnp.inf)
    for g in range(SSD_GROUPS):
        gsl = slice(g * GROUP_WIDTH, (g + 1) * GROUP_WIDTH)
        bm = act_ref[:, D_SSD + g * D_STATE:D_SSD + (g + 1) * D_STATE]
        cm = act_ref[:, D_SSD + SSD_GROUPS * D_STATE + g * D_STATE:
                     D_SSD + SSD_GROUPS * D_STATE + (g + 1) * D_STATE]
        bm16 = bm.astype(BF16)
        cm16 = cm.astype(BF16)
        cb = lax.dot_general(cm16, bm16, _NT, preferred_element_type=F32)
        hin = st_ref[0, g]
        xw = (act_ref[:, gsl] * w_e[:, gsl]).astype(BF16)
        st_ref[0, g] = hin * cd_e[:, gsl] + jnp.dot(bm.T.astype(BF16), xw,
                                                    preferred_element_type=F32)
        yoff = jnp.dot(cm16, hin.astype(BF16), preferred_element_type=F32) * ea_e[:, gsl]
        for pr in range(HEADS_PER_GROUP // 2):
            psl = slice(g * GROUP_WIDTH + pr * LANES, g * GROUP_WIDTH + (pr + 1) * LANES)
            xs_p = act_ref[:, psl]
            ms = []
            for h in (g * HEADS_PER_GROUP + 2 * pr, g * HEADS_PER_GROUP + 2 * pr + 1):
                seg = cum[:, h:h + 1] - cum_t[h:h + 1, :]
                dec = jnp.exp(jnp.where(tri, seg, neg_inf))
                ms.append((cb * dec * dt_t[h:h + 1, :]).astype(BF16))
            m2 = jnp.concatenate(ms, axis=1)
            xs16 = xs_p.astype(BF16)
            zero16 = jnp.zeros_like(xs16)
            rhs = jnp.concatenate([jnp.where(lo_half, xs16, zero16),
                                   jnp.where(lo_half, zero16, xs16)], axis=0)
            ydiag = jnp.dot(m2, rhs, preferred_element_type=F32)
            y_ref[0, :, psl] = (ydiag + yoff[:, pr * LANES:(pr + 1) * LANES]
                                + dskip_ref[:, psl] * xs_p)


def _ssd(p1, small, convw, convb, brow, alog_row, dskip_row, expand_mat, st0, tail0, c0,
         first_valid, name):
    b, s, _ = p1.shape
    nc = s // CHUNK
    const = lambda shape: pl.BlockSpec(shape, lambda bi, ci: (0,) * len(shape))
    in_specs = [
        pl.BlockSpec((1, CHUNK, 1024), lambda bi, ci: (bi, ci, 2)),
        pl.BlockSpec((1, CHUNK, 1024), lambda bi, ci: (bi, ci, 3)),
        pl.BlockSpec((1, CHUNK, 1024), lambda bi, ci: (bi, ci, 4)),
        pl.BlockSpec((1, CHUNK, LANES), lambda bi, ci: (bi, ci, 0)),
        const((CONV_K, CONV_DIM)), const((1, CONV_DIM)), const((1, LANES)), const((1, LANES)),
        const((1, D_SSD)), const((LANES, D_SSD)),
        const((SSD_GROUPS, D_STATE, GROUP_WIDTH)), const((TAIL_ROWS, CONV_DIM)), const((1, LANES)),
    ]
    out_shape = [
        jax.ShapeDtypeStruct((b, s, D_SSD), F32),
        jax.ShapeDtypeStruct((b, s, LANES), F32),
        jax.ShapeDtypeStruct((b, SSD_GROUPS, D_STATE, GROUP_WIDTH), F32),
        jax.ShapeDtypeStruct((b, TAIL_ROWS, CONV_DIM), F32),
        jax.ShapeDtypeStruct((b, 1, LANES), F32),
    ]
    out_specs = [
        pl.BlockSpec((1, CHUNK, D_SSD), lambda bi, ci: (bi, ci, 0)),
        pl.BlockSpec((1, CHUNK, LANES), lambda bi, ci: (bi, ci, 0)),
        pl.BlockSpec((1, SSD_GROUPS, D_STATE, GROUP_WIDTH), lambda bi, ci: (bi, 0, 0, 0)),
        pl.BlockSpec((1, TAIL_ROWS, CONV_DIM), lambda bi, ci: (bi, 0, 0)),
        pl.BlockSpec((1, 1, LANES), lambda bi, ci: (bi, 0, 0)),
    ]
    return pl.pallas_call(
        functools.partial(_ssd_kernel, first_valid=first_valid),
        grid=(b, nc),
        in_specs=in_specs,
        out_specs=out_specs,
        out_shape=out_shape,
        scratch_shapes=[pltpu.VMEM((TAIL_ROWS + CHUNK, CONV_DIM), F32),
                        pltpu.VMEM((CHUNK, CONV_DIM), F32)],
        compiler_params=pltpu.CompilerParams(
            dimension_semantics=("parallel", "arbitrary"),
            vmem_limit_bytes=VMEM_LIMIT),
        name=name,
    )(p1, p1, p1, small, convw, convb, brow, alog_row, dskip_row, expand_mat, st0, tail0, c0)


def _head_column(cblk, head_lane, lane):
    return jnp.sum(jnp.where(lane == head_lane, cblk, 0.0), axis=1, keepdims=True)


def _augment(x16, col, lane, hh, is_query):
    hi, mid, lo = _split3(col)
    one = jnp.ones_like(col)
    three = (hi.astype(F32), mid.astype(F32), lo.astype(F32))
    terms = three + (one, one, one) if is_query else (one, one, one) + three
    base = (1 - hh) * ATT_HEAD_DIM
    ext = jnp.zeros(x16.shape, F32)
    for t, v in enumerate(terms):
        ext = jnp.where(lane == base + t, v, ext)
    own = (lane < ATT_HEAD_DIM) if hh == 0 else (lane >= ATT_HEAD_DIM)
    return jnp.where(own, x16, ext.astype(BF16))


def _attn_kernel(q_ref, k_ref, v_ref, cq_ref, ck_ref, km_ref, vm_ref, ckm_ref, o_ref,
                 kaug_scr, vsel_scr, kaugm_scr, vselm_scr, qa_scr, s_scr, p_scr, alpha_scr,
                 m_scr, l_scr, acc_scr, *, tq, tk, meta_first_valid):
    hp = pl.program_id(1)
    qi = pl.program_id(2)
    seq = k_ref.shape[1]
    lane = lax.broadcasted_iota(jnp.int32, (1, LANES), 1)
    lo_half = lane < ATT_HEAD_DIM
    neg_inf = jnp.float32(-jnp.inf)

    @pl.when(qi == 0)
    def _():
        def prep(kb, vb, cb, store_k, store_v):
            zv = jnp.zeros_like(vb)
            for hh in range(2):
                col = -LOG2E * _head_column(cb, F_LANE0 + 2 * hp + hh, lane)
                store_k(hh, _augment(kb, col, lane, hh, False))
                store_v(hh, jnp.where(lo_half, vb, zv) if hh == 0 else jnp.where(lo_half, zv, vb))

        def chunk(r, carry):
            rows = pl.ds(pl.multiple_of(r * tk, tk), tk)

            def store_k(hh, val):
                kaug_scr[hh, rows, :] = val

            def store_v(hh, val):
                vsel_scr[hh, rows, :] = val

            prep(k_ref[0, rows, :], v_ref[0, rows, :], ck_ref[0, rows, :], store_k, store_v)
            return carry

        lax.fori_loop(0, seq // tk, chunk, 0)

        def store_km(hh, val):
            kaugm_scr[hh] = val

        def store_vm(hh, val):
            vselm_scr[hh] = val

        prep(km_ref[...], vm_ref[...], ckm_ref[...], store_km, store_vm)

    q = q_ref[0]
    cq = cq_ref[0]
    for hh in range(2):
        qa_scr[hh] = _augment(q, LOG2E * _head_column(cq, F_LANE0 + 2 * hp + hh, lane),
                              lane, hh, True)

    m_scr[...] = jnp.full(m_scr.shape, neg_inf, F32)
    l_scr[...] = jnp.zeros(l_scr.shape, F32)
    acc_scr[...] = jnp.zeros(acc_scr.shape, F32)

    def softmax_update(hh, s):
        m_prev = m_scr[hh]
        m_next = jnp.maximum(m_prev, jnp.max(s, axis=1, keepdims=True))
        alpha = jnp.exp2(m_prev - m_next)
        p = jnp.exp2(s - jnp.concatenate([m_next] * (s.shape[1] // LANES), axis=1))
        l_scr[hh] = alpha * l_scr[hh] + jnp.sum(p, axis=1, keepdims=True)
        m_scr[hh] = m_next
        return alpha, p.astype(BF16)

    kpos = lax.broadcasted_iota(jnp.int32, (1, CHUNK), 1)
    alphas, pvs = [], []
    for hh in range(2):
        s = lax.dot_general(qa_scr[hh], kaugm_scr[hh], _NT, preferred_element_type=F32)
        alpha, p = softmax_update(hh, jnp.where(kpos >= meta_first_valid, s, neg_inf))
        alphas.append(alpha)
        pvs.append(jnp.dot(p, vselm_scr[hh], preferred_element_type=F32))
    acc_scr[...] = acc_scr[...] * jnp.where(lo_half, alphas[0], alphas[1]) + (pvs[0] + pvs[1])

    def stage_a(j):
        rows = pl.ds(pl.multiple_of(j * tk, tk), tk)
        for hh in range(2):
            s_scr[hh] = lax.dot_general(qa_scr[hh], kaug_scr[hh, rows, :], _NT,
                                        preferred_element_type=F32)

    def stage_b(mask):
        for hh in range(2):
            s = s_scr[hh]
            if mask is not None:
                s = jnp.where(mask, s, neg_inf)
            alpha, p = softmax_update(hh, s)
            alpha_scr[hh] = alpha
            p_scr[hh] = p

    def stage_c(j):
        rows = pl.ds(pl.multiple_of(j * tk, tk), tk)
        pv = (jnp.dot(p_scr[0], vsel_scr[0, rows, :], preferred_element_type=F32)
              + jnp.dot(p_scr[1], vsel_scr[1, rows, :], preferred_element_type=F32))
        acc_scr[...] = acc_scr[...] * jnp.where(lo_half, alpha_scr[0], alpha_scr[1]) + pv

    p_scr[...] = jnp.zeros(p_scr.shape, BF16)
    alpha_scr[...] = jnp.ones(alpha_scr.shape, F32)
    stage_a(0)

    def body(i, carry):
        stage_c(jnp.maximum(i - 1, 0))
        stage_b(None)
        stage_a(i + 1)
        return carry

    lax.fori_loop(0, qi, body, 0)

    stage_c(jnp.maximum(qi - 1, 0))
    rr = lax.broadcasted_iota(jnp.int32, (tq, tk), 0)
    cc = lax.broadcasted_iota(jnp.int32, (tq, tk), 1)
    stage_b(cc <= rr)
    stage_c(qi)

    o_ref[0] = acc_scr[...] / jnp.where(lo_half, l_scr[0], l_scr[1])


def _attention(qkv, c, qkv_meta, c_meta, tq, tk, name):
    b, s, _ = qkv.shape
    npairs = H_ATT // 2
    in_specs = [
        pl.BlockSpec((1, tq, LANES), lambda bi, hp, qi: (bi, qi, hp)),
        pl.BlockSpec((1, s, LANES), lambda bi, hp, qi: (bi, 0, npairs + hp)),
        pl.BlockSpec((1, s, LANES), lambda bi, hp, qi: (bi, 0, 2 * npairs + hp)),
        pl.BlockSpec((1, tq, LANES), lambda bi, hp, qi: (bi, qi, 0)),
        pl.BlockSpec((1, s, LANES), lambda bi, hp, qi: (bi, 0, 0)),
        pl.BlockSpec((CHUNK, LANES), lambda bi, hp, qi: (0, npairs + hp)),
        pl.BlockSpec((CHUNK, LANES), lambda bi, hp, qi: (0, 2 * npairs + hp)),
        pl.BlockSpec((CHUNK, LANES), lambda bi, hp, qi: (0, 0)),
    ]
    return pl.pallas_call(
        functools.partial(_attn_kernel, tq=tq, tk=tk, meta_first_valid=CHUNK - N_META),
        grid=(b, npairs, s // tq),
        in_specs=in_specs,
        out_specs=pl.BlockSpec((1, tq, LANES), lambda bi, hp, qi: (bi, qi, hp)),
        out_shape=jax.ShapeDtypeStruct((b, s, D_ATT), F32),
        scratch_shapes=[pltpu.VMEM((2, s, LANES), BF16),
                        pltpu.VMEM((2, s, LANES), BF16),
                        pltpu.VMEM((2, CHUNK, LANES), BF16),
                        pltpu.VMEM((2, CHUNK, LANES), BF16),
                        pltpu.VMEM((2, tq, LANES), BF16),
                        pltpu.VMEM((2, tq, tk), F32),
                        pltpu.VMEM((2, tq, tk), BF16),
                        pltpu.VMEM((2, tq, LANES), F32),
                        pltpu.VMEM((2, tq, LANES), F32),
                        pltpu.VMEM((2, tq, LANES), F32),
                        pltpu.VMEM((tq, LANES), F32)],
        compiler_params=pltpu.CompilerParams(
            dimension_semantics=("parallel", "parallel", "arbitrary"),
            vmem_limit_bytes=VMEM_LIMIT),
        name=name,
    )(qkv, qkv, qkv, c, c, qkv_meta, qkv_meta, c_meta)


def _merge_kernel(y_ref, zs_ref, o_ref, za_ref, gs_ref, ga_ref, x_ref, wps_ref, wpa_ref,
                  wout_ref, snorm_ref, gbias_ref, npost_ref, out_ref):
    zs = zs_ref[...]
    u = y_ref[...] * (zs * jax.nn.sigmoid(zs))
    parts = []
    for g in range(SSD_GROUPS):
        ug = u[:, g * GROUP_WIDTH:(g + 1) * GROUP_WIDTH]
        ms = jnp.mean(ug * ug, axis=-1, keepdims=True)
        parts.append((ug * lax.rsqrt(ms + EPS)
                      * snorm_ref[:, g * GROUP_WIDTH:(g + 1) * GROUP_WIDTH]).astype(BF16))
    y_ssd = jnp.concatenate(parts, axis=1)
    t_ssd = jnp.dot(y_ssd, wps_ref[...], preferred_element_type=F32)
    za = za_ref[...]
    y_att = (o_ref[...] * (za * jax.nn.sigmoid(za))).astype(BF16)
    t_att = jnp.dot(y_att, wpa_ref[...], preferred_element_type=F32)
    g_ssd = jax.nn.sigmoid(gs_ref[...] + gbias_ref[:, 0:D_MODEL])
    g_att = jax.nn.sigmoid(ga_ref[...] + gbias_ref[:, D_MODEL:2 * D_MODEL])
    merged = (g_ssd * t_ssd + g_att * t_att).astype(BF16)
    t = jnp.dot(merged, wout_ref[...], preferred_element_type=F32)
    ms = jnp.mean(t * t, axis=-1, keepdims=True)
    out_ref[...] = x_ref[...] + t * lax.rsqrt(ms + EPS) * npost_ref[...]


def _merge(y2d, p1, o2d, x2d, wps, wpa, wout, snorm, gbias, npost, tm, name):
    m = x2d.shape[0]
    row_blk = lambda width, col: pl.BlockSpec((tm, width), lambda i: (i, col))
    const = lambda shape: pl.BlockSpec(shape, lambda i: (0, 0))
    in_specs = [
        row_blk(D_SSD, 0),
        row_blk(D_SSD, 0),
        row_blk(D_ATT, 0),
        row_blk(D_ATT, 5),
        row_blk(D_MODEL, 6),
        row_blk(D_MODEL, 7),
        row_blk(D_MODEL, 0),
        const((D_SSD, D_MODEL)), const((D_ATT, D_MODEL)), const((D_MODEL, D_MODEL)),
        const((1, D_SSD)), const((1, 2 * D_MODEL)), const((1, D_MODEL)),
    ]
    return pl.pallas_call(
        _merge_kernel,
        grid=(m // tm,),
        in_specs=in_specs,
        out_specs=pl.BlockSpec((tm, D_MODEL), lambda i: (i, 0)),
        out_shape=jax.ShapeDtypeStruct((m, D_MODEL), F32),
        compiler_params=pltpu.CompilerParams(
            dimension_semantics=("parallel",),
            vmem_limit_bytes=VMEM_LIMIT),
        name=name,
    )(y2d, p1, o2d, p1, p1, p1, x2d, wps, wpa, wout, snorm, gbias, npost)


def kernel(x, meta_tokens, norm_pre, w_in, conv_w, conv_b, dt_bias, a_log, d_skip, ssd_norm,
           fgate_bias, gate_bias, w_proj_ssd, w_proj_att, w_out, norm_post):
    bsz, seq, d = x.shape
    assert d == D_MODEL and seq % 1024 == 0 and norm_pre.shape[0] == 1

    w = w_in[0]
    o_xbc = D_SSD
    o_dt = o_xbc + CONV_DIM
    o_za = o_dt + H_SSD
    o_q = o_za + D_ATT
    o_k = o_q + D_ATT
    o_v = o_k + D_ATT
    o_f = o_v + D_ATT
    o_g = o_f + H_ATT
    w1 = jnp.concatenate([w[:, 0:o_xbc], w[:, o_xbc:o_dt], w[:, o_za:o_q], w[:, o_g:]],
                         axis=1).astype(BF16)
    att_scale = LOG2E / (ATT_HEAD_DIM ** 0.5)
    w_qkv = jnp.concatenate([w[:, o_q:o_k] * att_scale, w[:, o_k:o_v], w[:, o_v:o_f]],
                            axis=1).astype(BF16)
    w_small = jnp.concatenate(
        [w[:, o_dt:o_za], w[:, o_f:o_g],
         jnp.zeros((d, LANES - H_SSD - H_ATT), F32)], axis=1).astype(BF16)
    g_pre = norm_pre[0][None, :]
    pad_lanes = jnp.zeros((LANES - H_SSD - H_ATT,), F32)
    brow = jnp.concatenate([dt_bias[0], fgate_bias[0], pad_lanes])[None, :]
    alog_row = jnp.concatenate([a_log[0], jnp.zeros((LANES - H_SSD,), F32)])[None, :]
    dskip_row = jnp.repeat(d_skip[0], SSD_HEAD_DIM)[None, :]
    head_of_lane = jnp.arange(D_SSD, dtype=jnp.int32) // SSD_HEAD_DIM
    expand_mat = (jnp.arange(LANES, dtype=jnp.int32)[:, None] == head_of_lane[None, :]).astype(BF16)
    convw = conv_w[0]
    convb = conv_b[0][None, :]
    wps = w_proj_ssd[0].astype(BF16)
    wpa = w_proj_att[0].astype(BF16)
    wout = w_out[0].astype(BF16)
    snorm = ssd_norm[0][None, :]
    gbias = gate_bias[0][None, :]
    npost = norm_post[0][None, :]

    meta_blk = jnp.concatenate([jnp.zeros((CHUNK - N_META, d), F32), meta_tokens.astype(F32)], axis=0)
    p1_m, small_m = _in_proj(meta_blk, g_pre, w1, w_small, F32, CHUNK, 2048, "in_proj_meta")
    (qkv_m,) = _in_proj(meta_blk, g_pre, w_qkv, None, BF16, CHUNK, 3 * D_ATT, "in_proj_qkv_meta")
    st_zero = jnp.zeros((SSD_GROUPS, D_STATE, GROUP_WIDTH), F32)
    tail_zero = jnp.zeros((TAIL_ROWS, CONV_DIM), F32)
    c_zero = jnp.zeros((1, LANES), F32)
    _, c_m, st_m, tail_m, cc_m = _ssd(
        p1_m[None], small_m[None], convw, convb, brow, alog_row, dskip_row, expand_mat,
        st_zero, tail_zero, c_zero, CHUNK - N_META, "ssd_meta")

    x2d = x.reshape(bsz * seq, d)
    p1, small = _in_proj(x2d, g_pre, w1, w_small, F32, 1024, 2048, "in_proj")
    (qkv,) = _in_proj(x2d, g_pre, w_qkv, None, BF16, 1024, 3 * D_ATT, "in_proj_qkv")
    y, c, _, _, _ = _ssd(
        p1.reshape(bsz, seq, P1_COLS), small.reshape(bsz, seq, LANES), convw, convb, brow,
        alog_row, dskip_row, expand_mat, st_m[0], tail_m[0], cc_m[0], 0, "ssd")
    o = _attention(qkv.reshape(bsz, seq, 3 * D_ATT), c, qkv_m, c_m[0], 256, 256, "fox_attention")
    out = _merge(y.reshape(bsz * seq, D_SSD), p1, o.reshape(bsz * seq, D_ATT), x2d,
                 wps, wpa, wout, snorm, gbias, npost, 256, "merge_out")
    return out.reshape(bsz, seq, d)
```

```python
import functools

import numpy as np

import jax
import jax.numpy as jnp
from jax import lax
from jax.experimental import pallas as pl
from jax.experimental.pallas import tpu as pltpu

F32 = jnp.float32
BF16 = jnp.bfloat16

D_MODEL = 1024
N_META = 16
CHUNK = 128
LANES = 128
D_SSD = 2 * D_MODEL
SSD_HEAD_DIM = 64
H_SSD = D_SSD // SSD_HEAD_DIM
SSD_GROUPS = 4
HEADS_PER_GROUP = H_SSD // SSD_GROUPS
D_STATE = 128
GROUP_WIDTH = HEADS_PER_GROUP * SSD_HEAD_DIM
CONV_K = 4
CONV_DIM = D_SSD + 2 * SSD_GROUPS * D_STATE
H_ATT = 16
ATT_HEAD_DIM = 64
D_ATT = H_ATT * ATT_HEAD_DIM
EPS = 1e-6
LOG2E = 1.4426950408889634
TAIL_ROWS = 8
F_LANE0 = H_SSD
P1_COLS = 8 * 1024
VMEM_LIMIT = 56 * 1024 * 1024
ATT_BLOCK = 256
ATT_UNROLL = 4
MASK_BIAS = -1.0e30

_NT = (((1,), (1,)), ((), ()))


def _split3(v):
    hi = v.astype(BF16)
    r1 = v - hi.astype(F32)
    mid = r1.astype(BF16)
    lo = (r1 - mid.astype(F32)).astype(BF16)
    return hi, mid, lo


def _dot_exact_lhs(a_bf16, v):
    hi, mid, lo = _split3(v)
    d = functools.partial(jnp.dot, preferred_element_type=F32)
    return d(a_bf16, hi) + d(a_bf16, mid) + d(a_bf16, lo)


def _dot_exact_rhs(v, b_bf16):
    hi, mid, lo = _split3(v)
    d = functools.partial(jnp.dot, preferred_element_type=F32)
    return d(hi, b_bf16) + d(mid, b_bf16) + d(lo, b_bf16)


def _in_proj_kernel(x_ref, g_ref, w_ref, *rest, with_small):
    if with_small:
        ws_ref, o_ref, os_ref, u_ref = rest
    else:
        o_ref, u_ref = rest
    j = pl.program_id(1)

    @pl.when(j == 0)
    def _():
        x = x_ref[...]
        ms = jnp.mean(x * x, axis=-1, keepdims=True)
        u = (x * lax.rsqrt(ms + EPS) * g_ref[...]).astype(BF16)
        u_ref[...] = u
        if with_small:
            os_ref[...] = jnp.dot(u, ws_ref[...], preferred_element_type=F32)

    o_ref[...] = jnp.dot(u_ref[...], w_ref[...],
                         preferred_element_type=F32).astype(o_ref.dtype)


def _in_proj(x2d, g_row, w, w_small, out_dtype, tm, tn, name):
    m, d = x2d.shape
    n = w.shape[1]
    with_small = w_small is not None
    in_specs = [pl.BlockSpec((tm, d), lambda i, j: (i, 0)),
                pl.BlockSpec((1, d), lambda i, j: (0, 0)),
                pl.BlockSpec((d, tn), lambda i, j: (0, j))]
    out_shape = [jax.ShapeDtypeStruct((m, n), out_dtype)]
    out_specs = [pl.BlockSpec((tm, tn), lambda i, j: (i, j))]
    args = [x2d, g_row, w]
    if with_small:
        in_specs.append(pl.BlockSpec((d, LANES), lambda i, j: (0, 0)))
        out_shape.append(jax.ShapeDtypeStruct((m, LANES), F32))
        out_specs.append(pl.BlockSpec((tm, LANES), lambda i, j: (i, 0)))
        args.append(w_small)
    return pl.pallas_call(
        functools.partial(_in_proj_kernel, with_small=with_small),
        grid=(m // tm, n // tn),
        in_specs=in_specs,
        out_specs=out_specs,
        out_shape=out_shape,
        scratch_shapes=[pltpu.VMEM((tm, d), BF16)],
        compiler_params=pltpu.CompilerParams(
            dimension_semantics=("parallel", "arbitrary"),
            vmem_limit_bytes=VMEM_LIMIT),
        name=name,
    )(*args)


def _ssd_kernel(x0_ref, x1_ref, bc_ref, s_ref, convw_ref, convb_ref, brow_ref, alog_ref,
                dskip_ref, e_ref, st0_ref, tail0_ref, c0_ref,
                y_ref, c_ref, st_ref, tail_ref, cc_ref,
                ext_ref, act_ref, *, first_valid):
    ci = pl.program_id(1)

    @pl.when(ci == 0)
    def _():
        st_ref[0] = st0_ref[...]
        tail_ref[0] = tail0_ref[...]
        cc_ref[0] = c0_ref[...]

    ext_ref[0:TAIL_ROWS, :] = tail_ref[0]
    ext_ref[TAIL_ROWS:TAIL_ROWS + CHUNK, 0:1024] = x0_ref[0]
    ext_ref[TAIL_ROWS:TAIL_ROWS + CHUNK, 1024:2048] = x1_ref[0]
    ext_ref[TAIL_ROWS:TAIL_ROWS + CHUNK, 2048:3072] = bc_ref[0]
    tail_ref[0] = ext_ref[CHUNK:CHUNK + TAIL_ROWS, :]

    row = lax.broadcasted_iota(jnp.int32, (CHUNK, 1), 0)
    valid_row = row >= first_valid
    base = TAIL_ROWS - (CONV_K - 1)
    for j in range(CONV_DIM // GROUP_WIDTH):
        sl = slice(j * GROUP_WIDTH, (j + 1) * GROUP_WIDTH)
        acc = convb_ref[:, sl] + convw_ref[0:1, sl] * ext_ref[base:base + CHUNK, sl]
        for k in range(1, CONV_K):
            acc = acc + convw_ref[k:k + 1, sl] * ext_ref[base + k:base + k + CHUNK, sl]
        a = acc * jax.nn.sigmoid(acc)
        if first_valid:
            a = jnp.where(valid_row, a, 0.0)
        act_ref[:, sl] = a

    lane = lax.broadcasted_iota(jnp.int32, (1, LANES), 1)
    is_dt = lane < H_SSD
    is_f = (lane >= F_LANE0) & (lane < F_LANE0 + H_ATT)
    sv = s_ref[0] + brow_ref[...]
    z = jnp.where(is_dt, sv, -sv)
    sp = jnp.maximum(z, 0.0) + jnp.log1p(jnp.exp(-jnp.abs(z)))
    a_row = -jnp.exp(alog_ref[...])
    vals = jnp.where(is_dt, sp * a_row, jnp.where(is_f, -sp, 0.0))
    dt = jnp.where(is_dt, sp, 0.0)
    if first_valid:
        vals = jnp.where(valid_row, vals, 0.0)
        dt = jnp.where(valid_row, dt, 0.0)

    ri = lax.broadcasted_iota(jnp.int32, (CHUNK, CHUNK), 0)
    cj = lax.broadcasted_iota(jnp.int32, (CHUNK, CHUNK), 1)
    tri = cj <= ri
    cum = _dot_exact_lhs(tri.astype(BF16), vals)
    cfull = cum + cc_ref[0]
    c_ref[0] = cfull
    cum_t = cfull.T
    cc_ref[0] = jnp.where(is_f, cfull[CHUNK - 1:CHUNK, :], 0.0)
    dt_t = dt.T

    a_last = cum[CHUNK - 1:CHUNK, :]
    w = dt * jnp.exp(a_last - cum)
    ea = jnp.where(is_dt, jnp.exp(cum), 0.0)
    cd = jnp.where(is_dt, jnp.exp(a_last), 0.0)
    stacked = jnp.concatenate([w, ea, jnp.broadcast_to(cd, (8, LANES))], axis=0)
    expanded = _dot_exact_rhs(stacked, e_ref[...])
    w_e = expanded[0:CHUNK]
    ea_e = expanded[CHUNK:2 * CHUNK]
    cd_e = expanded[2 * CHUNK:2 * CHUNK + 1]

    lo_half = lane < SSD_HEAD_DIM
    neg_inf = jnp.float32(-jnp.inf)
    for g in range(SSD_GROUPS):
        gsl = slice(g * GROUP_WIDTH, (g + 1) * GROUP_WIDTH)
        bm = act_ref[:, D_SSD + g * D_STATE:D_SSD + (g + 1) * D_STATE]
        cm = act_ref[:, D_SSD + SSD_GROUPS * D_STATE + g * D_STATE:
                     D_SSD + SSD_GROUPS * D_STATE + (g + 1) * D_STATE]
        bm16 = bm.astype(BF16)
        cm16 = cm.astype(BF16)
        cb = lax.dot_general(cm16, bm16, _NT, preferred_element_type=F32)
        hin = st_ref[0, g]
        xw = (act_ref[:, gsl] * w_e[:, gsl]).astype(BF16)
        st_ref[0, g] = hin * cd_e[:, gsl] + jnp.dot(bm.T.astype(BF16), xw,
                                                    preferred_element_type=F32)
        yoff = jnp.dot(cm16, hin.astype(BF16), preferred_element_type=F32) * ea_e[:, gsl]
        for pr in range(HEADS_PER_GROUP // 2):
            psl = slice(g * GROUP_WIDTH + pr * LANES, g * GROUP_WIDTH + (pr + 1) * LANES)
            xs_p = act_ref[:, psl]
            ms = []
            for h in (g * HEADS_PER_GROUP + 2 * pr, g * HEADS_PER_GROUP + 2 * pr + 1):
                seg = cum[:, h:h + 1] - cum_t[h:h + 1, :]
                dec = jnp.exp(jnp.where(tri, seg, neg_inf))
                ms.append((cb * dec * dt_t[h:h + 1, :]).astype(BF16))
            m2 = jnp.concatenate(ms, axis=1)
            xs16 = xs_p.astype(BF16)
            zero16 = jnp.zeros_like(xs16)
            rhs = jnp.concatenate([jnp.where(lo_half, xs16, zero16),
                                   jnp.where(lo_half, zero16, xs16)], axis=0)
            ydiag = jnp.dot(m2, rhs, preferred_element_type=F32)
            y_ref[0, :, psl] = (ydiag + yoff[:, pr * LANES:(pr + 1) * LANES]
                                + dskip_ref[:, psl] * xs_p)


def _ssd(p1, small, convw, convb, brow, alog_row, dskip_row, expand_mat, st0, tail0, c0,
         first_valid, name):
    b, s, _ = p1.shape
    nc = s // CHUNK
    const = lambda shape: pl.BlockSpec(shape, lambda bi, ci: (0,) * len(shape))
    in_specs = [
        pl.BlockSpec((1, CHUNK, 1024), lambda bi, ci: (bi, ci, 2)),
        pl.BlockSpec((1, CHUNK, 1024), lambda bi, ci: (bi, ci, 3)),
        pl.BlockSpec((1, CHUNK, 1024), lambda bi, ci: (bi, ci, 4)),
        pl.BlockSpec((1, CHUNK, LANES), lambda bi, ci: (bi, ci, 0)),
        const((CONV_K, CONV_DIM)), const((1, CONV_DIM)), const((1, LANES)), const((1, LANES)),
        const((1, D_SSD)), const((LANES, D_SSD)),
        const((SSD_GROUPS, D_STATE, GROUP_WIDTH)), const((TAIL_ROWS, CONV_DIM)), const((1, LANES)),
    ]
    out_shape = [
        jax.ShapeDtypeStruct((b, s, D_SSD), F32),
        jax.ShapeDtypeStruct((b, s, LANES), F32),
        jax.ShapeDtypeStruct((b, SSD_GROUPS, D_STATE, GROUP_WIDTH), F32),
        jax.ShapeDtypeStruct((b, TAIL_ROWS, CONV_DIM), F32),
        jax.ShapeDtypeStruct((b, 1, LANES), F32),
    ]
    out_specs = [
        pl.BlockSpec((1, CHUNK, D_SSD), lambda bi, ci: (bi, ci, 0)),
        pl.BlockSpec((1, CHUNK, LANES), lambda bi, ci: (bi, ci, 0)),
        pl.BlockSpec((1, SSD_GROUPS, D_STATE, GROUP_WIDTH), lambda bi, ci: (bi, 0, 0, 0)),
        pl.BlockSpec((1, TAIL_ROWS, CONV_DIM), lambda bi, ci: (bi, 0, 0)),
        pl.BlockSpec((1, 1, LANES), lambda bi, ci: (bi, 0, 0)),
    ]
    return pl.pallas_call(
        functools.partial(_ssd_kernel, first_valid=first_valid),
        grid=(b, nc),
        in_specs=in_specs,
        out_specs=out_specs,
        out_shape=out_shape,
        scratch_shapes=[pltpu.VMEM((TAIL_ROWS + CHUNK, CONV_DIM), F32),
                        pltpu.VMEM((CHUNK, CONV_DIM), F32)],
        compiler_params=pltpu.CompilerParams(
            dimension_semantics=("parallel", "arbitrary"),
            vmem_limit_bytes=VMEM_LIMIT),
        name=name,
    )(p1, p1, p1, small, convw, convb, brow, alog_row, dskip_row, expand_mat, st0, tail0, c0)


def _head_column(blk, head_lane, lane):
    return jnp.sum(jnp.where(lane == head_lane, blk, 0.0), axis=1, keepdims=True)


BIAS_ONE_LANE = 6


def _bias_tile(cols, lane):
    tile = jnp.where(lane == BIAS_ONE_LANE, 1.0, 0.0)
    for h, col in enumerate(cols):
        for t, term in enumerate(_split3(col)):
            tile = jnp.where(lane == 3 * h + t, term.astype(F32), tile)
    return tile.astype(BF16)


def _placement_matrices():
    pq = np.zeros((2, 2 * LANES, LANES), np.float32)
    pk = np.zeros((2, 2 * LANES, LANES), np.float32)
    for hh in range(2):
        own0 = hh * ATT_HEAD_DIM
        base = (1 - hh) * ATT_HEAD_DIM
        for l in range(own0, own0 + ATT_HEAD_DIM):
            pq[hh, l, l] = 1.0
            pk[hh, l, l] = 1.0
        for t in range(3):
            pq[hh, LANES + 3 * hh + t, base + t] = 1.0
            pq[hh, LANES + BIAS_ONE_LANE, base + 3 + t] = 1.0
            pk[hh, LANES + BIAS_ONE_LANE, base + t] = 1.0
            pk[hh, LANES + 3 * hh + t, base + 3 + t] = -1.0
    return jnp.asarray(pq, BF16), jnp.asarray(pk.transpose(0, 2, 1), BF16)


def _pair_schedule(nq, unroll):
    assert nq % (2 * unroll) == 0
    slots = [[] for _ in range(unroll)]
    for i in range(nq // 2):
        slots[i % unroll] += [i, nq - 1 - i]
    full = [[(qb, li, kb) for li, qb in enumerate(qbs) for kb in range(qb + 1)] for qbs in slots]
    diag = [[(qb, li, qb + 1) for li, qb in enumerate(qbs)] for qbs in slots]
    g1, g2 = len(full[0]), len(diag[0])
    assert all(len(f) == g1 for f in full) and all(len(d) == g2 for d in diag)
    parts = []
    for sched in (full, diag):
        parts.append([qb * ATT_BLOCK for s in sched for (qb, _, _) in s])
        parts.append([li for s in sched for (_, li, _) in s])
        parts.append([kb for s in sched for (_, _, kb) in s])
    table = np.concatenate([np.asarray(p, np.int32) for p in parts])
    return slots, g1, g2, table


def _attn_kernel(tab_ref, q_ref, k_ref, v_ref, c_ref, km_ref, vm_ref, cm_ref, pq_ref, pk_ref,
                 o_ref, kaug_scr, vsel_scr, qaug_scr, s_scr, p_scr, alpha_scr, m_scr, acc_scr,
                 *, slots, g1, g2, meta_first_valid):
    hp = pl.program_id(1)
    seq = q_ref.shape[1]
    blk = ATT_BLOCK
    unroll = len(slots)
    lane = lax.broadcasted_iota(jnp.int32, (1, LANES), 1)
    lo_half = lane < ATT_HEAD_DIM
    own = (lo_half, jnp.logical_not(lo_half))
    l_lane = (ATT_HEAD_DIM, 0)
    neg_inf = jnp.float32(-jnp.inf)

    def values_operand(vb, hh):
        keep = jnp.where(own[hh], vb, jnp.zeros_like(vb))
        return jnp.where(lane == l_lane[hh], jnp.ones_like(vb), keep)

    def bias_columns(cblk):
        return [LOG2E * _head_column(cblk, F_LANE0 + 2 * hp + hh, lane) for hh in range(2)]

    def keys_operand(kb, bias, hh):
        x = jnp.concatenate([kb, bias], axis=1)
        return lax.dot_general(pk_ref[hh], x, _NT, preferred_element_type=F32).astype(BF16)

    row = lax.broadcasted_iota(jnp.int32, (CHUNK, 1), 0)
    meta_cols = [jnp.where(row >= meta_first_valid, col, -MASK_BIAS)
                 for col in bias_columns(cm_ref[...])]
    meta_bias = _bias_tile(meta_cols, lane)
    pad_bias = _bias_tile([jnp.full((CHUNK, 1), -MASK_BIAS, F32)] * 2, lane)
    for hh in range(2):
        kaug_scr[hh, 0, :, 0:CHUNK] = keys_operand(km_ref[...], meta_bias, hh)
        kaug_scr[hh, 0, :, CHUNK:blk] = keys_operand(jnp.zeros((CHUNK, LANES), BF16), pad_bias, hh)
        vsel_scr[hh, 0:CHUNK, :] = values_operand(vm_ref[...], hh)
        vsel_scr[hh, CHUNK:blk, :] = jnp.zeros((blk - CHUNK, LANES), BF16)

    def prep_chunk(r, carry):
        rows = pl.ds(pl.multiple_of(r * blk, blk), blk)
        krows = pl.ds(pl.multiple_of((r + 1) * blk, blk), blk)
        bias = _bias_tile(bias_columns(c_ref[0, rows, :]), lane)
        xq = jnp.concatenate([q_ref[0, rows, :], bias], axis=1)
        kb = k_ref[0, rows, :]
        vb = v_ref[0, rows, :]
        for hh in range(2):
            qaug_scr[hh, rows, :] = jnp.dot(xq, pq_ref[hh],
                                            preferred_element_type=F32).astype(BF16)
            kaug_scr[hh, r + 1] = keys_operand(kb, bias, hh)
            vsel_scr[hh, krows, :] = values_operand(vb, hh)
        return carry

    lax.fori_loop(0, seq // blk, prep_chunk, 0, unroll=4)

    m_scr[...] = jnp.full(m_scr.shape, neg_inf, F32)
    acc_scr[...] = jnp.zeros(acc_scr.shape, F32)

    rr = lax.broadcasted_iota(jnp.int32, (blk, blk), 0)
    cc = lax.broadcasted_iota(jnp.int32, (blk, blk), 1)
    causal = cc <= rr

    def run(base, ngroups, masked, s_scr, p_scr, alpha_scr):
        def entry(field, u, g):
            return tab_ref[base + (field * unroll + u) * ngroups + g]

        def block_rows(off):
            return pl.ds(pl.multiple_of(off, blk), blk)

        def stage_a(g):
            for u in range(unroll):
                qrows = block_rows(entry(0, u, g))
                kb = entry(2, u, g)
                for hh in range(2):
                    s_scr[u, hh] = jnp.dot(qaug_scr[hh, qrows, :], kaug_scr[hh, kb],
                                           preferred_element_type=F32)

        def stage_b(g):
            for u in range(unroll):
                li = entry(1, u, g)
                for hh in range(2):
                    s = s_scr[u, hh]
                    if masked:
                        s = jnp.where(causal, s, neg_inf)
                    m_prev = m_scr[u, hh, li]
                    m_next = jnp.maximum(m_prev, jnp.max(s, axis=1, keepdims=True))
                    alpha_scr[u, hh] = jnp.exp2(m_prev - m_next)
                    m_scr[u, hh, li] = m_next
                    p_scr[u, hh] = jnp.exp2(
                        s - jnp.concatenate([m_next] * (blk // LANES), axis=1)).astype(BF16)

        def stage_c(g):
            for u in range(unroll):
                li = entry(1, u, g)
                krows = block_rows(entry(2, u, g) * blk)
                for hh in range(2):
                    pv = jnp.dot(p_scr[u, hh], vsel_scr[hh, krows, :], preferred_element_type=F32)
                    acc_scr[u, hh, li] = acc_scr[u, hh, li] * alpha_scr[u, hh] + pv

        stage_a(0)
        stage_b(0)
        stage_a(min(1, ngroups - 1))

        def body(g, carry):
            stage_c(g - 1)
            stage_b(g)
            stage_a(jnp.minimum(g + 1, ngroups - 1))
            return carry

        lax.fori_loop(1, ngroups, body, 0)
        stage_c(ngroups - 1)

    run(0, g1, False, s_scr.at[0], p_scr.at[0], alpha_scr.at[0])
    run(3 * unroll * g1, g2, True, s_scr.at[1], p_scr.at[1], alpha_scr.at[1])

    for u, qbs in enumerate(slots):
        for li, qb in enumerate(qbs):
            outs = []
            for hh in range(2):
                a = acc_scr[u, hh, li]
                outs.append(a / _head_column(a, l_lane[hh], lane))
            o_ref[0, qb * blk:(qb + 1) * blk, :] = jnp.where(lo_half, outs[0], outs[1])


def _attention(qkv, c, qkv_meta, c_meta, name):
    b, s, _ = qkv.shape
    npairs = H_ATT // 2
    blk = ATT_BLOCK
    slots, g1, g2, table = _pair_schedule(s // blk, ATT_UNROLL)
    nloc = len(slots[0])
    seq_blk = lambda col0: pl.BlockSpec((1, s, LANES), lambda bi, hp, tab: (bi, 0, col0 + hp))
    meta_blk = lambda col0: pl.BlockSpec((CHUNK, LANES), lambda bi, hp, tab: (0, col0 + hp))
    grid_spec = pltpu.PrefetchScalarGridSpec(
        num_scalar_prefetch=1,
        grid=(b, npairs),
        in_specs=[
            seq_blk(0), seq_blk(npairs), seq_blk(2 * npairs),
            pl.BlockSpec((1, s, LANES), lambda bi, hp, tab: (bi, 0, 0)),
            meta_blk(npairs), meta_blk(2 * npairs),
            pl.BlockSpec((CHUNK, LANES), lambda bi, hp, tab: (0, 0)),
            pl.BlockSpec((2, 2 * LANES, LANES), lambda bi, hp, tab: (0, 0, 0)),
            pl.BlockSpec((2, LANES, 2 * LANES), lambda bi, hp, tab: (0, 0, 0)),
        ],
        out_specs=pl.BlockSpec((1, s, LANES), lambda bi, hp, tab: (bi, 0, hp)),
        scratch_shapes=[
            pltpu.VMEM((2, s // blk + 1, LANES, blk), BF16),
            pltpu.VMEM((2, s + blk, LANES), BF16),
            pltpu.VMEM((2, s, LANES), BF16),
            pltpu.VMEM((2, ATT_UNROLL, 2, blk, blk), F32),
            pltpu.VMEM((2, ATT_UNROLL, 2, blk, blk), BF16),
            pltpu.VMEM((2, ATT_UNROLL, 2, blk, LANES), F32),
            pltpu.VMEM((ATT_UNROLL, 2, nloc, blk, LANES), F32),
            pltpu.VMEM((ATT_UNROLL, 2, nloc, blk, LANES), F32),
        ])
    return pl.pallas_call(
        functools.partial(_attn_kernel, slots=slots, g1=g1, g2=g2,
                          meta_first_valid=CHUNK - N_META),
        grid_spec=grid_spec,
        out_shape=jax.ShapeDtypeStruct((b, s, D_ATT), F32),
        compiler_params=pltpu.CompilerParams(
            dimension_semantics=("parallel", "arbitrary"),
            vmem_limit_bytes=VMEM_LIMIT),
        name=name,
    )(jnp.asarray(table), qkv, qkv, qkv, c, qkv_meta, qkv_meta, c_meta, *_placement_matrices())


def _merge_kernel(y_ref, zs_ref, o_ref, za_ref, gs_ref, ga_ref, x_ref, wps_ref, wpa_ref,
                  wout_ref, snorm_ref, gbias_ref, npost_ref, out_ref):
    zs = zs_ref[...]
    u = y_ref[...] * (zs * jax.nn.sigmoid(zs))
    parts = []
    for g in range(SSD_GROUPS):
        ug = u[:, g * GROUP_WIDTH:(g + 1) * GROUP_WIDTH]
        ms = jnp.mean(ug * ug, axis=-1, keepdims=True)
        parts.append((ug * lax.rsqrt(ms + EPS)
                      * snorm_ref[:, g * GROUP_WIDTH:(g + 1) * GROUP_WIDTH]).astype(BF16))
    y_ssd = jnp.concatenate(parts, axis=1)
    t_ssd = jnp.dot(y_ssd, wps_ref[...], preferred_element_type=F32)
    za = za_ref[...]
    y_att = (o_ref[...] * (za * jax.nn.sigmoid(za))).astype(BF16)
    t_att = jnp.dot(y_att, wpa_ref[...], preferred_element_type=F32)
    g_ssd = jax.nn.sigmoid(gs_ref[...] + gbias_ref[:, 0:D_MODEL])
    g_att = jax.nn.sigmoid(ga_ref[...] + gbias_ref[:, D_MODEL:2 * D_MODEL])
    merged = (g_ssd * t_ssd + g_att * t_att).astype(BF16)
    t = jnp.dot(merged, wout_ref[...], preferred_element_type=F32)
    ms = jnp.mean(t * t, axis=-1, keepdims=True)
    out_ref[...] = x_ref[...] + t * lax.rsqrt(ms + EPS) * npost_ref[...]


def _merge(y2d, p1, o2d, x2d, wps, wpa, wout, snorm, gbias, npost, tm, name):
    m = x2d.shape[0]
    row_blk = lambda width, col: pl.BlockSpec((tm, width), lambda i: (i, col))
    const = lambda shape: pl.BlockSpec(shape, lambda i: (0, 0))
    in_specs = [
        row_blk(D_SSD, 0),
        row_blk(D_SSD, 0),
        row_blk(D_ATT, 0),
        row_blk(D_ATT, 5),
        row_blk(D_MODEL, 6),
        row_blk(D_MODEL, 7),
        row_blk(D_MODEL, 0),
        const((D_SSD, D_MODEL)), const((D_ATT, D_MODEL)), const((D_MODEL, D_MODEL)),
        const((1, D_SSD)), const((1, 2 * D_MODEL)), const((1, D_MODEL)),
    ]
    return pl.pallas_call(
        _merge_kernel,
        grid=(m // tm,),
        in_specs=in_specs,
        out_specs=pl.BlockSpec((tm, D_MODEL), lambda i: (i, 0)),
        out_shape=jax.ShapeDtypeStruct((m, D_MODEL), F32),
        compiler_params=pltpu.CompilerParams(
            dimension_semantics=("parallel",),
            vmem_limit_bytes=VMEM_LIMIT),
        name=name,
    )(y2d, p1, o2d, p1, p1, p1, x2d, wps, wpa, wout, snorm, gbias, npost)


def kernel(x, meta_tokens, norm_pre, w_in, conv_w, conv_b, dt_bias, a_log, d_skip, ssd_norm,
           fgate_bias, gate_bias, w_proj_ssd, w_proj_att, w_out, norm_post):
    bsz, seq, d = x.shape
    assert d == D_MODEL and seq % 1024 == 0 and norm_pre.shape[0] == 1

    w = w_in[0]
    o_xbc = D_SSD
    o_dt = o_xbc + CONV_DIM
    o_za = o_dt + H_SSD
    o_q = o_za + D_ATT
    o_k = o_q + D_ATT
    o_v = o_k + D_ATT
    o_f = o_v + D_ATT
    o_g = o_f + H_ATT
    w1 = jnp.concatenate([w[:, 0:o_xbc], w[:, o_xbc:o_dt], w[:, o_za:o_q], w[:, o_g:]],
                         axis=1).astype(BF16)
    att_scale = LOG2E / (ATT_HEAD_DIM ** 0.5)
    w_qkv = jnp.concatenate([w[:, o_q:o_k] * att_scale, w[:, o_k:o_v], w[:, o_v:o_f]],
                            axis=1).astype(BF16)
    w_small = jnp.concatenate(
        [w[:, o_dt:o_za], w[:, o_f:o_g],
         jnp.zeros((d, LANES - H_SSD - H_ATT), F32)], axis=1).astype(BF16)
    g_pre = norm_pre[0][None, :]
    pad_lanes = jnp.zeros((LANES - H_SSD - H_ATT,), F32)
    brow = jnp.concatenate([dt_bias[0], fgate_bias[0], pad_lanes])[None, :]
    alog_row = jnp.concatenate([a_log[0], jnp.zeros((LANES - H_SSD,), F32)])[None, :]
    dskip_row = jnp.repeat(d_skip[0], SSD_HEAD_DIM)[None, :]
    head_of_lane = jnp.arange(D_SSD, dtype=jnp.int32) // SSD_HEAD_DIM
    expand_mat = (jnp.arange(LANES, dtype=jnp.int32)[:, None] == head_of_lane[None, :]).astype(BF16)
    convw = conv_w[0]
    convb = conv_b[0][None, :]
    wps = w_proj_ssd[0].astype(BF16)
    wpa = w_proj_att[0].astype(BF16)
    wout = w_out[0].astype(BF16)
    snorm = ssd_norm[0][None, :]
    gbias = gate_bias[0][None, :]
    npost = norm_post[0][None, :]

    meta_blk = jnp.concatenate([jnp.zeros((CHUNK - N_META, d), F32), meta_tokens.astype(F32)], axis=0)
    p1_m, small_m = _in_proj(meta_blk, g_pre, w1, w_small, F32, CHUNK, 2048, "in_proj_meta")
    (qkv_m,) = _in_proj(meta_blk, g_pre, w_qkv, None, BF16, CHUNK, 3 * D_ATT, "in_proj_qkv_meta")
    st_zero = jnp.zeros((SSD_GROUPS, D_STATE, GROUP_WIDTH), F32)
    tail_zero = jnp.zeros((TAIL_ROWS, CONV_DIM), F32)
    c_zero = jnp.zeros((1, LANES), F32)
    _, c_m, st_m, tail_m, cc_m = _ssd(
        p1_m[None], small_m[None], convw, convb, brow, alog_row, dskip_row, expand_mat,
        st_zero, tail_zero, c_zero, CHUNK - N_META, "ssd_meta")

    x2d = x.reshape(bsz * seq, d)
    p1, small = _in_proj(x2d, g_pre, w1, w_small, F32, 1024, 2048, "in_proj")
    (qkv,) = _in_proj(x2d, g_pre, w_qkv, None, BF16, 1024, 3 * D_ATT, "in_proj_qkv")
    y, c, _, _, _ = _ssd(
        p1.reshape(bsz, seq, P1_COLS), small.reshape(bsz, seq, LANES), convw, convb, brow,
        alog_row, dskip_row, expand_mat, st_m[0], tail_m[0], cc_m[0], 0, "ssd")
    o = _attention(qkv.reshape(bsz, seq, 3 * D_ATT), c, qkv_m, c_m[0], "fox_attention")
    out = _merge(y.reshape(bsz * seq, D_SSD), p1, o.reshape(bsz * seq, D_ATT), x2d,
                 wps, wpa, wout, snorm, gbias, npost, 256, "merge_out")
    return out.reshape(bsz, seq, d)
```

```python
import functools

import numpy as np

import jax
import jax.numpy as jnp
from jax import lax
from jax.experimental import pallas as pl
from jax.experimental.pallas import tpu as pltpu

F32 = jnp.float32
BF16 = jnp.bfloat16

D_MODEL = 1024
N_META = 16
CHUNK = 128
LANES = 128
D_SSD = 2 * D_MODEL
SSD_HEAD_DIM = 64
H_SSD = D_SSD // SSD_HEAD_DIM
SSD_GROUPS = 4
HEADS_PER_GROUP = H_SSD // SSD_GROUPS
D_STATE = 128
GROUP_WIDTH = HEADS_PER_GROUP * SSD_HEAD_DIM
CONV_K = 4
CONV_DIM = D_SSD + 2 * SSD_GROUPS * D_STATE
H_ATT = 16
ATT_HEAD_DIM = 64
D_ATT = H_ATT * ATT_HEAD_DIM
EPS = 1e-6
LOG2E = 1.4426950408889634
TAIL_ROWS = 16
F_LANE0 = H_SSD
P_COLS = 11 * 1024
P_TILE_N = P_COLS // 4
Q_BLOCK0 = 8 * (1024 // LANES)
VMEM_LIMIT = 56 * 1024 * 1024
ATT_BLOCK = 256
ATT_UNROLL = 4
MASK_BIAS = -1.0e30

_NT = (((1,), (1,)), ((), ()))


def _split3(v):
    hi = v.astype(BF16)
    r1 = v - hi.astype(F32)
    mid = r1.astype(BF16)
    lo = (r1 - mid.astype(F32)).astype(BF16)
    return hi, mid, lo


def _dot_exact_lhs(a_bf16, v):
    hi, mid, lo = _split3(v)
    d = functools.partial(jnp.dot, preferred_element_type=F32)
    return d(a_bf16, hi) + d(a_bf16, mid) + d(a_bf16, lo)


def _dot_exact_rhs(v, b_bf16):
    hi, mid, lo = _split3(v)
    d = functools.partial(jnp.dot, preferred_element_type=F32)
    return d(hi, b_bf16) + d(mid, b_bf16) + d(lo, b_bf16)


def _in_proj_kernel(x_ref, g_ref, w_ref, *rest, with_small):
    if with_small:
        ws_ref, o_ref, os_ref, u_ref = rest
    else:
        o_ref, u_ref = rest
    j = pl.program_id(1)

    @pl.when(j == 0)
    def _():
        x = x_ref[...]
        ms = jnp.mean(x * x, axis=-1, keepdims=True)
        u = (x * lax.rsqrt(ms + EPS) * g_ref[...]).astype(BF16)
        u_ref[...] = u
        if with_small:
            os_ref[...] = jnp.dot(u, ws_ref[...], preferred_element_type=F32)

    o_ref[...] = jnp.dot(u_ref[...], w_ref[...],
                         preferred_element_type=F32).astype(o_ref.dtype)


def _in_proj(x2d, g_row, w, w_small, out_dtype, tm, tn, name):
    m, d = x2d.shape
    n = w.shape[1]
    with_small = w_small is not None
    in_specs = [pl.BlockSpec((tm, d), lambda i, j: (i, 0)),
                pl.BlockSpec((1, d), lambda i, j: (0, 0)),
                pl.BlockSpec((d, tn), lambda i, j: (0, j))]
    out_shape = [jax.ShapeDtypeStruct((m, n), out_dtype)]
    out_specs = [pl.BlockSpec((tm, tn), lambda i, j: (i, j))]
    args = [x2d, g_row, w]
    if with_small:
        in_specs.append(pl.BlockSpec((d, LANES), lambda i, j: (0, 0)))
        out_shape.append(jax.ShapeDtypeStruct((m, LANES), F32))
        out_specs.append(pl.BlockSpec((tm, LANES), lambda i, j: (i, 0)))
        args.append(w_small)
    return pl.pallas_call(
        functools.partial(_in_proj_kernel, with_small=with_small),
        grid=(m // tm, n // tn),
        in_specs=in_specs,
        out_specs=out_specs,
        out_shape=out_shape,
        scratch_shapes=[pltpu.VMEM((tm, d), BF16)],
        compiler_params=pltpu.CompilerParams(
            dimension_semantics=("parallel", "arbitrary"),
            vmem_limit_bytes=VMEM_LIMIT),
        name=name,
    )(*args)


def _ssd_kernel(x0_ref, x1_ref, bc_ref, s_ref, convw_ref, convb_ref, brow_ref, alog_ref,
                dskip_ref, e_ref, shift_ref, st0_ref, tail0_ref, c0_ref,
                y_ref, c_ref, st_ref, tail_ref, cc_ref,
                ext_ref, act_ref, *, first_valid):
    ci = pl.program_id(1)

    @pl.when(ci == 0)
    def _():
        st_ref[0] = st0_ref[...]
        tail_ref[0] = tail0_ref[...]
        cc_ref[0] = c0_ref[...]

    ext_ref[0:TAIL_ROWS, :] = tail_ref[0]
    ext_ref[TAIL_ROWS:TAIL_ROWS + CHUNK, 0:1024] = x0_ref[0]
    ext_ref[TAIL_ROWS:TAIL_ROWS + CHUNK, 1024:2048] = x1_ref[0]
    ext_ref[TAIL_ROWS:TAIL_ROWS + CHUNK, 2048:3072] = bc_ref[0]
    tail_ref[0] = ext_ref[CHUNK:CHUNK + TAIL_ROWS, :]

    row = lax.broadcasted_iota(jnp.int32, (CHUNK, 1), 0)
    valid_row = row >= first_valid
    shift = shift_ref[...]
    for j in range(CONV_DIM // GROUP_WIDTH):
        sl = slice(j * GROUP_WIDTH, (j + 1) * GROUP_WIDTH)
        taps = jnp.dot(shift, ext_ref[:, sl], preferred_element_type=F32)
        acc = (convb_ref[:, sl] + convw_ref[CONV_K - 1:CONV_K, sl]
               * ext_ref[TAIL_ROWS:TAIL_ROWS + CHUNK, sl].astype(F32))
        for k in range(CONV_K - 1):
            acc = acc + convw_ref[k:k + 1, sl] * taps[k * CHUNK:(k + 1) * CHUNK]
        a = acc * jax.nn.sigmoid(acc)
        if first_valid:
            a = jnp.where(valid_row, a, 0.0)
        act_ref[:, sl] = a

    lane = lax.broadcasted_iota(jnp.int32, (1, LANES), 1)
    is_dt = lane < H_SSD
    is_f = (lane >= F_LANE0) & (lane < F_LANE0 + H_ATT)
    sv = s_ref[0] + brow_ref[...]
    z = jnp.where(is_dt, sv, -sv)
    sp = jnp.maximum(z, 0.0) + jnp.log1p(jnp.exp(-jnp.abs(z)))
    a_row = -jnp.exp(alog_ref[...])
    vals = jnp.where(is_dt, sp * a_row, jnp.where(is_f, -sp, 0.0))
    dt = jnp.where(is_dt, sp, 0.0)
    if first_valid:
        vals = jnp.where(valid_row, vals, 0.0)
        dt = jnp.where(valid_row, dt, 0.0)

    ri = lax.broadcasted_iota(jnp.int32, (CHUNK, CHUNK), 0)
    cj = lax.broadcasted_iota(jnp.int32, (CHUNK, CHUNK), 1)
    tri = cj <= ri
    cum = _dot_exact_lhs(tri.astype(BF16), vals)
    cfull = cum + cc_ref[0]
    c_ref[0] = cfull
    cum_t = cfull.T
    cc_ref[0] = jnp.where(is_f, cfull[CHUNK - 1:CHUNK, :], 0.0)
    dt_t = dt.T

    a_last = cum[CHUNK - 1:CHUNK, :]
    w_t = (dt * jnp.exp(a_last - cum)).T
    cd = jnp.where(is_dt, jnp.exp(a_last), 0.0)
    cd_e = _dot_exact_rhs(jnp.broadcast_to(cd, (8, LANES)), e_ref[...])[0:1]

    lo_half = lane < SSD_HEAD_DIM
    neg_inf = jnp.float32(-jnp.inf)

    def split_heads(x16):
        zero16 = jnp.zeros_like(x16)
        return jnp.concatenate([jnp.where(lo_half, x16, zero16),
                                jnp.where(lo_half, zero16, x16)], axis=0)

    for g in range(SSD_GROUPS):
        bm = act_ref[:, D_SSD + g * D_STATE:D_SSD + (g + 1) * D_STATE]
        cm = act_ref[:, D_SSD + SSD_GROUPS * D_STATE + g * D_STATE:
                     D_SSD + SSD_GROUPS * D_STATE + (g + 1) * D_STATE]
        cb = lax.dot_general(cm.astype(BF16), bm.astype(BF16), _NT,
                             preferred_element_type=F32)
        bm_t = bm.T
        for pr in range(HEADS_PER_GROUP // 2):
            psl = slice(g * GROUP_WIDTH + pr * LANES, g * GROUP_WIDTH + (pr + 1) * LANES)
            ssl = slice(pr * LANES, (pr + 1) * LANES)
            xs_p = act_ref[:, psl]
            hin = st_ref[0, g, :, ssl]
            intra, inter, bw = [], [], []
            for h in (g * HEADS_PER_GROUP + 2 * pr, g * HEADS_PER_GROUP + 2 * pr + 1):
                a_col = cum[:, h:h + 1]
                seg = a_col - cum_t[h:h + 1, :]
                dec = jnp.exp(jnp.where(tri, seg, neg_inf))
                intra.append((cb * dec * dt_t[h:h + 1, :]).astype(BF16))
                inter.append((cm * jnp.exp(a_col)).astype(BF16))
                bw.append((bm_t * w_t[h:h + 1, :]).astype(BF16))
            xs2 = split_heads(xs_p.astype(BF16))
            y_lhs = jnp.concatenate(intra + inter, axis=1)
            y_rhs = jnp.concatenate([xs2, split_heads(hin.astype(BF16))], axis=0)
            y_pair = jnp.dot(y_lhs, y_rhs, preferred_element_type=F32)
            y_ref[0, :, psl] = (y_pair + dskip_ref[:, psl] * xs_p).astype(y_ref.dtype)
            st_ref[0, g, :, ssl] = hin * cd_e[:, psl] + jnp.dot(
                jnp.concatenate(bw, axis=1), xs2, preferred_element_type=F32)


def _conv_shift_matrix():
    m = np.zeros(((CONV_K - 1) * CHUNK, TAIL_ROWS + CHUNK), np.float32)
    for k in range(CONV_K - 1):
        for t in range(CHUNK):
            m[k * CHUNK + t, TAIL_ROWS + t - (CONV_K - 1) + k] = 1.0
    return jnp.asarray(m, BF16)


def _ssd(p1, small, convw, convb, brow, alog_row, dskip_row, expand_mat, st0, tail0, c0,
         first_valid, name):
    b, s, _ = p1.shape
    nc = s // CHUNK
    const = lambda shape: pl.BlockSpec(shape, lambda bi, ci: (0,) * len(shape))
    in_specs = [
        pl.BlockSpec((1, CHUNK, 1024), lambda bi, ci: (bi, ci, 2)),
        pl.BlockSpec((1, CHUNK, 1024), lambda bi, ci: (bi, ci, 3)),
        pl.BlockSpec((1, CHUNK, 1024), lambda bi, ci: (bi, ci, 4)),
        pl.BlockSpec((1, CHUNK, LANES), lambda bi, ci: (bi, ci, 0)),
        const((CONV_K, CONV_DIM)), const((1, CONV_DIM)), const((1, LANES)), const((1, LANES)),
        const((1, D_SSD)), const((LANES, D_SSD)), const(((CONV_K - 1) * CHUNK, TAIL_ROWS + CHUNK)),
        const((SSD_GROUPS, D_STATE, GROUP_WIDTH)), const((TAIL_ROWS, CONV_DIM)), const((1, LANES)),
    ]
    out_shape = [
        jax.ShapeDtypeStruct((b, s, D_SSD), BF16),
        jax.ShapeDtypeStruct((b, s, LANES), F32),
        jax.ShapeDtypeStruct((b, SSD_GROUPS, D_STATE, GROUP_WIDTH), F32),
        jax.ShapeDtypeStruct((b, TAIL_ROWS, CONV_DIM), BF16),
        jax.ShapeDtypeStruct((b, 1, LANES), F32),
    ]
    out_specs = [
        pl.BlockSpec((1, CHUNK, D_SSD), lambda bi, ci: (bi, ci, 0)),
        pl.BlockSpec((1, CHUNK, LANES), lambda bi, ci: (bi, ci, 0)),
        pl.BlockSpec((1, SSD_GROUPS, D_STATE, GROUP_WIDTH), lambda bi, ci: (bi, 0, 0, 0)),
        pl.BlockSpec((1, TAIL_ROWS, CONV_DIM), lambda bi, ci: (bi, 0, 0)),
        pl.BlockSpec((1, 1, LANES), lambda bi, ci: (bi, 0, 0)),
    ]
    return pl.pallas_call(
        functools.partial(_ssd_kernel, first_valid=first_valid),
        grid=(b, nc),
        in_specs=in_specs,
        out_specs=out_specs,
        out_shape=out_shape,
        scratch_shapes=[pltpu.VMEM((TAIL_ROWS + CHUNK, CONV_DIM), BF16),
                        pltpu.VMEM((CHUNK, CONV_DIM), F32)],
        compiler_params=pltpu.CompilerParams(
            dimension_semantics=("parallel", "arbitrary"),
            vmem_limit_bytes=VMEM_LIMIT),
        name=name,
    )(p1, p1, p1, small, convw, convb, brow, alog_row, dskip_row, expand_mat,
      _conv_shift_matrix(), st0, tail0, c0)


def _head_column(blk, head_lane, lane):
    return jnp.sum(jnp.where(lane == head_lane, blk, 0.0), axis=1, keepdims=True)


BIAS_ONE_LANE = 6


def _bias_tile(cols, lane):
    tile = jnp.where(lane == BIAS_ONE_LANE, 1.0, 0.0)
    for h, col in enumerate(cols):
        for t, term in enumerate(_split3(col)):
            tile = jnp.where(lane == 3 * h + t, term.astype(F32), tile)
    return tile.astype(BF16)


def _placement_matrices():
    pq = np.zeros((2, 2 * LANES, LANES), np.float32)
    pk = np.zeros((2, 2 * LANES, LANES), np.float32)
    for hh in range(2):
        own0 = hh * ATT_HEAD_DIM
        base = (1 - hh) * ATT_HEAD_DIM
        for l in range(own0, own0 + ATT_HEAD_DIM):
            pq[hh, l, l] = 1.0
            pk[hh, l, l] = 1.0
        for t in range(3):
            pq[hh, LANES + 3 * hh + t, base + t] = 1.0
            pq[hh, LANES + BIAS_ONE_LANE, base + 3 + t] = 1.0
            pk[hh, LANES + BIAS_ONE_LANE, base + t] = 1.0
            pk[hh, LANES + 3 * hh + t, base + 3 + t] = -1.0
    return jnp.asarray(pq, BF16), jnp.asarray(pk.transpose(0, 2, 1), BF16)


def _pair_schedule(nq, unroll):
    assert nq % (2 * unroll) == 0
    slots = [[] for _ in range(unroll)]
    for i in range(nq // 2):
        slots[i % unroll] += [i, nq - 1 - i]
    full = [[(qb, li, kb) for li, qb in enumerate(qbs) for kb in range(qb + 1)] for qbs in slots]
    diag = [[(qb, li, qb + 1) for li, qb in enumerate(qbs)] for qbs in slots]
    g1, g2 = len(full[0]), len(diag[0])
    assert all(len(f) == g1 for f in full) and all(len(d) == g2 for d in diag)
    parts = []
    for sched in (full, diag):
        parts.append([qb * ATT_BLOCK for s in sched for (qb, _, _) in s])
        parts.append([li for s in sched for (_, li, _) in s])
        parts.append([kb for s in sched for (_, _, kb) in s])
    table = np.concatenate([np.asarray(p, np.int32) for p in parts])
    return slots, g1, g2, table


def _attn_kernel(tab_ref, q_ref, k_ref, v_ref, c_ref, km_ref, vm_ref, cm_ref, pq_ref, pk_ref,
                 o_ref, kaug_scr, vsel_scr, qaug_scr, s_scr, p_scr, alpha_scr, m_scr, acc_scr,
                 *, slots, g1, g2, meta_first_valid):
    hp = pl.program_id(1)
    seq = q_ref.shape[1]
    blk = ATT_BLOCK
    unroll = len(slots)
    lane = lax.broadcasted_iota(jnp.int32, (1, LANES), 1)
    lo_half = lane < ATT_HEAD_DIM
    own = (lo_half, jnp.logical_not(lo_half))
    l_lane = (ATT_HEAD_DIM, 0)
    neg_inf = jnp.float32(-jnp.inf)

    def values_operand(vb, hh):
        keep = jnp.where(own[hh], vb, jnp.zeros_like(vb))
        return jnp.where(lane == l_lane[hh], jnp.ones_like(vb), keep)

    def bias_columns(cblk):
        return [LOG2E * _head_column(cblk, F_LANE0 + 2 * hp + hh, lane) for hh in range(2)]

    def keys_operand(kb, bias, hh):
        x = jnp.concatenate([kb, bias], axis=1)
        return lax.dot_general(pk_ref[hh], x, _NT, preferred_element_type=F32).astype(BF16)

    row = lax.broadcasted_iota(jnp.int32, (CHUNK, 1), 0)
    meta_cols = [jnp.where(row >= meta_first_valid, col, -MASK_BIAS)
                 for col in bias_columns(cm_ref[...])]
    meta_bias = _bias_tile(meta_cols, lane)
    pad_bias = _bias_tile([jnp.full((CHUNK, 1), -MASK_BIAS, F32)] * 2, lane)
    for hh in range(2):
        kaug_scr[hh, 0, :, 0:CHUNK] = keys_operand(km_ref[...], meta_bias, hh)
        kaug_scr[hh, 0, :, CHUNK:blk] = keys_operand(jnp.zeros((CHUNK, LANES), BF16), pad_bias, hh)
        vsel_scr[hh, 0:CHUNK, :] = values_operand(vm_ref[...], hh)
        vsel_scr[hh, CHUNK:blk, :] = jnp.zeros((blk - CHUNK, LANES), BF16)

    def prep_chunk(r, carry):
        rows = pl.ds(pl.multiple_of(r * blk, blk), blk)
        krows = pl.ds(pl.multiple_of((r + 1) * blk, blk), blk)
        bias = _bias_tile(bias_columns(c_ref[0, rows, :]), lane)
        xq = jnp.concatenate([q_ref[0, rows, :], bias], axis=1)
        kb = k_ref[0, rows, :]
        vb = v_ref[0, rows, :]
        for hh in range(2):
            qaug_scr[hh, rows, :] = jnp.dot(xq, pq_ref[hh],
                                            preferred_element_type=F32).astype(BF16)
            kaug_scr[hh, r + 1] = keys_operand(kb, bias, hh)
            vsel_scr[hh, krows, :] = values_operand(vb, hh)
        return carry

    lax.fori_loop(0, seq // blk, prep_chunk, 0, unroll=4)

    m_scr[...] = jnp.full(m_scr.shape, neg_inf, F32)
    acc_scr[...] = jnp.zeros(acc_scr.shape, F32)

    rr = lax.broadcasted_iota(jnp.int32, (blk, blk), 0)
    cc = lax.broadcasted_iota(jnp.int32, (blk, blk), 1)
    causal = cc <= rr

    def run(base, ngroups, masked, s_scr, p_scr, alpha_scr):
        def entry(field, u, g):
            return tab_ref[base + (field * unroll + u) * ngroups + g]

        def block_rows(off):
            return pl.ds(pl.multiple_of(off, blk), blk)

        def stage_a(g):
            for u in range(unroll):
                qrows = block_rows(entry(0, u, g))
                kb = entry(2, u, g)
                for hh in range(2):
                    s_scr[u, hh] = jnp.dot(qaug_scr[hh, qrows, :], kaug_scr[hh, kb],
                                           preferred_element_type=F32)

        def stage_b(g):
            for u in range(unroll):
                li = entry(1, u, g)
                for hh in range(2):
                    s = s_scr[u, hh]
                    if masked:
                        s = jnp.where(causal, s, neg_inf)
                    m_prev = m_scr[u, hh, li]
                    m_next = jnp.maximum(m_prev, jnp.max(s, axis=1, keepdims=True))
                    alpha_scr[u, hh] = jnp.exp2(m_prev - m_next)
                    m_scr[u, hh, li] = m_next
                    p_scr[u, hh] = jnp.exp2(
                        s - jnp.concatenate([m_next] * (blk // LANES), axis=1)).astype(BF16)

        def stage_c(g):
            for u in range(unroll):
                li = entry(1, u, g)
                krows = block_rows(entry(2, u, g) * blk)
                for hh in range(2):
                    pv = jnp.dot(p_scr[u, hh], vsel_scr[hh, krows, :], preferred_element_type=F32)
                    acc_scr[u, hh, li] = acc_scr[u, hh, li] * alpha_scr[u, hh] + pv

        stage_a(0)
        stage_b(0)
        stage_a(min(1, ngroups - 1))

        def body(g, carry):
            stage_c(g - 1)
            stage_b(g)
            stage_a(jnp.minimum(g + 1, ngroups - 1))
            return carry

        lax.fori_loop(1, ngroups, body, 0)
        stage_c(ngroups - 1)

    run(0, g1, False, s_scr.at[0], p_scr.at[0], alpha_scr.at[0])
    run(3 * unroll * g1, g2, True, s_scr.at[1], p_scr.at[1], alpha_scr.at[1])

    for u, qbs in enumerate(slots):
        for li, qb in enumerate(qbs):
            outs = []
            for hh in range(2):
                a = acc_scr[u, hh, li]
                outs.append(a / _head_column(a, l_lane[hh], lane))
            o_ref[0, qb * blk:(qb + 1) * blk, :] = jnp.where(lo_half, outs[0],
                                                             outs[1]).astype(o_ref.dtype)


def _attention(qkv, c, qkv_meta, c_meta, name):
    b, s, _ = qkv.shape
    npairs = H_ATT // 2
    blk = ATT_BLOCK
    slots, g1, g2, table = _pair_schedule(s // blk, ATT_UNROLL)
    nloc = len(slots[0])
    seq_blk = lambda col0: pl.BlockSpec((1, s, LANES), lambda bi, hp, tab: (bi, 0, col0 + hp))
    meta_blk = lambda col0: pl.BlockSpec((CHUNK, LANES), lambda bi, hp, tab: (0, col0 + hp))
    grid_spec = pltpu.PrefetchScalarGridSpec(
        num_scalar_prefetch=1,
        grid=(b, npairs),
        in_specs=[
            seq_blk(Q_BLOCK0), seq_blk(Q_BLOCK0 + npairs), seq_blk(Q_BLOCK0 + 2 * npairs),
            pl.BlockSpec((1, s, LANES), lambda bi, hp, tab: (bi, 0, 0)),
            meta_blk(Q_BLOCK0 + npairs), meta_blk(Q_BLOCK0 + 2 * npairs),
            pl.BlockSpec((CHUNK, LANES), lambda bi, hp, tab: (0, 0)),
            pl.BlockSpec((2, 2 * LANES, LANES), lambda bi, hp, tab: (0, 0, 0)),
            pl.BlockSpec((2, LANES, 2 * LANES), lambda bi, hp, tab: (0, 0, 0)),
        ],
        out_specs=pl.BlockSpec((1, s, LANES), lambda bi, hp, tab: (bi, 0, hp)),
        scratch_shapes=[
            pltpu.VMEM((2, s // blk + 1, LANES, blk), BF16),
            pltpu.VMEM((2, s + blk, LANES), BF16),
            pltpu.VMEM((2, s, LANES), BF16),
            pltpu.VMEM((2, ATT_UNROLL, 2, blk, blk), F32),
            pltpu.VMEM((2, ATT_UNROLL, 2, blk, blk), BF16),
            pltpu.VMEM((2, ATT_UNROLL, 2, blk, LANES), F32),
            pltpu.VMEM((ATT_UNROLL, 2, nloc, blk, LANES), F32),
            pltpu.VMEM((ATT_UNROLL, 2, nloc, blk, LANES), F32),
        ])
    return pl.pallas_call(
        functools.partial(_attn_kernel, slots=slots, g1=g1, g2=g2,
                          meta_first_valid=CHUNK - N_META),
        grid_spec=grid_spec,
        out_shape=jax.ShapeDtypeStruct((b, s, D_ATT), BF16),
        compiler_params=pltpu.CompilerParams(
            dimension_semantics=("parallel", "arbitrary"),
            vmem_limit_bytes=VMEM_LIMIT),
        name=name,
    )(jnp.asarray(table), qkv, qkv, qkv, c, qkv_meta, qkv_meta, c_meta, *_placement_matrices())


def _merge_kernel(y_ref, zs_ref, o_ref, za_ref, gs_ref, ga_ref, x_ref, wps_ref, wpa_ref,
                  wout_ref, snorm_ref, gbias_ref, npost_ref, out_ref):
    zs = zs_ref[...].astype(F32)
    u = y_ref[...].astype(F32) * (zs * jax.nn.sigmoid(zs))
    parts = []
    for g in range(SSD_GROUPS):
        ug = u[:, g * GROUP_WIDTH:(g + 1) * GROUP_WIDTH]
        ms = jnp.mean(ug * ug, axis=-1, keepdims=True)
        parts.append((ug * lax.rsqrt(ms + EPS)
                      * snorm_ref[:, g * GROUP_WIDTH:(g + 1) * GROUP_WIDTH]).astype(BF16))
    y_ssd = jnp.concatenate(parts, axis=1)
    t_ssd = jnp.dot(y_ssd, wps_ref[...], preferred_element_type=F32)
    za = za_ref[...].astype(F32)
    y_att = (o_ref[...].astype(F32) * (za * jax.nn.sigmoid(za))).astype(BF16)
    t_att = jnp.dot(y_att, wpa_ref[...], preferred_element_type=F32)
    g_ssd = jax.nn.sigmoid(gs_ref[...].astype(F32) + gbias_ref[:, 0:D_MODEL])
    g_att = jax.nn.sigmoid(ga_ref[...].astype(F32) + gbias_ref[:, D_MODEL:2 * D_MODEL])
    merged = (g_ssd * t_ssd + g_att * t_att).astype(BF16)
    t = jnp.dot(merged, wout_ref[...], preferred_element_type=F32)
    ms = jnp.mean(t * t, axis=-1, keepdims=True)
    out_ref[...] = x_ref[...] + t * lax.rsqrt(ms + EPS) * npost_ref[...]


def _merge(y2d, p1, o2d, x2d, wps, wpa, wout, snorm, gbias, npost, tm, name):
    m = x2d.shape[0]
    row_blk = lambda width, col: pl.BlockSpec((tm, width), lambda i: (i, col))
    const = lambda shape: pl.BlockSpec(shape, lambda i: (0, 0), pipeline_mode=pl.Buffered(1))
    in_specs = [
        row_blk(D_SSD, 0),
        row_blk(D_SSD, 0),
        row_blk(D_ATT, 0),
        row_blk(D_ATT, 5),
        row_blk(D_MODEL, 6),
        row_blk(D_MODEL, 7),
        row_blk(D_MODEL, 0),
        const((D_SSD, D_MODEL)), const((D_ATT, D_MODEL)), const((D_MODEL, D_MODEL)),
        const((1, D_SSD)), const((1, 2 * D_MODEL)), const((1, D_MODEL)),
    ]
    return pl.pallas_call(
        _merge_kernel,
        grid=(m // tm,),
        in_specs=in_specs,
        out_specs=pl.BlockSpec((tm, D_MODEL), lambda i: (i, 0)),
        out_shape=jax.ShapeDtypeStruct((m, D_MODEL), F32),
        compiler_params=pltpu.CompilerParams(
            dimension_semantics=("parallel",),
            vmem_limit_bytes=VMEM_LIMIT),
        name=name,
    )(y2d, p1, o2d, p1, p1, p1, x2d, wps, wpa, wout, snorm, gbias, npost)


def kernel(x, meta_tokens, norm_pre, w_in, conv_w, conv_b, dt_bias, a_log, d_skip, ssd_norm,
           fgate_bias, gate_bias, w_proj_ssd, w_proj_att, w_out, norm_post):
    bsz, seq, d = x.shape
    assert d == D_MODEL and seq % 1024 == 0 and norm_pre.shape[0] == 1

    w = w_in[0]
    o_xbc = D_SSD
    o_dt = o_xbc + CONV_DIM
    o_za = o_dt + H_SSD
    o_q = o_za + D_ATT
    o_k = o_q + D_ATT
    o_v = o_k + D_ATT
    o_f = o_v + D_ATT
    o_g = o_f + H_ATT
    att_scale = LOG2E / (ATT_HEAD_DIM ** 0.5)
    w_all = jnp.concatenate([w[:, 0:o_xbc], w[:, o_xbc:o_dt], w[:, o_za:o_q], w[:, o_g:],
                             w[:, o_q:o_k] * att_scale, w[:, o_k:o_v], w[:, o_v:o_f]],
                            axis=1).astype(BF16)
    w_small = jnp.concatenate(
        [w[:, o_dt:o_za], w[:, o_f:o_g],
         jnp.zeros((d, LANES - H_SSD - H_ATT), F32)], axis=1).astype(BF16)
    g_pre = norm_pre[0][None, :]
    pad_lanes = jnp.zeros((LANES - H_SSD - H_ATT,), F32)
    brow = jnp.concatenate([dt_bias[0], fgate_bias[0], pad_lanes])[None, :]
    alog_row = jnp.concatenate([a_log[0], jnp.zeros((LANES - H_SSD,), F32)])[None, :]
    dskip_row = jnp.repeat(d_skip[0], SSD_HEAD_DIM)[None, :]
    head_of_lane = jnp.arange(D_SSD, dtype=jnp.int32) // SSD_HEAD_DIM
    expand_mat = (jnp.arange(LANES, dtype=jnp.int32)[:, None] == head_of_lane[None, :]).astype(BF16)
    convw = conv_w[0]
    convb = conv_b[0][None, :]
    wps = w_proj_ssd[0].astype(BF16)
    wpa = w_proj_att[0].astype(BF16)
    wout = w_out[0].astype(BF16)
    snorm = ssd_norm[0][None, :]
    gbias = gate_bias[0][None, :]
    npost = norm_post[0][None, :]

    meta_blk = jnp.concatenate([jnp.zeros((CHUNK - N_META, d), F32), meta_tokens.astype(F32)], axis=0)
    p_m, small_m = _in_proj(meta_blk, g_pre, w_all, w_small, BF16, CHUNK, P_TILE_N, "in_proj_meta")
    st_zero = jnp.zeros((SSD_GROUPS, D_STATE, GROUP_WIDTH), F32)
    tail_zero = jnp.zeros((TAIL_ROWS, CONV_DIM), BF16)
    c_zero = jnp.zeros((1, LANES), F32)
    _, c_m, st_m, tail_m, cc_m = _ssd(
        p_m[None], small_m[None], convw, convb, brow, alog_row, dskip_row, expand_mat,
        st_zero, tail_zero, c_zero, CHUNK - N_META, "ssd_meta")

    x2d = x.reshape(bsz * seq, d)
    p, small = _in_proj(x2d, g_pre, w_all, w_small, BF16, 1024, P_TILE_N, "in_proj")
    p3d = p.reshape(bsz, seq, P_COLS)
    y, c, _, _, _ = _ssd(
        p3d, small.reshape(bsz, seq, LANES), convw, convb, brow,
        alog_row, dskip_row, expand_mat, st_m[0], tail_m[0], cc_m[0], 0, "ssd")
    o = _attention(p3d, c, p_m, c_m[0], "fox_attention")
    out = _merge(y.reshape(bsz * seq, D_SSD), p, o.reshape(bsz * seq, D_ATT), x2d,
                 wps, wpa, wout, snorm, gbias, npost, 512, "merge_out")
    return out.reshape(bsz, seq, d)
```

```python
import functools

import numpy as np

import jax
import jax.numpy as jnp
from jax import lax
from jax.experimental import pallas as pl
from jax.experimental.pallas import tpu as pltpu

F32 = jnp.float32
BF16 = jnp.bfloat16

D_MODEL = 1024
N_META = 16
CHUNK = 128
LANES = 128
D_SSD = 2 * D_MODEL
SSD_HEAD_DIM = 64
H_SSD = D_SSD // SSD_HEAD_DIM
SSD_GROUPS = 4
HEADS_PER_GROUP = H_SSD // SSD_GROUPS
D_STATE = 128
GROUP_WIDTH = HEADS_PER_GROUP * SSD_HEAD_DIM
CONV_K = 4
CONV_DIM = D_SSD + 2 * SSD_GROUPS * D_STATE
H_ATT = 16
ATT_HEAD_DIM = 64
D_ATT = H_ATT * ATT_HEAD_DIM
EPS = 1e-6
LOG2E = 1.4426950408889634
TAIL_ROWS = 16
F_LANE0 = H_SSD
P_COLS = 11 * 1024
P_TILE_N = P_COLS // 4
Q_BLOCK0 = 8 * (1024 // LANES)
VMEM_LIMIT = 56 * 1024 * 1024
ATT_BLOCK = 256
ATT_UNROLL = 4
MASK_BIAS = -1.0e30
BIAS_ONE_LANE = 3 * H_ATT

_NT = (((1,), (1,)), ((), ()))


def _split3(v):
    hi = v.astype(BF16)
    r1 = v - hi.astype(F32)
    mid = r1.astype(BF16)
    lo = (r1 - mid.astype(F32)).astype(BF16)
    return hi, mid, lo


def _dot_exact_lhs(a_bf16, v):
    hi, mid, lo = _split3(v)
    d = functools.partial(jnp.dot, preferred_element_type=F32)
    return d(a_bf16, hi) + d(a_bf16, mid) + d(a_bf16, lo)


def _dot_exact_rhs(v, b_bf16):
    hi, mid, lo = _split3(v)
    d = functools.partial(jnp.dot, preferred_element_type=F32)
    return d(hi, b_bf16) + d(mid, b_bf16) + d(lo, b_bf16)


def _in_proj_kernel(x_ref, g_ref, w_ref, *rest, with_small):
    if with_small:
        ws_ref, o_ref, os_ref, u_ref = rest
    else:
        o_ref, u_ref = rest
    j = pl.program_id(1)

    @pl.when(j == 0)
    def _():
        x = x_ref[...]
        ms = jnp.mean(x * x, axis=-1, keepdims=True)
        u = (x * lax.rsqrt(ms + EPS) * g_ref[...]).astype(BF16)
        u_ref[...] = u
        if with_small:
            os_ref[...] = jnp.dot(u, ws_ref[...], preferred_element_type=F32)

    o_ref[...] = jnp.dot(u_ref[...], w_ref[...],
                         preferred_element_type=F32).astype(o_ref.dtype)


def _in_proj(x2d, g_row, w, w_small, out_dtype, tm, tn, name):
    m, d = x2d.shape
    n = w.shape[1]
    with_small = w_small is not None
    in_specs = [pl.BlockSpec((tm, d), lambda i, j: (i, 0)),
                pl.BlockSpec((1, d), lambda i, j: (0, 0)),
                pl.BlockSpec((d, tn), lambda i, j: (0, j))]
    out_shape = [jax.ShapeDtypeStruct((m, n), out_dtype)]
    out_specs = [pl.BlockSpec((tm, tn), lambda i, j: (i, j))]
    args = [x2d, g_row, w]
    if with_small:
        in_specs.append(pl.BlockSpec((d, LANES), lambda i, j: (0, 0)))
        out_shape.append(jax.ShapeDtypeStruct((m, LANES), F32))
        out_specs.append(pl.BlockSpec((tm, LANES), lambda i, j: (i, 0)))
        args.append(w_small)
    return pl.pallas_call(
        functools.partial(_in_proj_kernel, with_small=with_small),
        grid=(m // tm, n // tn),
        in_specs=in_specs,
        out_specs=out_specs,
        out_shape=out_shape,
        scratch_shapes=[pltpu.VMEM((tm, d), BF16)],
        compiler_params=pltpu.CompilerParams(
            dimension_semantics=("parallel", "arbitrary"),
            vmem_limit_bytes=VMEM_LIMIT),
        name=name,
    )(*args)


def _ssd_kernel(x0_ref, x1_ref, bc_ref, s_ref, convw_ref, convb_ref, brow_ref, alog_ref,
                dskip_ref, e_ref, shift_ref, pb_ref, st0_ref, tail0_ref, c0_ref,
                y_ref, bias_ref, st_ref, tail_ref, cc_ref,
                ext_ref, act_ref, *, first_valid):
    ci = pl.program_id(1)

    @pl.when(ci == 0)
    def _():
        st_ref[0] = st0_ref[...]
        tail_ref[0] = tail0_ref[...]
        cc_ref[0] = c0_ref[...]

    ext_ref[0:TAIL_ROWS, :] = tail_ref[0]
    ext_ref[TAIL_ROWS:TAIL_ROWS + CHUNK, 0:1024] = x0_ref[0]
    ext_ref[TAIL_ROWS:TAIL_ROWS + CHUNK, 1024:2048] = x1_ref[0]
    ext_ref[TAIL_ROWS:TAIL_ROWS + CHUNK, 2048:3072] = bc_ref[0]
    tail_ref[0] = ext_ref[CHUNK:CHUNK + TAIL_ROWS, :]

    row = lax.broadcasted_iota(jnp.int32, (CHUNK, 1), 0)
    valid_row = row >= first_valid
    shift = shift_ref[...]
    for j in range(CONV_DIM // GROUP_WIDTH):
        sl = slice(j * GROUP_WIDTH, (j + 1) * GROUP_WIDTH)
        taps = jnp.dot(shift, ext_ref[:, sl], preferred_element_type=F32)
        acc = (convb_ref[:, sl] + convw_ref[CONV_K - 1:CONV_K, sl]
               * ext_ref[TAIL_ROWS:TAIL_ROWS + CHUNK, sl].astype(F32))
        for k in range(CONV_K - 1):
            acc = acc + convw_ref[k:k + 1, sl] * taps[k * CHUNK:(k + 1) * CHUNK]
        a = acc * jax.nn.sigmoid(acc)
        if first_valid:
            a = jnp.where(valid_row, a, 0.0)
        act_ref[:, sl] = a

    lane = lax.broadcasted_iota(jnp.int32, (1, LANES), 1)
    is_dt = lane < H_SSD
    is_f = (lane >= F_LANE0) & (lane < F_LANE0 + H_ATT)
    sv = s_ref[0] + brow_ref[...]
    z = jnp.where(is_dt, sv, -sv)
    sp = jnp.maximum(z, 0.0) + jnp.log(1.0 + jnp.exp(-jnp.abs(z)))
    a_row = -jnp.exp(alog_ref[...])
    vals = jnp.where(is_dt, sp * a_row, jnp.where(is_f, -sp, 0.0))
    dt = jnp.where(is_dt, sp, 0.0)
    if first_valid:
        vals = jnp.where(valid_row, vals, 0.0)
        dt = jnp.where(valid_row, dt, 0.0)

    ri = lax.broadcasted_iota(jnp.int32, (CHUNK, CHUNK), 0)
    cj = lax.broadcasted_iota(jnp.int32, (CHUNK, CHUNK), 1)
    tri = cj <= ri
    cum = _dot_exact_lhs(tri.astype(BF16), vals)
    cfull = cum + cc_ref[0]
    c_log2 = LOG2E * cfull
    if first_valid:
        c_log2 = jnp.where(valid_row, c_log2, -MASK_BIAS)
    terms = jnp.concatenate(_split3(c_log2), axis=1)
    placed = jnp.dot(terms, pb_ref[...], preferred_element_type=F32)
    bias_ref[0] = jnp.where(lane == BIAS_ONE_LANE, 1.0, placed).astype(BF16)
    cum_t = cfull.T
    cc_ref[0] = jnp.where(is_f, cfull[CHUNK - 1:CHUNK, :], 0.0)
    dt_t = dt.T

    a_last = cum[CHUNK - 1:CHUNK, :]
    w_t = (dt * jnp.exp(a_last - cum)).T
    cd = jnp.where(is_dt, jnp.exp(a_last), 0.0)
    cd_e = _dot_exact_rhs(jnp.broadcast_to(cd, (8, LANES)), e_ref[...])[0:1]

    lo_half = lane < SSD_HEAD_DIM
    neg_inf = jnp.float32(-jnp.inf)

    def split_heads(x16):
        zero16 = jnp.zeros_like(x16)
        return jnp.concatenate([jnp.where(lo_half, x16, zero16),
                                jnp.where(lo_half, zero16, x16)], axis=0)

    for g in range(SSD_GROUPS):
        bm = act_ref[:, D_SSD + g * D_STATE:D_SSD + (g + 1) * D_STATE]
        cm = act_ref[:, D_SSD + SSD_GROUPS * D_STATE + g * D_STATE:
                     D_SSD + SSD_GROUPS * D_STATE + (g + 1) * D_STATE]
        cb = lax.dot_general(cm.astype(BF16), bm.astype(BF16), _NT,
                             preferred_element_type=F32)
        bm_t = bm.T
        for pr in range(HEADS_PER_GROUP // 2):
            psl = slice(g * GROUP_WIDTH + pr * LANES, g * GROUP_WIDTH + (pr + 1) * LANES)
            ssl = slice(pr * LANES, (pr + 1) * LANES)
            xs_p = act_ref[:, psl]
            hin = st_ref[0, g, :, ssl]
            intra, inter, bw = [], [], []
            for h in (g * HEADS_PER_GROUP + 2 * pr, g * HEADS_PER_GROUP + 2 * pr + 1):
                a_col = cum[:, h:h + 1]
                seg = a_col - cum_t[h:h + 1, :]
                dec = jnp.exp(jnp.where(tri, seg, neg_inf))
                intra.append((cb * dec * dt_t[h:h + 1, :]).astype(BF16))
                inter.append((cm * jnp.exp(a_col)).astype(BF16))
                bw.append((bm_t * w_t[h:h + 1, :]).astype(BF16))
            xs2 = split_heads(xs_p.astype(BF16))
            y_lhs = jnp.concatenate(intra + inter, axis=1)
            y_rhs = jnp.concatenate([xs2, split_heads(hin.astype(BF16))], axis=0)
            y_pair = jnp.dot(y_lhs, y_rhs, preferred_element_type=F32)
            y_ref[0, :, psl] = (y_pair + dskip_ref[:, psl] * xs_p).astype(y_ref.dtype)
            st_ref[0, g, :, ssl] = hin * cd_e[:, psl] + jnp.dot(
                jnp.concatenate(bw, axis=1), xs2, preferred_element_type=F32)


def _conv_shift_matrix():
    m = np.zeros(((CONV_K - 1) * CHUNK, TAIL_ROWS + CHUNK), np.float32)
    for k in range(CONV_K - 1):
        for t in range(CHUNK):
            m[k * CHUNK + t, TAIL_ROWS + t - (CONV_K - 1) + k] = 1.0
    return jnp.asarray(m, BF16)


def _ssd(p1, small, convw, convb, brow, alog_row, dskip_row, expand_mat, st0, tail0, c0,
         first_valid, name):
    b, s, _ = p1.shape
    nc = s // CHUNK
    const = lambda shape: pl.BlockSpec(shape, lambda bi, ci: (0,) * len(shape))
    in_specs = [
        pl.BlockSpec((1, CHUNK, 1024), lambda bi, ci: (bi, ci, 2)),
        pl.BlockSpec((1, CHUNK, 1024), lambda bi, ci: (bi, ci, 3)),
        pl.BlockSpec((1, CHUNK, 1024), lambda bi, ci: (bi, ci, 4)),
        pl.BlockSpec((1, CHUNK, LANES), lambda bi, ci: (bi, ci, 0)),
        const((CONV_K, CONV_DIM)), const((1, CONV_DIM)), const((1, LANES)), const((1, LANES)),
        const((1, D_SSD)), const((LANES, D_SSD)), const(((CONV_K - 1) * CHUNK, TAIL_ROWS + CHUNK)),
        const((3 * LANES, LANES)),
        const((SSD_GROUPS, D_STATE, GROUP_WIDTH)), const((TAIL_ROWS, CONV_DIM)), const((1, LANES)),
    ]
    out_shape = [
        jax.ShapeDtypeStruct((b, s, D_SSD), BF16),
        jax.ShapeDtypeStruct((b, s, LANES), BF16),
        jax.ShapeDtypeStruct((b, SSD_GROUPS, D_STATE, GROUP_WIDTH), F32),
        jax.ShapeDtypeStruct((b, TAIL_ROWS, CONV_DIM), BF16),
        jax.ShapeDtypeStruct((b, 1, LANES), F32),
    ]
    out_specs = [
        pl.BlockSpec((1, CHUNK, D_SSD), lambda bi, ci: (bi, ci, 0)),
        pl.BlockSpec((1, CHUNK, LANES), lambda bi, ci: (bi, ci, 0)),
        pl.BlockSpec((1, SSD_GROUPS, D_STATE, GROUP_WIDTH), lambda bi, ci: (bi, 0, 0, 0)),
        pl.BlockSpec((1, TAIL_ROWS, CONV_DIM), lambda bi, ci: (bi, 0, 0)),
        pl.BlockSpec((1, 1, LANES), lambda bi, ci: (bi, 0, 0)),
    ]
    return pl.pallas_call(
        functools.partial(_ssd_kernel, first_valid=first_valid),
        grid=(b, nc),
        in_specs=in_specs,
        out_specs=out_specs,
        out_shape=out_shape,
        scratch_shapes=[pltpu.VMEM((TAIL_ROWS + CHUNK, CONV_DIM), BF16),
                        pltpu.VMEM((CHUNK, CONV_DIM), F32)],
        compiler_params=pltpu.CompilerParams(
            dimension_semantics=("parallel", "arbitrary"),
            vmem_limit_bytes=VMEM_LIMIT),
        name=name,
    )(p1, p1, p1, small, convw, convb, brow, alog_row, dskip_row, expand_mat,
      _conv_shift_matrix(), _bias_placement_matrix(), st0, tail0, c0)


def _head_column(blk, head_lane, lane):
    return jnp.sum(jnp.where(lane == head_lane, blk, 0.0), axis=1, keepdims=True)


def _placement_matrices():
    npairs = H_ATT // 2
    pq = np.zeros((npairs, 2, 2 * LANES, LANES), np.float32)
    pk = np.zeros((npairs, 2, 2 * LANES, LANES), np.float32)
    for hp in range(npairs):
        for hh in range(2):
            own0 = hh * ATT_HEAD_DIM
            base = (1 - hh) * ATT_HEAD_DIM
            for l in range(own0, own0 + ATT_HEAD_DIM):
                pq[hp, hh, l, l] = 1.0
                pk[hp, hh, l, l] = 1.0
            for t in range(3):
                pq[hp, hh, LANES + 3 * (2 * hp + hh) + t, base + t] = 1.0
                pq[hp, hh, LANES + BIAS_ONE_LANE, base + 3 + t] = 1.0
                pk[hp, hh, LANES + BIAS_ONE_LANE, base + t] = 1.0
                pk[hp, hh, LANES + 3 * (2 * hp + hh) + t, base + 3 + t] = -1.0
    return jnp.asarray(pq, BF16), jnp.asarray(pk.transpose(0, 1, 3, 2), BF16)


def _bias_placement_matrix():
    m = np.zeros((3 * LANES, LANES), np.float32)
    for h in range(H_ATT):
        for t in range(3):
            m[t * LANES + F_LANE0 + h, 3 * h + t] = 1.0
    return jnp.asarray(m, BF16)


def _pair_schedule(nq, unroll):
    assert nq % (2 * unroll) == 0
    slots = [[] for _ in range(unroll)]
    for i in range(nq // 2):
        slots[i % unroll] += [i, nq - 1 - i]
    full = [[(qb, li, kb) for li, qb in enumerate(qbs) for kb in range(qb + 1)] for qbs in slots]
    diag = [[(qb, li, qb + 1) for li, qb in enumerate(qbs)] for qbs in slots]
    g1, g2 = len(full[0]), len(diag[0])
    assert all(len(f) == g1 for f in full) and all(len(d) == g2 for d in diag)
    parts = []
    for sched in (full, diag):
        parts.append([qb * ATT_BLOCK for s in sched for (qb, _, _) in s])
        parts.append([li for s in sched for (_, li, _) in s])
        parts.append([kb for s in sched for (_, _, kb) in s])
    table = np.concatenate([np.asarray(p, np.int32) for p in parts])
    return slots, g1, g2, table


def _attn_kernel(tab_ref, q_ref, k_ref, v_ref, bias_ref, km_ref, vm_ref, bm_ref, pq_ref, pk_ref,
                 o_ref, kaug_scr, vsel_scr, qaug_scr, s_scr, p_scr, alpha_scr, m_scr, acc_scr,
                 *, slots, g1, g2):
    seq = q_ref.shape[1]
    blk = ATT_BLOCK
    unroll = len(slots)
    lane = lax.broadcasted_iota(jnp.int32, (1, LANES), 1)
    lo_half = lane < ATT_HEAD_DIM
    own = (lo_half, jnp.logical_not(lo_half))
    l_lane = (ATT_HEAD_DIM, 0)
    neg_inf = jnp.float32(-jnp.inf)

    def values_operand(vb, hh):
        keep = jnp.where(own[hh], vb, jnp.zeros_like(vb))
        return jnp.where(lane == l_lane[hh], jnp.ones_like(vb), keep)

    def keys_operand(kb, bias, hh):
        x = jnp.concatenate([kb, bias], axis=1)
        return lax.dot_general(pk_ref[0, hh], x, _NT, preferred_element_type=F32).astype(BF16)

    meta_bias = bm_ref[...]
    pad_bias = jnp.broadcast_to(meta_bias[0:1, :], (CHUNK, LANES))
    for hh in range(2):
        kaug_scr[hh, 0, :, 0:CHUNK] = keys_operand(km_ref[...], meta_bias, hh)
        kaug_scr[hh, 0, :, CHUNK:blk] = keys_operand(jnp.zeros((CHUNK, LANES), BF16), pad_bias, hh)
        vsel_scr[hh, 0:CHUNK, :] = values_operand(vm_ref[...], hh)
        vsel_scr[hh, CHUNK:blk, :] = jnp.zeros((blk - CHUNK, LANES), BF16)

    def prep_chunk(r, carry):
        rows = pl.ds(pl.multiple_of(r * blk, blk), blk)
        krows = pl.ds(pl.multiple_of((r + 1) * blk, blk), blk)
        bias = bias_ref[0, rows, :]
        xq = jnp.concatenate([q_ref[0, rows, :], bias], axis=1)
        kb = k_ref[0, rows, :]
        vb = v_ref[0, rows, :]
        for hh in range(2):
            qaug_scr[hh, rows, :] = jnp.dot(xq, pq_ref[0, hh],
                                            preferred_element_type=F32).astype(BF16)
            kaug_scr[hh, r + 1] = keys_operand(kb, bias, hh)
            vsel_scr[hh, krows, :] = values_operand(vb, hh)
        return carry

    lax.fori_loop(0, seq // blk, prep_chunk, 0, unroll=4)

    m_scr[...] = jnp.full(m_scr.shape, neg_inf, F32)
    acc_scr[...] = jnp.zeros(acc_scr.shape, F32)

    rr = lax.broadcasted_iota(jnp.int32, (blk, blk), 0)
    cc = lax.broadcasted_iota(jnp.int32, (blk, blk), 1)
    causal = cc <= rr

    def run(base, ngroups, masked, s_scr, p_scr, alpha_scr):
        def entry(field, u, g):
            return tab_ref[base + (field * unroll + u) * ngroups + g]

        def block_rows(off):
            return pl.ds(pl.multiple_of(off, blk), blk)

        def stage_a(g):
            for u in range(unroll):
                qrows = block_rows(entry(0, u, g))
                kb = entry(2, u, g)
                for hh in range(2):
                    s_scr[u, hh] = jnp.dot(qaug_scr[hh, qrows, :], kaug_scr[hh, kb],
                                           preferred_element_type=F32)

        def stage_b(g):
            for u in range(unroll):
                li = entry(1, u, g)
                for hh in range(2):
                    s = s_scr[u, hh]
                    if masked:
                        s = jnp.where(causal, s, neg_inf)
                    m_prev = m_scr[u, hh, li]
                    m_next = jnp.maximum(m_prev, jnp.max(s, axis=1, keepdims=True))
                    alpha_scr[u, hh] = jnp.exp2(m_prev - m_next)
                    m_scr[u, hh, li] = m_next
                    p_scr[u, hh] = jnp.exp2(
                        s - jnp.concatenate([m_next] * (blk // LANES), axis=1)).astype(BF16)

        def stage_c(g):
            for u in range(unroll):
                li = entry(1, u, g)
                krows = block_rows(entry(2, u, g) * blk)
                for hh in range(2):
                    pv = jnp.dot(p_scr[u, hh], vsel_scr[hh, krows, :], preferred_element_type=F32)
                    acc_scr[u, hh, li] = acc_scr[u, hh, li] * alpha_scr[u, hh] + pv

        stage_a(0)
        stage_b(0)
        stage_a(min(1, ngroups - 1))

        def body(g, carry):
            stage_c(g - 1)
            stage_b(g)
            stage_a(jnp.minimum(g + 1, ngroups - 1))
            return carry

        lax.fori_loop(1, ngroups, body, 0)
        stage_c(ngroups - 1)

    run(0, g1, False, s_scr.at[0], p_scr.at[0], alpha_scr.at[0])
    run(3 * unroll * g1, g2, True, s_scr.at[1], p_scr.at[1], alpha_scr.at[1])

    for u, qbs in enumerate(slots):
        for li, qb in enumerate(qbs):
            outs = []
            for hh in range(2):
                a = acc_scr[u, hh, li]
                outs.append(a / _head_column(a, l_lane[hh], lane))
            o_ref[0, qb * blk:(qb + 1) * blk, :] = jnp.where(lo_half, outs[0],
                                                             outs[1]).astype(o_ref.dtype)


def _attention(qkv, bias, qkv_meta, bias_meta, name):
    b, s, _ = qkv.shape
    npairs = H_ATT // 2
    blk = ATT_BLOCK
    slots, g1, g2, table = _pair_schedule(s // blk, ATT_UNROLL)
    nloc = len(slots[0])
    seq_blk = lambda col0: pl.BlockSpec((1, s, LANES), lambda bi, hp, tab: (bi, 0, col0 + hp))
    meta_blk = lambda col0: pl.BlockSpec((CHUNK, LANES), lambda bi, hp, tab: (0, col0 + hp))
    grid_spec = pltpu.PrefetchScalarGridSpec(
        num_scalar_prefetch=1,
        grid=(b, npairs),
        in_specs=[
            seq_blk(Q_BLOCK0), seq_blk(Q_BLOCK0 + npairs), seq_blk(Q_BLOCK0 + 2 * npairs),
            pl.BlockSpec((1, s, LANES), lambda bi, hp, tab: (bi, 0, 0)),
            meta_blk(Q_BLOCK0 + npairs), meta_blk(Q_BLOCK0 + 2 * npairs),
            pl.BlockSpec((CHUNK, LANES), lambda bi, hp, tab: (0, 0)),
            pl.BlockSpec((1, 2, 2 * LANES, LANES), lambda bi, hp, tab: (hp, 0, 0, 0)),
            pl.BlockSpec((1, 2, LANES, 2 * LANES), lambda bi, hp, tab: (hp, 0, 0, 0)),
        ],
        out_specs=pl.BlockSpec((1, s, LANES), lambda bi, hp, tab: (bi, 0, hp)),
        scratch_shapes=[
            pltpu.VMEM((2, s // blk + 1, LANES, blk), BF16),
            pltpu.VMEM((2, s + blk, LANES), BF16),
            pltpu.VMEM((2, s, LANES), BF16),
            pltpu.VMEM((2, ATT_UNROLL, 2, blk, blk), F32),
            pltpu.VMEM((2, ATT_UNROLL, 2, blk, blk), BF16),
            pltpu.VMEM((2, ATT_UNROLL, 2, blk, LANES), F32),
            pltpu.VMEM((ATT_UNROLL, 2, nloc, blk, LANES), F32),
            pltpu.VMEM((ATT_UNROLL, 2, nloc, blk, LANES), F32),
        ])
    return pl.pallas_call(
        functools.partial(_attn_kernel, slots=slots, g1=g1, g2=g2),
        grid_spec=grid_spec,
        out_shape=jax.ShapeDtypeStruct((b, s, D_ATT), BF16),
        compiler_params=pltpu.CompilerParams(
            dimension_semantics=("parallel", "arbitrary"),
            vmem_limit_bytes=VMEM_LIMIT),
        name=name,
    )(jnp.asarray(table), qkv, qkv, qkv, bias, qkv_meta, qkv_meta, bias_meta,
      *_placement_matrices())


def _merge_kernel(y_ref, zs_ref, o_ref, za_ref, gs_ref, ga_ref, x_ref, wps_ref, wpa_ref,
                  wout_ref, snorm_ref, gbias_ref, npost_ref, out_ref):
    zs = zs_ref[...].astype(F32)
    u = y_ref[...].astype(F32) * (zs * jax.nn.sigmoid(zs))
    parts = []
    for g in range(SSD_GROUPS):
        ug = u[:, g * GROUP_WIDTH:(g + 1) * GROUP_WIDTH]
        ms = jnp.mean(ug * ug, axis=-1, keepdims=True)
        parts.append((ug * lax.rsqrt(ms + EPS)
                      * snorm_ref[:, g * GROUP_WIDTH:(g + 1) * GROUP_WIDTH]).astype(BF16))
    y_ssd = jnp.concatenate(parts, axis=1)
    t_ssd = jnp.dot(y_ssd, wps_ref[...], preferred_element_type=F32)
    za = za_ref[...].astype(F32)
    y_att = (o_ref[...].astype(F32) * (za * jax.nn.sigmoid(za))).astype(BF16)
    t_att = jnp.dot(y_att, wpa_ref[...], preferred_element_type=F32)
    g_ssd = jax.nn.sigmoid(gs_ref[...].astype(F32) + gbias_ref[:, 0:D_MODEL])
    g_att = jax.nn.sigmoid(ga_ref[...].astype(F32) + gbias_ref[:, D_MODEL:2 * D_MODEL])
    merged = (g_ssd * t_ssd + g_att * t_att).astype(BF16)
    t = jnp.dot(merged, wout_ref[...], preferred_element_type=F32)
    ms = jnp.mean(t * t, axis=-1, keepdims=True)
    out_ref[...] = x_ref[...] + t * lax.rsqrt(ms + EPS) * npost_ref[...]


def _merge(y2d, p1, o2d, x2d, wps, wpa, wout, snorm, gbias, npost, tm, name):
    m = x2d.shape[0]
    row_blk = lambda width, col: pl.BlockSpec((tm, width), lambda i: (i, col))
    const = lambda shape: pl.BlockSpec(shape, lambda i: (0, 0), pipeline_mode=pl.Buffered(1))
    in_specs = [
        row_blk(D_SSD, 0),
        row_blk(D_SSD, 0),
        row_blk(D_ATT, 0),
        row_blk(D_ATT, 5),
        row_blk(D_MODEL, 6),
        row_blk(D_MODEL, 7),
        row_blk(D_MODEL, 0),
        const((D_SSD, D_MODEL)), const((D_ATT, D_MODEL)), const((D_MODEL, D_MODEL)),
        const((1, D_SSD)), const((1, 2 * D_MODEL)), const((1, D_MODEL)),
    ]
    return pl.pallas_call(
        _merge_kernel,
        grid=(m // tm,),
        in_specs=in_specs,
        out_specs=pl.BlockSpec((tm, D_MODEL), lambda i: (i, 0)),
        out_shape=jax.ShapeDtypeStruct((m, D_MODEL), F32),
        compiler_params=pltpu.CompilerParams(
            dimension_semantics=("parallel",),
            vmem_limit_bytes=VMEM_LIMIT),
        name=name,
    )(y2d, p1, o2d, p1, p1, p1, x2d, wps, wpa, wout, snorm, gbias, npost)


def kernel(x, meta_tokens, norm_pre, w_in, conv_w, conv_b, dt_bias, a_log, d_skip, ssd_norm,
           fgate_bias, gate_bias, w_proj_ssd, w_proj_att, w_out, norm_post):
    bsz, seq, d = x.shape
    assert d == D_MODEL and seq % 1024 == 0 and norm_pre.shape[0] == 1

    w = w_in[0]
    o_xbc = D_SSD
    o_dt = o_xbc + CONV_DIM
    o_za = o_dt + H_SSD
    o_q = o_za + D_ATT
    o_k = o_q + D_ATT
    o_v = o_k + D_ATT
    o_f = o_v + D_ATT
    o_g = o_f + H_ATT
    att_scale = LOG2E / (ATT_HEAD_DIM ** 0.5)
    w_all = jnp.concatenate(
        [w[:, 0:o_dt].astype(BF16), w[:, o_za:o_q].astype(BF16), w[:, o_g:].astype(BF16),
         (w[:, o_q:o_k] * att_scale).astype(BF16), w[:, o_k:o_f].astype(BF16)], axis=1)
    w_small = jnp.concatenate(
        [w[:, o_dt:o_za], w[:, o_f:o_g],
         jnp.zeros((d, LANES - H_SSD - H_ATT), F32)], axis=1).astype(BF16)
    g_pre = norm_pre[0][None, :]
    pad_lanes = jnp.zeros((LANES - H_SSD - H_ATT,), F32)
    brow = jnp.concatenate([dt_bias[0], fgate_bias[0], pad_lanes])[None, :]
    alog_row = jnp.concatenate([a_log[0], jnp.zeros((LANES - H_SSD,), F32)])[None, :]
    dskip_row = jnp.repeat(d_skip[0], SSD_HEAD_DIM)[None, :]
    head_of_lane = jnp.arange(D_SSD, dtype=jnp.int32) // SSD_HEAD_DIM
    expand_mat = (jnp.arange(LANES, dtype=jnp.int32)[:, None] == head_of_lane[None, :]).astype(BF16)
    convw = conv_w[0]
    convb = conv_b[0][None, :]
    wps = w_proj_ssd[0].astype(BF16)
    wpa = w_proj_att[0].astype(BF16)
    wout = w_out[0].astype(BF16)
    snorm = ssd_norm[0][None, :]
    gbias = gate_bias[0][None, :]
    npost = norm_post[0][None, :]

    meta_blk = jnp.concatenate([jnp.zeros((CHUNK - N_META, d), F32), meta_tokens.astype(F32)], axis=0)
    p_m, small_m = _in_proj(meta_blk, g_pre, w_all, w_small, BF16, CHUNK, P_TILE_N, "in_proj_meta")
    st_zero = jnp.zeros((SSD_GROUPS, D_STATE, GROUP_WIDTH), F32)
    tail_zero = jnp.zeros((TAIL_ROWS, CONV_DIM), BF16)
    c_zero = jnp.zeros((1, LANES), F32)
    _, bias_m, st_m, tail_m, cc_m = _ssd(
        p_m[None], small_m[None], convw, convb, brow, alog_row, dskip_row, expand_mat,
        st_zero, tail_zero, c_zero, CHUNK - N_META, "ssd_meta")

    x2d = x.reshape(bsz * seq, d)
    p, small = _in_proj(x2d, g_pre, w_all, w_small, BF16, 1024, P_TILE_N, "in_proj")
    p3d = p.reshape(bsz, seq, P_COLS)
    y, bias, _, _, _ = _ssd(
        p3d, small.reshape(bsz, seq, LANES), convw, convb, brow,
        alog_row, dskip_row, expand_mat, st_m[0], tail_m[0], cc_m[0], 0, "ssd")
    o = _attention(p3d, bias, p_m, bias_m[0], "fox_attention")
    out = _merge(y.reshape(bsz * seq, D_SSD), p, o.reshape(bsz * seq, D_ATT), x2d,
                 wps, wpa, wout, snorm, gbias, npost, 512, "merge_out")
    return out.reshape(bsz, seq, d)
```

```python
import functools

import numpy as np

import jax
import jax.numpy as jnp
from jax import lax
from jax.experimental import pallas as pl
from jax.experimental.pallas import tpu as pltpu

F32 = jnp.float32
BF16 = jnp.bfloat16

D_MODEL = 1024
N_META = 16
CHUNK = 128
LANES = 128
D_SSD = 2 * D_MODEL
SSD_HEAD_DIM = 64
H_SSD = D_SSD // SSD_HEAD_DIM
SSD_GROUPS = 4
HEADS_PER_GROUP = H_SSD // SSD_GROUPS
D_STATE = 128
GROUP_WIDTH = HEADS_PER_GROUP * SSD_HEAD_DIM
CONV_K = 4
CONV_DIM = D_SSD + 2 * SSD_GROUPS * D_STATE
H_ATT = 16
ATT_HEAD_DIM = 64
D_ATT = H_ATT * ATT_HEAD_DIM
EPS = 1e-6
LOG2E = 1.4426950408889634
TAIL_ROWS = 16
F_LANE0 = H_SSD
P_COLS = 11 * 1024
P_TILE_N = P_COLS // 4
Q_BLOCK0 = 8 * (1024 // LANES)
VMEM_LIMIT = 56 * 1024 * 1024
ATT_BLOCK = 256
ATT_UNROLL = 4
MASK_BIAS = -1.0e30
BIAS_ONE_LANE = 3 * H_ATT

_NT = (((1,), (1,)), ((), ()))


def _split3(v):
    hi = v.astype(BF16)
    r1 = v - hi.astype(F32)
    mid = r1.astype(BF16)
    lo = (r1 - mid.astype(F32)).astype(BF16)
    return hi, mid, lo


def _dot_exact_lhs(a_bf16, v):
    hi, mid, lo = _split3(v)
    d = functools.partial(jnp.dot, preferred_element_type=F32)
    return d(a_bf16, hi) + d(a_bf16, mid) + d(a_bf16, lo)


def _dot_exact_rhs(v, b_bf16):
    hi, mid, lo = _split3(v)
    d = functools.partial(jnp.dot, preferred_element_type=F32)
    return d(hi, b_bf16) + d(mid, b_bf16) + d(lo, b_bf16)


def _in_proj_kernel(x_ref, g_ref, w_ref, *rest, with_small):
    if with_small:
        ws_ref, o_ref, os_ref, u_ref = rest
    else:
        o_ref, u_ref = rest
    j = pl.program_id(1)

    @pl.when(j == 0)
    def _():
        x = x_ref[...]
        ms = jnp.mean(x * x, axis=-1, keepdims=True)
        u = (x * lax.rsqrt(ms + EPS) * g_ref[...]).astype(BF16)
        u_ref[...] = u
        if with_small:
            os_ref[...] = lax.dot_general(u, ws_ref[...], _NT, preferred_element_type=F32)

    o_ref[...] = lax.dot_general(u_ref[...], w_ref[...], _NT,
                                 preferred_element_type=F32).astype(o_ref.dtype)


def _in_proj(x2d, g_row, w_t, w_small_t, out_dtype, tm, tn, name):
    m, d = x2d.shape
    n = w_t.shape[0]
    with_small = w_small_t is not None
    in_specs = [pl.BlockSpec((tm, d), lambda i, j: (i, 0)),
                pl.BlockSpec((1, d), lambda i, j: (0, 0)),
                pl.BlockSpec((tn, d), lambda i, j: (j, 0))]
    out_shape = [jax.ShapeDtypeStruct((m, n), out_dtype)]
    out_specs = [pl.BlockSpec((tm, tn), lambda i, j: (i, j))]
    args = [x2d, g_row, w_t]
    w_small = w_small_t
    if with_small:
        in_specs.append(pl.BlockSpec((LANES, d), lambda i, j: (0, 0)))
        out_shape.append(jax.ShapeDtypeStruct((m, LANES), F32))
        out_specs.append(pl.BlockSpec((tm, LANES), lambda i, j: (i, 0)))
        args.append(w_small)
    return pl.pallas_call(
        functools.partial(_in_proj_kernel, with_small=with_small),
        grid=(m // tm, n // tn),
        in_specs=in_specs,
        out_specs=out_specs,
        out_shape=out_shape,
        scratch_shapes=[pltpu.VMEM((tm, d), BF16)],
        compiler_params=pltpu.CompilerParams(
            dimension_semantics=("parallel", "arbitrary"),
            vmem_limit_bytes=VMEM_LIMIT),
        name=name,
    )(*args)


def _ssd_kernel(x0_ref, x1_ref, bc_ref, s_ref, convw_ref, convb_ref, brow_ref, alog_ref,
                dskip_ref, e_ref, shift_ref, pb_ref, st0_ref, tail0_ref, c0_ref,
                y_ref, bias_ref, st_ref, tail_ref, cc_ref,
                ext_ref, act_ref, *, first_valid):
    ci = pl.program_id(1)

    @pl.when(ci == 0)
    def _():
        st_ref[0] = st0_ref[...]
        tail_ref[0] = tail0_ref[...]
        cc_ref[0] = c0_ref[...]

    ext_ref[0:TAIL_ROWS, :] = tail_ref[0]
    ext_ref[TAIL_ROWS:TAIL_ROWS + CHUNK, 0:1024] = x0_ref[0]
    ext_ref[TAIL_ROWS:TAIL_ROWS + CHUNK, 1024:2048] = x1_ref[0]
    ext_ref[TAIL_ROWS:TAIL_ROWS + CHUNK, 2048:3072] = bc_ref[0]
    tail_ref[0] = ext_ref[CHUNK:CHUNK + TAIL_ROWS, :]

    row = lax.broadcasted_iota(jnp.int32, (CHUNK, 1), 0)
    valid_row = row >= first_valid
    shift = shift_ref[...]
    for j in range(CONV_DIM // GROUP_WIDTH):
        sl = slice(j * GROUP_WIDTH, (j + 1) * GROUP_WIDTH)
        taps = jnp.dot(shift, ext_ref[:, sl], preferred_element_type=F32)
        acc = (convb_ref[:, sl] + convw_ref[CONV_K - 1:CONV_K, sl]
               * ext_ref[TAIL_ROWS:TAIL_ROWS + CHUNK, sl].astype(F32))
        for k in range(CONV_K - 1):
            acc = acc + convw_ref[k:k + 1, sl] * taps[k * CHUNK:(k + 1) * CHUNK]
        a = acc * jax.nn.sigmoid(acc)
        if first_valid:
            a = jnp.where(valid_row, a, 0.0)
        act_ref[:, sl] = a

    lane = lax.broadcasted_iota(jnp.int32, (1, LANES), 1)
    is_dt = lane < H_SSD
    is_f = (lane >= F_LANE0) & (lane < F_LANE0 + H_ATT)
    sv = s_ref[0] + brow_ref[...]
    z = jnp.where(is_dt, sv, -sv)
    sp = jnp.maximum(z, 0.0) + jnp.log(1.0 + jnp.exp(-jnp.abs(z)))
    a_row = -jnp.exp(alog_ref[...])
    vals = jnp.where(is_dt, sp * a_row, jnp.where(is_f, -sp, 0.0))
    dt = jnp.where(is_dt, sp, 0.0)
    if first_valid:
        vals = jnp.where(valid_row, vals, 0.0)
        dt = jnp.where(valid_row, dt, 0.0)

    ri = lax.broadcasted_iota(jnp.int32, (CHUNK, CHUNK), 0)
    cj = lax.broadcasted_iota(jnp.int32, (CHUNK, CHUNK), 1)
    tri = cj <= ri
    cum = _dot_exact_lhs(tri.astype(BF16), vals)
    cfull = cum + cc_ref[0]
    c_log2 = LOG2E * cfull
    if first_valid:
        c_log2 = jnp.where(valid_row, c_log2, -MASK_BIAS)
    terms = jnp.concatenate(_split3(c_log2), axis=1)
    placed = jnp.dot(terms, pb_ref[...], preferred_element_type=F32)
    bias_ref[0] = jnp.where(lane == BIAS_ONE_LANE, 1.0, placed).astype(BF16)
    cum_t = cfull.T
    cc_ref[0] = jnp.where(is_f, cfull[CHUNK - 1:CHUNK, :], 0.0)
    dt_t = dt.T

    a_last = cum[CHUNK - 1:CHUNK, :]
    w_t = (dt * jnp.exp(a_last - cum)).T
    cd = jnp.where(is_dt, jnp.exp(a_last), 0.0)
    cd_e = _dot_exact_rhs(jnp.broadcast_to(cd, (8, LANES)), e_ref[...])[0:1]

    lo_half = lane < SSD_HEAD_DIM
    neg_inf = jnp.float32(-jnp.inf)

    def split_heads(x16):
        zero16 = jnp.zeros_like(x16)
        return jnp.concatenate([jnp.where(lo_half, x16, zero16),
                                jnp.where(lo_half, zero16, x16)], axis=0)

    for g in range(SSD_GROUPS):
        bm = act_ref[:, D_SSD + g * D_STATE:D_SSD + (g + 1) * D_STATE]
        cm = act_ref[:, D_SSD + SSD_GROUPS * D_STATE + g * D_STATE:
                     D_SSD + SSD_GROUPS * D_STATE + (g + 1) * D_STATE]
        cb = lax.dot_general(cm.astype(BF16), bm.astype(BF16), _NT,
                             preferred_element_type=F32)
        bm_t = bm.T
        for pr in range(HEADS_PER_GROUP // 2):
            psl = slice(g * GROUP_WIDTH + pr * LANES, g * GROUP_WIDTH + (pr + 1) * LANES)
            ssl = slice(pr * LANES, (pr + 1) * LANES)
            xs_p = act_ref[:, psl]
            hin = st_ref[0, g, :, ssl]
            intra, inter, bw = [], [], []
            for h in (g * HEADS_PER_GROUP + 2 * pr, g * HEADS_PER_GROUP + 2 * pr + 1):
                a_col = cum[:, h:h + 1]
                seg = a_col - cum_t[h:h + 1, :]
                dec = jnp.exp(jnp.where(tri, seg, neg_inf))
                intra.append((cb * dec * dt_t[h:h + 1, :]).astype(BF16))
                inter.append((cm * jnp.exp(a_col)).astype(BF16))
                bw.append((bm_t * w_t[h:h + 1, :]).astype(BF16))
            xs2 = split_heads(xs_p.astype(BF16))
            y_lhs = jnp.concatenate(intra + inter, axis=1)
            y_rhs = jnp.concatenate([xs2, split_heads(hin.astype(BF16))], axis=0)
            y_pair = jnp.dot(y_lhs, y_rhs, preferred_element_type=F32)
            y_ref[0, :, psl] = (y_pair + dskip_ref[:, psl] * xs_p).astype(y_ref.dtype)
            st_ref[0, g, :, ssl] = hin * cd_e[:, psl] + jnp.dot(
                jnp.concatenate(bw, axis=1), xs2, preferred_element_type=F32)


def _conv_shift_matrix():
    m = np.zeros(((CONV_K - 1) * CHUNK, TAIL_ROWS + CHUNK), np.float32)
    for k in range(CONV_K - 1):
        for t in range(CHUNK):
            m[k * CHUNK + t, TAIL_ROWS + t - (CONV_K - 1) + k] = 1.0
    return jnp.asarray(m, BF16)


def _ssd(p1, small, convw, convb, brow, alog_row, dskip_row, expand_mat, st0, tail0, c0,
         first_valid, name):
    b, s, _ = p1.shape
    nc = s // CHUNK
    const = lambda shape: pl.BlockSpec(shape, lambda bi, ci: (0,) * len(shape))
    in_specs = [
        pl.BlockSpec((1, CHUNK, 1024), lambda bi, ci: (bi, ci, 2)),
        pl.BlockSpec((1, CHUNK, 1024), lambda bi, ci: (bi, ci, 3)),
        pl.BlockSpec((1, CHUNK, 1024), lambda bi, ci: (bi, ci, 4)),
        pl.BlockSpec((1, CHUNK, LANES), lambda bi, ci: (bi, ci, 0)),
        const((CONV_K, CONV_DIM)), const((1, CONV_DIM)), const((1, LANES)), const((1, LANES)),
        const((1, D_SSD)), const((LANES, D_SSD)), const(((CONV_K - 1) * CHUNK, TAIL_ROWS + CHUNK)),
        const((3 * LANES, LANES)),
        const((SSD_GROUPS, D_STATE, GROUP_WIDTH)), const((TAIL_ROWS, CONV_DIM)), const((1, LANES)),
    ]
    out_shape = [
        jax.ShapeDtypeStruct((b, s, D_SSD), BF16),
        jax.ShapeDtypeStruct((b, s, LANES), BF16),
        jax.ShapeDtypeStruct((b, SSD_GROUPS, D_STATE, GROUP_WIDTH), F32),
        jax.ShapeDtypeStruct((b, TAIL_ROWS, CONV_DIM), BF16),
        jax.ShapeDtypeStruct((b, 1, LANES), F32),
    ]
    out_specs = [
        pl.BlockSpec((1, CHUNK, D_SSD), lambda bi, ci: (bi, ci, 0)),
        pl.BlockSpec((1, CHUNK, LANES), lambda bi, ci: (bi, ci, 0)),
        pl.BlockSpec((1, SSD_GROUPS, D_STATE, GROUP_WIDTH), lambda bi, ci: (bi, 0, 0, 0)),
        pl.BlockSpec((1, TAIL_ROWS, CONV_DIM), lambda bi, ci: (bi, 0, 0)),
        pl.BlockSpec((1, 1, LANES), lambda bi, ci: (bi, 0, 0)),
    ]
    return pl.pallas_call(
        functools.partial(_ssd_kernel, first_valid=first_valid),
        grid=(b, nc),
        in_specs=in_specs,
        out_specs=out_specs,
        out_shape=out_shape,
        scratch_shapes=[pltpu.VMEM((TAIL_ROWS + CHUNK, CONV_DIM), BF16),
                        pltpu.VMEM((CHUNK, CONV_DIM), F32)],
        compiler_params=pltpu.CompilerParams(
            dimension_semantics=("parallel", "arbitrary"),
            vmem_limit_bytes=VMEM_LIMIT),
        name=name,
    )(p1, p1, p1, small, convw, convb, brow, alog_row, dskip_row, expand_mat,
      _conv_shift_matrix(), _bias_placement_matrix(), st0, tail0, c0)


def _head_column(blk, head_lane, lane):
    return jnp.sum(jnp.where(lane == head_lane, blk, 0.0), axis=1, keepdims=True)


def _placement_matrices():
    npairs = H_ATT // 2
    pq = np.zeros((npairs, 2, 2 * LANES, LANES), np.float32)
    pk = np.zeros((npairs, 2, 2 * LANES, LANES), np.float32)
    for hp in range(npairs):
        for hh in range(2):
            own0 = hh * ATT_HEAD_DIM
            base = (1 - hh) * ATT_HEAD_DIM
            for l in range(own0, own0 + ATT_HEAD_DIM):
                pq[hp, hh, l, l] = 1.0
                pk[hp, hh, l, l] = 1.0
            for t in range(3):
                pq[hp, hh, LANES + 3 * (2 * hp + hh) + t, base + t] = 1.0
                pq[hp, hh, LANES + BIAS_ONE_LANE, base + 3 + t] = 1.0
                pk[hp, hh, LANES + BIAS_ONE_LANE, base + t] = 1.0
                pk[hp, hh, LANES + 3 * (2 * hp + hh) + t, base + 3 + t] = -1.0
    return jnp.asarray(pq, BF16), jnp.asarray(pk.transpose(0, 1, 3, 2), BF16)


def _bias_placement_matrix():
    m = np.zeros((3 * LANES, LANES), np.float32)
    for h in range(H_ATT):
        for t in range(3):
            m[t * LANES + F_LANE0 + h, 3 * h + t] = 1.0
    return jnp.asarray(m, BF16)


def _pair_schedule(nq, unroll):
    assert nq % (2 * unroll) == 0
    slots = [[] for _ in range(unroll)]
    for i in range(nq // 2):
        slots[i % unroll] += [i, nq - 1 - i]
    full = [[(qb, li, kb) for li, qb in enumerate(qbs) for kb in range(qb + 1)] for qbs in slots]
    diag = [[(qb, li, qb + 1) for li, qb in enumerate(qbs)] for qbs in slots]
    g1, g2 = len(full[0]), len(diag[0])
    assert all(len(f) == g1 for f in full) and all(len(d) == g2 for d in diag)
    parts = []
    for sched in (full, diag):
        parts.append([qb * ATT_BLOCK for s in sched for (qb, _, _) in s])
        parts.append([li for s in sched for (_, li, _) in s])
        parts.append([kb for s in sched for (_, _, kb) in s])
    table = np.concatenate([np.asarray(p, np.int32) for p in parts])
    return slots, g1, g2, table


def _attn_kernel(tab_ref, q_ref, k_ref, v_ref, bias_ref, km_ref, vm_ref, bm_ref, pq_ref, pk_ref,
                 o_ref, kaug_scr, vsel_scr, qaug_scr, s_scr, p_scr, alpha_scr, m_scr, acc_scr,
                 *, slots, g1, g2):
    seq = q_ref.shape[1]
    blk = ATT_BLOCK
    unroll = len(slots)
    lane = lax.broadcasted_iota(jnp.int32, (1, LANES), 1)
    lo_half = lane < ATT_HEAD_DIM
    own = (lo_half, jnp.logical_not(lo_half))
    l_lane = (ATT_HEAD_DIM, 0)
    neg_inf = jnp.float32(-jnp.inf)

    def values_operand(vb, hh):
        keep = jnp.where(own[hh], vb, jnp.zeros_like(vb))
        return jnp.where(lane == l_lane[hh], jnp.ones_like(vb), keep)

    def keys_operand(kb, bias, hh):
        x = jnp.concatenate([kb, bias], axis=1)
        return lax.dot_general(pk_ref[0, hh], x, _NT, preferred_element_type=F32).astype(BF16)

    meta_bias = bm_ref[...]
    pad_bias = jnp.broadcast_to(meta_bias[0:1, :], (CHUNK, LANES))
    for hh in range(2):
        kaug_scr[hh, 0, :, 0:CHUNK] = keys_operand(km_ref[...], meta_bias, hh)
        kaug_scr[hh, 0, :, CHUNK:blk] = keys_operand(jnp.zeros((CHUNK, LANES), BF16), pad_bias, hh)
        vsel_scr[hh, 0:CHUNK, :] = values_operand(vm_ref[...], hh)
        vsel_scr[hh, CHUNK:blk, :] = jnp.zeros((blk - CHUNK, LANES), BF16)

    def prep_chunk(r, carry):
        rows = pl.ds(pl.multiple_of(r * blk, blk), blk)
        krows = pl.ds(pl.multiple_of((r + 1) * blk, blk), blk)
        bias = bias_ref[0, rows, :]
        xq = jnp.concatenate([q_ref[0, rows, :], bias], axis=1)
        kb = k_ref[0, rows, :]
        vb = v_ref[0, rows, :]
        for hh in range(2):
            qaug_scr[hh, rows, :] = jnp.dot(xq, pq_ref[0, hh],
                                            preferred_element_type=F32).astype(BF16)
            kaug_scr[hh, r + 1] = keys_operand(kb, bias, hh)
            vsel_scr[hh, krows, :] = values_operand(vb, hh)
        return carry

    lax.fori_loop(0, seq // blk, prep_chunk, 0, unroll=4)

    m_scr[...] = jnp.full(m_scr.shape, neg_inf, F32)
    acc_scr[...] = jnp.zeros(acc_scr.shape, F32)

    rr = lax.broadcasted_iota(jnp.int32, (blk, blk), 0)
    cc = lax.broadcasted_iota(jnp.int32, (blk, blk), 1)
    causal = cc <= rr

    def run(base, ngroups, masked, s_scr, p_scr, alpha_scr):
        def entry(field, u, g):
            return tab_ref[base + (field * unroll + u) * ngroups + g]

        def block_rows(off):
            return pl.ds(pl.multiple_of(off, blk), blk)

        def stage_a(g):
            for u in range(unroll):
                qrows = block_rows(entry(0, u, g))
                kb = entry(2, u, g)
                for hh in range(2):
                    s_scr[u, hh] = jnp.dot(qaug_scr[hh, qrows, :], kaug_scr[hh, kb],
                                           preferred_element_type=F32)

        def stage_b(g):
            for u in range(unroll):
                li = entry(1, u, g)
                for hh in range(2):
                    s = s_scr[u, hh]
                    if masked:
                        s = jnp.where(causal, s, neg_inf)
                    m_prev = m_scr[u, hh, li]
                    m_next = jnp.maximum(m_prev, jnp.max(s, axis=1, keepdims=True))
                    alpha_scr[u, hh] = jnp.exp2(m_prev - m_next)
                    m_scr[u, hh, li] = m_next
                    p_scr[u, hh] = jnp.exp2(
                        s - jnp.concatenate([m_next] * (blk // LANES), axis=1)).astype(BF16)

        def stage_c(g):
            for u in range(unroll):
                li = entry(1, u, g)
                krows = block_rows(entry(2, u, g) * blk)
                for hh in range(2):
                    pv = jnp.dot(p_scr[u, hh], vsel_scr[hh, krows, :], preferred_element_type=F32)
                    acc_scr[u, hh, li] = acc_scr[u, hh, li] * alpha_scr[u, hh] + pv

        stage_a(0)
        stage_b(0)
        stage_a(min(1, ngroups - 1))

        def body(g, carry):
            stage_c(g - 1)
            stage_b(g)
            stage_a(jnp.minimum(g + 1, ngroups - 1))
            return carry

        lax.fori_loop(1, ngroups, body, 0)
        stage_c(ngroups - 1)

    run(0, g1, False, s_scr.at[0], p_scr.at[0], alpha_scr.at[0])
    run(3 * unroll * g1, g2, True, s_scr.at[1], p_scr.at[1], alpha_scr.at[1])

    for u, qbs in enumerate(slots):
        for li, qb in enumerate(qbs):
            outs = []
            for hh in range(2):
                a = acc_scr[u, hh, li]
                outs.append(a / _head_column(a, l_lane[hh], lane))
            o_ref[0, qb * blk:(qb + 1) * blk, :] = jnp.where(lo_half, outs[0],
                                                             outs[1]).astype(o_ref.dtype)


def _attention(qkv, bias, qkv_meta, bias_meta, name):
    b, s, _ = qkv.shape
    npairs = H_ATT // 2
    blk = ATT_BLOCK
    slots, g1, g2, table = _pair_schedule(s // blk, ATT_UNROLL)
    nloc = len(slots[0])
    seq_blk = lambda col0: pl.BlockSpec((1, s, LANES), lambda bi, hp, tab: (bi, 0, col0 + hp))
    meta_blk = lambda col0: pl.BlockSpec((CHUNK, LANES), lambda bi, hp, tab: (0, col0 + hp))
    grid_spec = pltpu.PrefetchScalarGridSpec(
        num_scalar_prefetch=1,
        grid=(b, npairs),
        in_specs=[
            seq_blk(Q_BLOCK0), seq_blk(Q_BLOCK0 + npairs), seq_blk(Q_BLOCK0 + 2 * npairs),
            pl.BlockSpec((1, s, LANES), lambda bi, hp, tab: (bi, 0, 0)),
            meta_blk(Q_BLOCK0 + npairs), meta_blk(Q_BLOCK0 + 2 * npairs),
            pl.BlockSpec((CHUNK, LANES), lambda bi, hp, tab: (0, 0)),
            pl.BlockSpec((1, 2, 2 * LANES, LANES), lambda bi, hp, tab: (hp, 0, 0, 0)),
            pl.BlockSpec((1, 2, LANES, 2 * LANES), lambda bi, hp, tab: (hp, 0, 0, 0)),
        ],
        out_specs=pl.BlockSpec((1, s, LANES), lambda bi, hp, tab: (bi, 0, hp)),
        scratch_shapes=[
            pltpu.VMEM((2, s // blk + 1, LANES, blk), BF16),
            pltpu.VMEM((2, s + blk, LANES), BF16),
            pltpu.VMEM((2, s, LANES), BF16),
            pltpu.VMEM((2, ATT_UNROLL, 2, blk, blk), F32),
            pltpu.VMEM((2, ATT_UNROLL, 2, blk, blk), BF16),
            pltpu.VMEM((2, ATT_UNROLL, 2, blk, LANES), F32),
            pltpu.VMEM((ATT_UNROLL, 2, nloc, blk, LANES), F32),
            pltpu.VMEM((ATT_UNROLL, 2, nloc, blk, LANES), F32),
        ])
    return pl.pallas_call(
        functools.partial(_attn_kernel, slots=slots, g1=g1, g2=g2),
        grid_spec=grid_spec,
        out_shape=jax.ShapeDtypeStruct((b, s, D_ATT), BF16),
        compiler_params=pltpu.CompilerParams(
            dimension_semantics=("parallel", "arbitrary"),
            vmem_limit_bytes=VMEM_LIMIT),
        name=name,
    )(jnp.asarray(table), qkv, qkv, qkv, bias, qkv_meta, qkv_meta, bias_meta,
      *_placement_matrices())


def _merge_kernel(y_ref, zs_ref, o_ref, za_ref, gs_ref, ga_ref, x_ref, wps_ref, wpa_ref,
                  wout_ref, snorm_ref, gbias_ref, npost_ref, out_ref):
    zs = zs_ref[...].astype(F32)
    u = y_ref[...].astype(F32) * (zs * jax.nn.sigmoid(zs))
    parts = []
    for g in range(SSD_GROUPS):
        ug = u[:, g * GROUP_WIDTH:(g + 1) * GROUP_WIDTH]
        ms = jnp.mean(ug * ug, axis=-1, keepdims=True)
        parts.append((ug * lax.rsqrt(ms + EPS)
                      * snorm_ref[:, g * GROUP_WIDTH:(g + 1) * GROUP_WIDTH]).astype(BF16))
    y_ssd = jnp.concatenate(parts, axis=1)
    t_ssd = jnp.dot(y_ssd, wps_ref[...], preferred_element_type=F32)
    za = za_ref[...].astype(F32)
    y_att = (o_ref[...].astype(F32) * (za * jax.nn.sigmoid(za))).astype(BF16)
    t_att = jnp.dot(y_att, wpa_ref[...], preferred_element_type=F32)
    g_ssd = jax.nn.sigmoid(gs_ref[...].astype(F32) + gbias_ref[:, 0:D_MODEL])
    g_att = jax.nn.sigmoid(ga_ref[...].astype(F32) + gbias_ref[:, D_MODEL:2 * D_MODEL])
    merged = (g_ssd * t_ssd + g_att * t_att).astype(BF16)
    t = jnp.dot(merged, wout_ref[...], preferred_element_type=F32)
    ms = jnp.mean(t * t, axis=-1, keepdims=True)
    out_ref[...] = x_ref[...] + t * lax.rsqrt(ms + EPS) * npost_ref[...]


def _merge(y2d, p1, o2d, x2d, wps, wpa, wout, snorm, gbias, npost, tm, name):
    m = x2d.shape[0]
    row_blk = lambda width, col: pl.BlockSpec((tm, width), lambda i: (i, col))
    const = lambda shape: pl.BlockSpec(shape, lambda i: (0, 0), pipeline_mode=pl.Buffered(1))
    in_specs = [
        row_blk(D_SSD, 0),
        row_blk(D_SSD, 0),
        row_blk(D_ATT, 0),
        row_blk(D_ATT, 5),
        row_blk(D_MODEL, 6),
        row_blk(D_MODEL, 7),
        row_blk(D_MODEL, 0),
        const((D_SSD, D_MODEL)), const((D_ATT, D_MODEL)), const((D_MODEL, D_MODEL)),
        const((1, D_SSD)), const((1, 2 * D_MODEL)), const((1, D_MODEL)),
    ]
    return pl.pallas_call(
        _merge_kernel,
        grid=(m // tm,),
        in_specs=in_specs,
        out_specs=pl.BlockSpec((tm, D_MODEL), lambda i: (i, 0)),
        out_shape=jax.ShapeDtypeStruct((m, D_MODEL), F32),
        compiler_params=pltpu.CompilerParams(
            dimension_semantics=("parallel",),
            vmem_limit_bytes=VMEM_LIMIT),
        name=name,
    )(y2d, p1, o2d, p1, p1, p1, x2d, wps, wpa, wout, snorm, gbias, npost)


def kernel(x, meta_tokens, norm_pre, w_in, conv_w, conv_b, dt_bias, a_log, d_skip, ssd_norm,
           fgate_bias, gate_bias, w_proj_ssd, w_proj_att, w_out, norm_post):
    bsz, seq, d = x.shape
    assert d == D_MODEL and seq % 1024 == 0 and norm_pre.shape[0] == 1

    w = jnp.swapaxes(w_in[0], 0, 1)
    o_xbc = D_SSD
    o_dt = o_xbc + CONV_DIM
    o_za = o_dt + H_SSD
    o_q = o_za + D_ATT
    o_k = o_q + D_ATT
    o_v = o_k + D_ATT
    o_f = o_v + D_ATT
    o_g = o_f + H_ATT
    att_scale = LOG2E / (ATT_HEAD_DIM ** 0.5)
    w_all = jnp.concatenate(
        [w[0:o_dt].astype(BF16), w[o_za:o_q].astype(BF16), w[o_g:].astype(BF16),
         (w[o_q:o_k] * att_scale).astype(BF16), w[o_k:o_f].astype(BF16)], axis=0)
    w_small = jnp.concatenate(
        [w[o_dt:o_za], w[o_f:o_g],
         jnp.zeros((LANES - H_SSD - H_ATT, d), F32)], axis=0).astype(BF16)
    g_pre = norm_pre[0][None, :]
    pad_lanes = jnp.zeros((LANES - H_SSD - H_ATT,), F32)
    brow = jnp.concatenate([dt_bias[0], fgate_bias[0], pad_lanes])[None, :]
    alog_row = jnp.concatenate([a_log[0], jnp.zeros((LANES - H_SSD,), F32)])[None, :]
    dskip_row = jnp.repeat(d_skip[0], SSD_HEAD_DIM)[None, :]
    head_of_lane = jnp.arange(D_SSD, dtype=jnp.int32) // SSD_HEAD_DIM
    expand_mat = (jnp.arange(LANES, dtype=jnp.int32)[:, None] == head_of_lane[None, :]).astype(BF16)
    convw = conv_w[0]
    convb = conv_b[0][None, :]
    wps = w_proj_ssd[0].astype(BF16)
    wpa = w_proj_att[0].astype(BF16)
    wout = w_out[0].astype(BF16)
    snorm = ssd_norm[0][None, :]
    gbias = gate_bias[0][None, :]
    npost = norm_post[0][None, :]

    meta_blk = jnp.concatenate([jnp.zeros((CHUNK - N_META, d), F32), meta_tokens.astype(F32)], axis=0)
    p_m, small_m = _in_proj(meta_blk, g_pre, w_all, w_small, BF16, CHUNK, P_TILE_N, "in_proj_meta")
    st_zero = jnp.zeros((SSD_GROUPS, D_STATE, GROUP_WIDTH), F32)
    tail_zero = jnp.zeros((TAIL_ROWS, CONV_DIM), BF16)
    c_zero = jnp.zeros((1, LANES), F32)
    _, bias_m, st_m, tail_m, cc_m = _ssd(
        p_m[None], small_m[None], convw, convb, brow, alog_row, dskip_row, expand_mat,
        st_zero, tail_zero, c_zero, CHUNK - N_META, "ssd_meta")

    x2d = x.reshape(bsz * seq, d)
    p, small = _in_proj(x2d, g_pre, w_all, w_small, BF16, 1024, P_TILE_N, "in_proj")
    p3d = p.reshape(bsz, seq, P_COLS)
    y, bias, _, _, _ = _ssd(
        p3d, small.reshape(bsz, seq, LANES), convw, convb, brow,
        alog_row, dskip_row, expand_mat, st_m[0], tail_m[0], cc_m[0], 0, "ssd")
    o = _attention(p3d, bias, p_m, bias_m[0], "fox_attention")
    out = _merge(y.reshape(bsz * seq, D_SSD), p, o.reshape(bsz * seq, D_ATT), x2d,
                 wps, wpa, wout, snorm, gbias, npost, 512, "merge_out")
    return out.reshape(bsz, seq, d)
```

```python
import functools

import numpy as np

import jax
import jax.numpy as jnp
from jax import lax
from jax.experimental import pallas as pl
from jax.experimental.pallas import tpu as pltpu

F32 = jnp.float32
BF16 = jnp.bfloat16

D_MODEL = 1024
N_META = 16
CHUNK = 128
LANES = 128
D_SSD = 2 * D_MODEL
SSD_HEAD_DIM = 64
H_SSD = D_SSD // SSD_HEAD_DIM
SSD_GROUPS = 4
HEADS_PER_GROUP = H_SSD // SSD_GROUPS
D_STATE = 128
GROUP_WIDTH = HEADS_PER_GROUP * SSD_HEAD_DIM
CONV_K = 4
CONV_DIM = D_SSD + 2 * SSD_GROUPS * D_STATE
H_ATT = 16
ATT_HEAD_DIM = 64
D_ATT = H_ATT * ATT_HEAD_DIM
EPS = 1e-6
LOG2E = 1.4426950408889634
TAIL_ROWS = 16
F_LANE0 = H_SSD
P_COLS = 11 * 1024
P_TILE_N = P_COLS // 4
Q_BLOCK0 = 8 * (1024 // LANES)
VMEM_LIMIT = 56 * 1024 * 1024
ATT_BLOCK = 256
ATT_UNROLL = 4
MASK_BIAS = -1.0e30
BIAS_ONE_LANE = 3 * H_ATT

_NT = (((1,), (1,)), ((), ()))


def _split3(v):
    hi = v.astype(BF16)
    r1 = v - hi.astype(F32)
    mid = r1.astype(BF16)
    lo = (r1 - mid.astype(F32)).astype(BF16)
    return hi, mid, lo


def _dot_exact_lhs(a_bf16, v):
    hi, mid, lo = _split3(v)
    d = functools.partial(jnp.dot, preferred_element_type=F32)
    return d(a_bf16, hi) + d(a_bf16, mid) + d(a_bf16, lo)


def _dot_exact_rhs(v, b_bf16):
    hi, mid, lo = _split3(v)
    d = functools.partial(jnp.dot, preferred_element_type=F32)
    return d(hi, b_bf16) + d(mid, b_bf16) + d(lo, b_bf16)


def _in_proj_kernel(x_ref, g_ref, w_ref, *rest, with_small):
    if with_small:
        ws_ref, o_ref, os_ref, u_ref = rest
    else:
        o_ref, u_ref = rest
    j = pl.program_id(1)

    @pl.when(j == 0)
    def _():
        x = x_ref[...]
        ms = jnp.mean(x * x, axis=-1, keepdims=True)
        u = (x * lax.rsqrt(ms + EPS) * g_ref[...]).astype(BF16)
        u_ref[...] = u
        if with_small:
            os_ref[...] = lax.dot_general(u, ws_ref[...], _NT, preferred_element_type=F32)

    o_ref[...] = lax.dot_general(u_ref[...], w_ref[...], _NT,
                                 preferred_element_type=F32).astype(o_ref.dtype)


def _in_proj(x2d, g_row, w_t, w_small_t, out_dtype, tm, tn, name):
    m, d = x2d.shape
    n = w_t.shape[0]
    with_small = w_small_t is not None
    in_specs = [pl.BlockSpec((tm, d), lambda i, j: (i, 0)),
                pl.BlockSpec((1, d), lambda i, j: (0, 0)),
                pl.BlockSpec((tn, d), lambda i, j: (j, 0))]
    out_shape = [jax.ShapeDtypeStruct((m, n), out_dtype)]
    out_specs = [pl.BlockSpec((tm, tn), lambda i, j: (i, j))]
    args = [x2d, g_row, w_t]
    w_small = w_small_t
    if with_small:
        in_specs.append(pl.BlockSpec((LANES, d), lambda i, j: (0, 0)))
        out_shape.append(jax.ShapeDtypeStruct((m, LANES), F32))
        out_specs.append(pl.BlockSpec((tm, LANES), lambda i, j: (i, 0)))
        args.append(w_small)
    return pl.pallas_call(
        functools.partial(_in_proj_kernel, with_small=with_small),
        grid=(m // tm, n // tn),
        in_specs=in_specs,
        out_specs=out_specs,
        out_shape=out_shape,
        scratch_shapes=[pltpu.VMEM((tm, d), BF16)],
        compiler_params=pltpu.CompilerParams(
            dimension_semantics=("parallel", "arbitrary"),
            vmem_limit_bytes=VMEM_LIMIT),
        name=name,
    )(*args)


def _ssd_kernel(x0_ref, x1_ref, bc_ref, s_ref, convw_ref, convb_ref, brow_ref, alog_ref,
                dskip_ref, e_ref, shift_ref, pb_ref, st0_ref, tail0_ref, c0_ref,
                y_ref, bias_ref, st_ref, tail_ref, cc_ref,
                ext_ref, act_ref, *, first_valid):
    ci = pl.program_id(1)

    @pl.when(ci == 0)
    def _():
        st_ref[0] = st0_ref[...]
        tail_ref[0] = tail0_ref[...]
        cc_ref[0] = c0_ref[...]

    ext_ref[0:TAIL_ROWS, :] = tail_ref[0]
    ext_ref[TAIL_ROWS:TAIL_ROWS + CHUNK, 0:1024] = x0_ref[0]
    ext_ref[TAIL_ROWS:TAIL_ROWS + CHUNK, 1024:2048] = x1_ref[0]
    ext_ref[TAIL_ROWS:TAIL_ROWS + CHUNK, 2048:3072] = bc_ref[0]
    tail_ref[0] = ext_ref[CHUNK:CHUNK + TAIL_ROWS, :]

    row = lax.broadcasted_iota(jnp.int32, (CHUNK, 1), 0)
    valid_row = row >= first_valid
    slab = 2 * LANES
    for j in range(CONV_DIM // slab):
        sl = slice(j * slab, (j + 1) * slab)
        acc = (convb_ref[:, sl] + convw_ref[CONV_K - 1:CONV_K, sl]
               * ext_ref[TAIL_ROWS:TAIL_ROWS + CHUNK, sl].astype(F32))
        for k in range(CONV_K - 1):
            tap = jnp.dot(shift_ref[k * CHUNK:(k + 1) * CHUNK, :], ext_ref[:, sl],
                          preferred_element_type=F32)
            acc = acc + convw_ref[k:k + 1, sl] * tap
        a = acc * jax.nn.sigmoid(acc)
        if first_valid:
            a = jnp.where(valid_row, a, 0.0)
        act_ref[:, sl] = a

    lane = lax.broadcasted_iota(jnp.int32, (1, LANES), 1)
    is_dt = lane < H_SSD
    is_f = (lane >= F_LANE0) & (lane < F_LANE0 + H_ATT)
    sv = s_ref[0] + brow_ref[...]
    z = jnp.where(is_dt, sv, -sv)
    sp = jnp.maximum(z, 0.0) + jnp.log(1.0 + jnp.exp(-jnp.abs(z)))
    a_row = -jnp.exp(alog_ref[...])
    vals = jnp.where(is_dt, sp * a_row, jnp.where(is_f, -sp, 0.0))
    dt = jnp.where(is_dt, sp, 0.0)
    if first_valid:
        vals = jnp.where(valid_row, vals, 0.0)
        dt = jnp.where(valid_row, dt, 0.0)

    ri = lax.broadcasted_iota(jnp.int32, (CHUNK, CHUNK), 0)
    cj = lax.broadcasted_iota(jnp.int32, (CHUNK, CHUNK), 1)
    tri = cj <= ri
    cum = _dot_exact_lhs(tri.astype(BF16), vals)
    cfull = cum + cc_ref[0]
    c_log2 = LOG2E * cfull
    if first_valid:
        c_log2 = jnp.where(valid_row, c_log2, -MASK_BIAS)
    terms = jnp.concatenate(_split3(c_log2), axis=1)
    placed = jnp.dot(terms, pb_ref[...], preferred_element_type=F32)
    bias_ref[0] = jnp.where(lane == BIAS_ONE_LANE, 1.0, placed).astype(BF16)
    cum_t = cfull.T
    cc_ref[0] = jnp.where(is_f, cfull[CHUNK - 1:CHUNK, :], 0.0)
    dt_t = dt.T

    a_last = cum[CHUNK - 1:CHUNK, :]
    w_t = (dt * jnp.exp(a_last - cum)).T
    cd = jnp.where(is_dt, jnp.exp(a_last), 0.0)
    cd_e = _dot_exact_rhs(jnp.broadcast_to(cd, (8, LANES)), e_ref[...])[0:1]

    lo_half = lane < SSD_HEAD_DIM
    neg_inf = jnp.float32(-jnp.inf)

    def split_heads(x16):
        zero16 = jnp.zeros_like(x16)
        return jnp.concatenate([jnp.where(lo_half, x16, zero16),
                                jnp.where(lo_half, zero16, x16)], axis=0)

    for g in range(SSD_GROUPS):
        bm = act_ref[:, D_SSD + g * D_STATE:D_SSD + (g + 1) * D_STATE]
        cm = act_ref[:, D_SSD + SSD_GROUPS * D_STATE + g * D_STATE:
                     D_SSD + SSD_GROUPS * D_STATE + (g + 1) * D_STATE]
        cb = lax.dot_general(cm.astype(BF16), bm.astype(BF16), _NT,
                             preferred_element_type=F32)
        bm_t = bm.T
        for pr in range(HEADS_PER_GROUP // 2):
            psl = slice(g * GROUP_WIDTH + pr * LANES, g * GROUP_WIDTH + (pr + 1) * LANES)
            ssl = slice(pr * LANES, (pr + 1) * LANES)
            xs_p = act_ref[:, psl]
            hin = st_ref[0, g, :, ssl]
            intra, inter, bw = [], [], []
            for h in (g * HEADS_PER_GROUP + 2 * pr, g * HEADS_PER_GROUP + 2 * pr + 1):
                a_col = cum[:, h:h + 1]
                seg = a_col - cum_t[h:h + 1, :]
                dec = jnp.exp(jnp.where(tri, seg, neg_inf))
                intra.append((cb * dec * dt_t[h:h + 1, :]).astype(BF16))
                inter.append((cm * jnp.exp(a_col)).astype(BF16))
                bw.append((bm_t * w_t[h:h + 1, :]).astype(BF16))
            xs2 = split_heads(xs_p.astype(BF16))
            y_lhs = jnp.concatenate(intra + inter, axis=1)
            y_rhs = jnp.concatenate([xs2, split_heads(hin.astype(BF16))], axis=0)
            y_pair = jnp.dot(y_lhs, y_rhs, preferred_element_type=F32)
            y_ref[0, :, psl] = (y_pair + dskip_ref[:, psl] * xs_p).astype(y_ref.dtype)
            st_ref[0, g, :, ssl] = hin * cd_e[:, psl] + jnp.dot(
                jnp.concatenate(bw, axis=1), xs2, preferred_element_type=F32)


def _conv_shift_matrix():
    m = np.zeros(((CONV_K - 1) * CHUNK, TAIL_ROWS + CHUNK), np.float32)
    for k in range(CONV_K - 1):
        for t in range(CHUNK):
            m[k * CHUNK + t, TAIL_ROWS + t - (CONV_K - 1) + k] = 1.0
    return jnp.asarray(m, BF16)


def _ssd(p1, small, convw, convb, brow, alog_row, dskip_row, expand_mat, st0, tail0, c0,
         first_valid, name):
    b, s, _ = p1.shape
    nc = s // CHUNK
    const = lambda shape: pl.BlockSpec(shape, lambda bi, ci: (0,) * len(shape))
    in_specs = [
        pl.BlockSpec((1, CHUNK, 1024), lambda bi, ci: (bi, ci, 2)),
        pl.BlockSpec((1, CHUNK, 1024), lambda bi, ci: (bi, ci, 3)),
        pl.BlockSpec((1, CHUNK, 1024), lambda bi, ci: (bi, ci, 4)),
        pl.BlockSpec((1, CHUNK, LANES), lambda bi, ci: (bi, ci, 0)),
        const((CONV_K, CONV_DIM)), const((1, CONV_DIM)), const((1, LANES)), const((1, LANES)),
        const((1, D_SSD)), const((LANES, D_SSD)), const(((CONV_K - 1) * CHUNK, TAIL_ROWS + CHUNK)),
        const((3 * LANES, LANES)),
        const((SSD_GROUPS, D_STATE, GROUP_WIDTH)), const((TAIL_ROWS, CONV_DIM)), const((1, LANES)),
    ]
    out_shape = [
        jax.ShapeDtypeStruct((b, s, D_SSD), BF16),
        jax.ShapeDtypeStruct((b, s, LANES), BF16),
        jax.ShapeDtypeStruct((b, SSD_GROUPS, D_STATE, GROUP_WIDTH), F32),
        jax.ShapeDtypeStruct((b, TAIL_ROWS, CONV_DIM), BF16),
        jax.ShapeDtypeStruct((b, 1, LANES), F32),
    ]
    out_specs = [
        pl.BlockSpec((1, CHUNK, D_SSD), lambda bi, ci: (bi, ci, 0)),
        pl.BlockSpec((1, CHUNK, LANES), lambda bi, ci: (bi, ci, 0)),
        pl.BlockSpec((1, SSD_GROUPS, D_STATE, GROUP_WIDTH), lambda bi, ci: (bi, 0, 0, 0)),
        pl.BlockSpec((1, TAIL_ROWS, CONV_DIM), lambda bi, ci: (bi, 0, 0)),
        pl.BlockSpec((1, 1, LANES), lambda bi, ci: (bi, 0, 0)),
    ]
    return pl.pallas_call(
        functools.partial(_ssd_kernel, first_valid=first_valid),
        grid=(b, nc),
        in_specs=in_specs,
        out_specs=out_specs,
        out_shape=out_shape,
        scratch_shapes=[pltpu.VMEM((TAIL_ROWS + CHUNK, CONV_DIM), BF16),
                        pltpu.VMEM((CHUNK, CONV_DIM), F32)],
        compiler_params=pltpu.CompilerParams(
            dimension_semantics=("parallel", "arbitrary"),
            vmem_limit_bytes=VMEM_LIMIT),
        name=name,
    )(p1, p1, p1, small, convw, convb, brow, alog_row, dskip_row, expand_mat,
      _conv_shift_matrix(), _bias_placement_matrix(), st0, tail0, c0)


V_ROWS = 80
L_ROW = ATT_HEAD_DIM
STAT_ROWS = 8


def _placement_matrices():
    npairs = H_ATT // 2
    pq = np.zeros((npairs, 2, 2 * LANES, LANES), np.float32)
    pk = np.zeros((npairs, 2, 2 * LANES, LANES), np.float32)
    pv = np.zeros((2, V_ROWS, LANES), np.float32)
    for hp in range(npairs):
        for hh in range(2):
            own0 = hh * ATT_HEAD_DIM
            base = (1 - hh) * ATT_HEAD_DIM
            for l in range(own0, own0 + ATT_HEAD_DIM):
                pq[hp, hh, l, l] = 1.0
                pk[hp, hh, l, l] = 1.0
            for t in range(3):
                pq[hp, hh, LANES + 3 * (2 * hp + hh) + t, base + t] = 1.0
                pq[hp, hh, LANES + BIAS_ONE_LANE, base + 3 + t] = 1.0
                pk[hp, hh, LANES + BIAS_ONE_LANE, base + t] = 1.0
                pk[hp, hh, LANES + 3 * (2 * hp + hh) + t, base + 3 + t] = -1.0
    for hh in range(2):
        for r in range(ATT_HEAD_DIM):
            pv[hh, r, hh * ATT_HEAD_DIM + r] = 1.0
    return (jnp.asarray(pq.transpose(0, 1, 3, 2), BF16), jnp.asarray(pk, BF16),
            jnp.asarray(pv, BF16))


def _bias_placement_matrix():
    m = np.zeros((3 * LANES, LANES), np.float32)
    for h in range(H_ATT):
        for t in range(3):
            m[t * LANES + F_LANE0 + h, 3 * h + t] = 1.0
    return jnp.asarray(m, BF16)


def _pair_schedule(nq, unroll):
    assert nq % (2 * unroll) == 0
    slots = [[] for _ in range(unroll)]
    for i in range(nq // 2):
        slots[i % unroll] += [i, nq - 1 - i]
    full = [[(qb, li, kb) for li, qb in enumerate(qbs) for kb in range(qb + 1)] for qbs in slots]
    diag = [[(qb, li, qb + 1) for li, qb in enumerate(qbs)] for qbs in slots]
    g1, g2 = len(full[0]), len(diag[0])
    assert all(len(f) == g1 for f in full) and all(len(d) == g2 for d in diag)
    parts = []
    for sched in (full, diag):
        parts.append([qb for s in sched for (qb, _, _) in s])
        parts.append([li for s in sched for (_, li, _) in s])
        parts.append([kb for s in sched for (_, _, kb) in s])
    table = np.concatenate([np.asarray(p, np.int32) for p in parts])
    return slots, g1, g2, table


def _attn_kernel(tab_ref, q_ref, k_ref, v_ref, bias_ref, km_ref, vm_ref, bm_ref, pq_ref, pk_ref,
                 pv_ref, o_ref, kaug_scr, vsel_scr, qaug_scr, s_scr, p_scr, alpha_scr, m_scr,
                 acc_scr, *, slots, g1, g2):
    seq = q_ref.shape[1]
    blk = ATT_BLOCK
    unroll = len(slots)
    neg_inf = jnp.float32(-jnp.inf)
    vrow = lax.broadcasted_iota(jnp.int32, (V_ROWS, 1), 0)

    def keys_operand(kb, bias, hh):
        x = jnp.concatenate([kb, bias], axis=1)
        return jnp.dot(x, pk_ref[0, hh], preferred_element_type=F32).astype(BF16)

    def values_operand(vb, hh):
        vt = lax.dot_general(pv_ref[hh], vb, _NT, preferred_element_type=F32)
        return jnp.where(vrow == L_ROW, 1.0, vt).astype(BF16)

    meta_bias = bm_ref[...]
    pad_bias = jnp.broadcast_to(meta_bias[0:1, :], (CHUNK, LANES))
    zero_blk = jnp.zeros((CHUNK, LANES), BF16)
    for hh in range(2):
        kaug_scr[hh, 0:CHUNK, :] = keys_operand(km_ref[...], meta_bias, hh)
        kaug_scr[hh, CHUNK:blk, :] = keys_operand(zero_blk, pad_bias, hh)
        vsel_scr[hh, 0] = values_operand(jnp.concatenate([vm_ref[...], zero_blk], axis=0), hh)

    def prep_chunk(r, carry):
        rows = pl.ds(pl.multiple_of(r * blk, blk), blk)
        krows = pl.ds(pl.multiple_of((r + 1) * blk, blk), blk)
        bias = bias_ref[0, rows, :]
        xq = jnp.concatenate([q_ref[0, rows, :], bias], axis=1)
        kb = k_ref[0, rows, :]
        vb = v_ref[0, rows, :]
        for hh in range(2):
            qaug_scr[hh, r] = lax.dot_general(pq_ref[0, hh], xq, _NT,
                                              preferred_element_type=F32).astype(BF16)
            kaug_scr[hh, krows, :] = keys_operand(kb, bias, hh)
            vsel_scr[hh, r + 1] = values_operand(vb, hh)
        return carry

    lax.fori_loop(0, seq // blk, prep_chunk, 0, unroll=4)

    m_scr[...] = jnp.full(m_scr.shape, neg_inf, F32)
    acc_scr[...] = jnp.zeros(acc_scr.shape, F32)

    key_idx = lax.broadcasted_iota(jnp.int32, (blk, blk), 0)
    query_idx = lax.broadcasted_iota(jnp.int32, (blk, blk), 1)
    causal = key_idx <= query_idx
    stat_rows = m_scr.shape[-2]

    def run(base, ngroups, masked, s_scr, p_scr, alpha_scr):
        def entry(field, u, g):
            return tab_ref[base + (field * unroll + u) * ngroups + g]

        def stage_a(g):
            for u in range(unroll):
                qb = entry(0, u, g)
                krows = pl.ds(pl.multiple_of(entry(2, u, g) * blk, blk), blk)
                for hh in range(2):
                    s_scr[u, hh] = jnp.dot(kaug_scr[hh, krows, :], qaug_scr[hh, qb],
                                           preferred_element_type=F32)

        def stage_b(g):
            for u in range(unroll):
                li = entry(1, u, g)
                for hh in range(2):
                    s = s_scr[u, hh]
                    if masked:
                        s = jnp.where(causal, s, neg_inf)
                    m_prev = m_scr[u, hh, li][0:1]
                    m_next = jnp.maximum(m_prev, jnp.max(s, axis=0, keepdims=True))
                    alpha_scr[u, hh] = jnp.broadcast_to(jnp.exp2(m_prev - m_next),
                                                        (stat_rows, blk))
                    m_scr[u, hh, li] = jnp.broadcast_to(m_next, (stat_rows, blk))
                    p_scr[u, hh] = jnp.exp2(s - m_next).astype(BF16)

        def stage_c(g):
            for u in range(unroll):
                li = entry(1, u, g)
                kb = entry(2, u, g)
                for hh in range(2):
                    pv = jnp.dot(vsel_scr[hh, kb], p_scr[u, hh], preferred_element_type=F32)
                    acc_scr[u, hh, li] = acc_scr[u, hh, li] * alpha_scr[u, hh][0:1] + pv

        stage_a(0)
        stage_b(0)
        stage_a(min(1, ngroups - 1))

        def body(g, carry):
            stage_c(g - 1)
            stage_b(g)
            stage_a(jnp.minimum(g + 1, ngroups - 1))
            return carry

        lax.fori_loop(1, ngroups, body, 0)
        stage_c(ngroups - 1)

    run(0, g1, False, s_scr.at[0], p_scr.at[0], alpha_scr.at[0])
    run(3 * unroll * g1, g2, True, s_scr.at[1], p_scr.at[1], alpha_scr.at[1])

    for u, qbs in enumerate(slots):
        for li, qb in enumerate(qbs):
            outs = []
            for hh in range(2):
                a = acc_scr[u, hh, li]
                outs.append(a[0:ATT_HEAD_DIM] / a[L_ROW:L_ROW + 1])
            o_ref[0, qb * blk:(qb + 1) * blk, :] = jnp.concatenate(outs, axis=0).T.astype(o_ref.dtype)


def _attention(qkv, bias, qkv_meta, bias_meta, name):
    b, s, _ = qkv.shape
    npairs = H_ATT // 2
    blk = ATT_BLOCK
    slots, g1, g2, table = _pair_schedule(s // blk, ATT_UNROLL)
    nloc = len(slots[0])
    seq_blk = lambda col0: pl.BlockSpec((1, s, LANES), lambda bi, hp, tab: (bi, 0, col0 + hp))
    meta_blk = lambda col0: pl.BlockSpec((CHUNK, LANES), lambda bi, hp, tab: (0, col0 + hp))
    grid_spec = pltpu.PrefetchScalarGridSpec(
        num_scalar_prefetch=1,
        grid=(b, npairs),
        in_specs=[
            seq_blk(Q_BLOCK0), seq_blk(Q_BLOCK0 + npairs), seq_blk(Q_BLOCK0 + 2 * npairs),
            pl.BlockSpec((1, s, LANES), lambda bi, hp, tab: (bi, 0, 0)),
            meta_blk(Q_BLOCK0 + npairs), meta_blk(Q_BLOCK0 + 2 * npairs),
            pl.BlockSpec((CHUNK, LANES), lambda bi, hp, tab: (0, 0)),
            pl.BlockSpec((1, 2, LANES, 2 * LANES), lambda bi, hp, tab: (hp, 0, 0, 0)),
            pl.BlockSpec((1, 2, 2 * LANES, LANES), lambda bi, hp, tab: (hp, 0, 0, 0)),
            pl.BlockSpec((2, V_ROWS, LANES), lambda bi, hp, tab: (0, 0, 0)),
        ],
        out_specs=pl.BlockSpec((1, s, LANES), lambda bi, hp, tab: (bi, 0, hp)),
        scratch_shapes=[
            pltpu.VMEM((2, s + blk, LANES), BF16),
            pltpu.VMEM((2, s // blk + 1, V_ROWS, blk), BF16),
            pltpu.VMEM((2, s // blk, LANES, blk), BF16),
            pltpu.VMEM((2, ATT_UNROLL, 2, blk, blk), F32),
            pltpu.VMEM((2, ATT_UNROLL, 2, blk, blk), BF16),
            pltpu.VMEM((2, ATT_UNROLL, 2, STAT_ROWS, blk), F32),
            pltpu.VMEM((ATT_UNROLL, 2, nloc, STAT_ROWS, blk), F32),
            pltpu.VMEM((ATT_UNROLL, 2, nloc, V_ROWS, blk), F32),
        ])
    return pl.pallas_call(
        functools.partial(_attn_kernel, slots=slots, g1=g1, g2=g2),
        grid_spec=grid_spec,
        out_shape=jax.ShapeDtypeStruct((b, s, D_ATT), BF16),
        compiler_params=pltpu.CompilerParams(
            dimension_semantics=("parallel", "arbitrary"),
            vmem_limit_bytes=VMEM_LIMIT),
        name=name,
    )(jnp.asarray(table), qkv, qkv, qkv, bias, qkv_meta, qkv_meta, bias_meta,
      *_placement_matrices())


def _merge_kernel(y_ref, zs_ref, o_ref, za_ref, gs_ref, ga_ref, x_ref, wps_ref, wpa_ref,
                  wout_ref, snorm_ref, gbias_ref, npost_ref, out_ref):
    zs = zs_ref[...].astype(F32)
    u = y_ref[...].astype(F32) * (zs * jax.nn.sigmoid(zs))
    parts = []
    for g in range(SSD_GROUPS):
        ug = u[:, g * GROUP_WIDTH:(g + 1) * GROUP_WIDTH]
        ms = jnp.mean(ug * ug, axis=-1, keepdims=True)
        parts.append((ug * lax.rsqrt(ms + EPS)
                      * snorm_ref[:, g * GROUP_WIDTH:(g + 1) * GROUP_WIDTH]).astype(BF16))
    y_ssd = jnp.concatenate(parts, axis=1)
    t_ssd = jnp.dot(y_ssd, wps_ref[...], preferred_element_type=F32)
    za = za_ref[...].astype(F32)
    y_att = (o_ref[...].astype(F32) * (za * jax.nn.sigmoid(za))).astype(BF16)
    t_att = jnp.dot(y_att, wpa_ref[...], preferred_element_type=F32)
    g_ssd = jax.nn.sigmoid(gs_ref[...].astype(F32) + gbias_ref[:, 0:D_MODEL])
    g_att = jax.nn.sigmoid(ga_ref[...].astype(F32) + gbias_ref[:, D_MODEL:2 * D_MODEL])
    merged = (g_ssd * t_ssd + g_att * t_att).astype(BF16)
    t = jnp.dot(merged, wout_ref[...], preferred_element_type=F32)
    ms = jnp.mean(t * t, axis=-1, keepdims=True)
    out_ref[...] = x_ref[...] + t * lax.rsqrt(ms + EPS) * npost_ref[...]


def _merge(y2d, p1, o2d, x2d, wps, wpa, wout, snorm, gbias, npost, tm, name):
    m = x2d.shape[0]
    row_blk = lambda width, col: pl.BlockSpec((tm, width), lambda i: (i, col))
    const = lambda shape: pl.BlockSpec(shape, lambda i: (0, 0), pipeline_mode=pl.Buffered(1))
    in_specs = [
        row_blk(D_SSD, 0),
        row_blk(D_SSD, 0),
        row_blk(D_ATT, 0),
        row_blk(D_ATT, 5),
        row_blk(D_MODEL, 6),
        row_blk(D_MODEL, 7),
        row_blk(D_MODEL, 0),
        const((D_SSD, D_MODEL)), const((D_ATT, D_MODEL)), const((D_MODEL, D_MODEL)),
        const((1, D_SSD)), const((1, 2 * D_MODEL)), const((1, D_MODEL)),
    ]
    return pl.pallas_call(
        _merge_kernel,
        grid=(m // tm,),
        in_specs=in_specs,
        out_specs=pl.BlockSpec((tm, D_MODEL), lambda i: (i, 0)),
        out_shape=jax.ShapeDtypeStruct((m, D_MODEL), F32),
        compiler_params=pltpu.CompilerParams(
            dimension_semantics=("parallel",),
            vmem_limit_bytes=VMEM_LIMIT),
        name=name,
    )(y2d, p1, o2d, p1, p1, p1, x2d, wps, wpa, wout, snorm, gbias, npost)


def kernel(x, meta_tokens, norm_pre, w_in, conv_w, conv_b, dt_bias, a_log, d_skip, ssd_norm,
           fgate_bias, gate_bias, w_proj_ssd, w_proj_att, w_out, norm_post):
    bsz, seq, d = x.shape
    assert d == D_MODEL and seq % 1024 == 0 and norm_pre.shape[0] == 1

    w = jnp.swapaxes(w_in[0], 0, 1)
    o_xbc = D_SSD
    o_dt = o_xbc + CONV_DIM
    o_za = o_dt + H_SSD
    o_q = o_za + D_ATT
    o_k = o_q + D_ATT
    o_v = o_k + D_ATT
    o_f = o_v + D_ATT
    o_g = o_f + H_ATT
    att_scale = LOG2E / (ATT_HEAD_DIM ** 0.5)
    w_all = jnp.concatenate(
        [w[0:o_dt].astype(BF16), w[o_za:o_q].astype(BF16), w[o_g:].astype(BF16),
         (w[o_q:o_k] * att_scale).astype(BF16), w[o_k:o_f].astype(BF16)], axis=0)
    w_small = jnp.concatenate(
        [w[o_dt:o_za], w[o_f:o_g],
         jnp.zeros((LANES - H_SSD - H_ATT, d), F32)], axis=0).astype(BF16)
    g_pre = norm_pre[0][None, :]
    pad_lanes = jnp.zeros((LANES - H_SSD - H_ATT,), F32)
    brow = jnp.concatenate([dt_bias[0], fgate_bias[0], pad_lanes])[None, :]
    alog_row = jnp.concatenate([a_log[0], jnp.zeros((LANES - H_SSD,), F32)])[None, :]
    dskip_row = jnp.repeat(d_skip[0], SSD_HEAD_DIM)[None, :]
    head_of_lane = jnp.arange(D_SSD, dtype=jnp.int32) // SSD_HEAD_DIM
    expand_mat = (jnp.arange(LANES, dtype=jnp.int32)[:, None] == head_of_lane[None, :]).astype(BF16)
    convw = conv_w[0]
    convb = conv_b[0][None, :]
    wps = w_proj_ssd[0].astype(BF16)
    wpa = w_proj_att[0].astype(BF16)
    wout = w_out[0].astype(BF16)
    snorm = ssd_norm[0][None, :]
    gbias = gate_bias[0][None, :]
    npost = norm_post[0][None, :]

    meta_blk = jnp.concatenate([jnp.zeros((CHUNK - N_META, d), F32), meta_tokens.astype(F32)], axis=0)
    p_m, small_m = _in_proj(meta_blk, g_pre, w_all, w_small, BF16, CHUNK, P_TILE_N, "in_proj_meta")
    st_zero = jnp.zeros((SSD_GROUPS, D_STATE, GROUP_WIDTH), F32)
    tail_zero = jnp.zeros((TAIL_ROWS, CONV_DIM), BF16)
    c_zero = jnp.zeros((1, LANES), F32)
    _, bias_m, st_m, tail_m, cc_m = _ssd(
        p_m[None], small_m[None], convw, convb, brow, alog_row, dskip_row, expand_mat,
        st_zero, tail_zero, c_zero, CHUNK - N_META, "ssd_meta")

    x2d = x.reshape(bsz * seq, d)
    p, small = _in_proj(x2d, g_pre, w_all, w_small, BF16, 1024, P_TILE_N, "in_proj")
    p3d = p.reshape(bsz, seq, P_COLS)
    y, bias, _, _, _ = _ssd(
        p3d, small.reshape(bsz, seq, LANES), convw, convb, brow,
        alog_row, dskip_row, expand_mat, st_m[0], tail_m[0], cc_m[0], 0, "ssd")
    o = _attention(p3d, bias, p_m, bias_m[0], "fox_attention")
    out = _merge(y.reshape(bsz * seq, D_SSD), p, o.reshape(bsz * seq, D_ATT), x2d,
                 wps, wpa, wout, snorm, gbias, npost, 512, "merge_out")
    return out.reshape(bsz, seq, d)
```

```python
import functools

import numpy as np

import jax
import jax.numpy as jnp
from jax import lax
from jax.experimental import pallas as pl
from jax.experimental.pallas import tpu as pltpu

F32 = jnp.float32
BF16 = jnp.bfloat16

D_MODEL = 1024
N_META = 16
CHUNK = 128
LANES = 128
D_SSD = 2 * D_MODEL
SSD_HEAD_DIM = 64
H_SSD = D_SSD // SSD_HEAD_DIM
SSD_GROUPS = 4
HEADS_PER_GROUP = H_SSD // SSD_GROUPS
D_STATE = 128
GROUP_WIDTH = HEADS_PER_GROUP * SSD_HEAD_DIM
CONV_K = 4
CONV_DIM = D_SSD + 2 * SSD_GROUPS * D_STATE
H_ATT = 16
ATT_HEAD_DIM = 64
D_ATT = H_ATT * ATT_HEAD_DIM
EPS = 1e-6
LOG2E = 1.4426950408889634
TAIL_ROWS = 16
F_LANE0 = H_SSD
P_COLS = 11 * 1024
P_TILE_N = P_COLS // 4
Q_BLOCK0 = 8 * (1024 // LANES)
VMEM_LIMIT = 56 * 1024 * 1024
ATT_BLOCK = 256
ATT_UNROLL = 4
MASK_BIAS = -1.0e30
BIAS_ONE_LANE = 3 * H_ATT

_NT = (((1,), (1,)), ((), ()))


def _split3(v):
    hi = v.astype(BF16)
    r1 = v - hi.astype(F32)
    mid = r1.astype(BF16)
    lo = (r1 - mid.astype(F32)).astype(BF16)
    return hi, mid, lo


def _dot_exact_lhs(a_bf16, v):
    hi, mid, lo = _split3(v)
    d = functools.partial(jnp.dot, preferred_element_type=F32)
    return d(a_bf16, hi) + d(a_bf16, mid) + d(a_bf16, lo)


def _dot_exact_rhs(v, b_bf16):
    hi, mid, lo = _split3(v)
    d = functools.partial(jnp.dot, preferred_element_type=F32)
    return d(hi, b_bf16) + d(mid, b_bf16) + d(lo, b_bf16)


def _in_proj_kernel(x_ref, g_ref, w_ref, *rest, with_small):
    if with_small:
        ws_ref, o_ref, os_ref, u_ref = rest
    else:
        o_ref, u_ref = rest
    j = pl.program_id(1)

    @pl.when(j == 0)
    def _():
        x = x_ref[...]
        ms = jnp.mean(x * x, axis=-1, keepdims=True)
        u = (x * lax.rsqrt(ms + EPS) * g_ref[...]).astype(BF16)
        u_ref[...] = u
        if with_small:
            os_ref[...] = lax.dot_general(u, ws_ref[...], _NT, preferred_element_type=F32)

    o_ref[...] = lax.dot_general(u_ref[...], w_ref[...], _NT,
                                 preferred_element_type=F32).astype(o_ref.dtype)


def _in_proj(x2d, g_row, w_t, w_small_t, out_dtype, tm, tn, name):
    m, d = x2d.shape
    n = w_t.shape[0]
    with_small = w_small_t is not None
    in_specs = [pl.BlockSpec((tm, d), lambda i, j: (i, 0)),
                pl.BlockSpec((1, d), lambda i, j: (0, 0)),
                pl.BlockSpec((tn, d), lambda i, j: (j, 0))]
    out_shape = [jax.ShapeDtypeStruct((m, n), out_dtype)]
    out_specs = [pl.BlockSpec((tm, tn), lambda i, j: (i, j))]
    args = [x2d, g_row, w_t]
    w_small = w_small_t
    if with_small:
        in_specs.append(pl.BlockSpec((LANES, d), lambda i, j: (0, 0)))
        out_shape.append(jax.ShapeDtypeStruct((m, LANES), F32))
        out_specs.append(pl.BlockSpec((tm, LANES), lambda i, j: (i, 0)))
        args.append(w_small)
    return pl.pallas_call(
        functools.partial(_in_proj_kernel, with_small=with_small),
        grid=(m // tm, n // tn),
        in_specs=in_specs,
        out_specs=out_specs,
        out_shape=out_shape,
        scratch_shapes=[pltpu.VMEM((tm, d), BF16)],
        compiler_params=pltpu.CompilerParams(
            dimension_semantics=("parallel", "arbitrary"),
            vmem_limit_bytes=VMEM_LIMIT),
        name=name,
    )(*args)


def _ssd_kernel(x0_ref, x1_ref, bc_ref, s_ref, convw_ref, convb_ref, brow_ref, alog_ref,
                dskip_ref, e_ref, shift_ref, pb_ref, st0_ref, tail0_ref, c0_ref,
                y_ref, bias_ref, st_ref, tail_ref, cc_ref,
                ext_ref, act_ref, *, first_valid):
    ci = pl.program_id(1)

    @pl.when(ci == 0)
    def _():
        st_ref[0] = st0_ref[...]
        tail_ref[0] = tail0_ref[...]
        cc_ref[0] = c0_ref[...]

    ext_ref[0:TAIL_ROWS, :] = tail_ref[0]
    ext_ref[TAIL_ROWS:TAIL_ROWS + CHUNK, 0:1024] = x0_ref[0]
    ext_ref[TAIL_ROWS:TAIL_ROWS + CHUNK, 1024:2048] = x1_ref[0]
    ext_ref[TAIL_ROWS:TAIL_ROWS + CHUNK, 2048:3072] = bc_ref[0]
    tail_ref[0] = ext_ref[CHUNK:CHUNK + TAIL_ROWS, :]

    row = lax.broadcasted_iota(jnp.int32, (CHUNK, 1), 0)
    valid_row = row >= first_valid
    shift = shift_ref[...]
    for j in range(CONV_DIM // GROUP_WIDTH):
        sl = slice(j * GROUP_WIDTH, (j + 1) * GROUP_WIDTH)
        taps = jnp.dot(shift, ext_ref[:, sl], preferred_element_type=F32)
        acc = (convb_ref[:, sl] + convw_ref[CONV_K - 1:CONV_K, sl]
               * ext_ref[TAIL_ROWS:TAIL_ROWS + CHUNK, sl].astype(F32))
        for k in range(CONV_K - 1):
            acc = acc + convw_ref[k:k + 1, sl] * taps[k * CHUNK:(k + 1) * CHUNK]
        a = acc * jax.nn.sigmoid(acc)
        if first_valid:
            a = jnp.where(valid_row, a, 0.0)
        act_ref[:, sl] = a

    lane = lax.broadcasted_iota(jnp.int32, (1, LANES), 1)
    is_dt = lane < H_SSD
    is_f = (lane >= F_LANE0) & (lane < F_LANE0 + H_ATT)
    sv = s_ref[0] + brow_ref[...]
    z = jnp.where(is_dt, sv, -sv)
    sp = jnp.maximum(z, 0.0) + jnp.log(1.0 + jnp.exp(-jnp.abs(z)))
    a_row = -jnp.exp(alog_ref[...])
    vals = jnp.where(is_dt, sp * a_row, jnp.where(is_f, -sp, 0.0))
    dt = jnp.where(is_dt, sp, 0.0)
    if first_valid:
        vals = jnp.where(valid_row, vals, 0.0)
        dt = jnp.where(valid_row, dt, 0.0)

    ri = lax.broadcasted_iota(jnp.int32, (CHUNK, CHUNK), 0)
    cj = lax.broadcasted_iota(jnp.int32, (CHUNK, CHUNK), 1)
    tri = cj <= ri
    cum = _dot_exact_lhs(tri.astype(BF16), vals)
    cfull = cum + cc_ref[0]
    c_log2 = LOG2E * cfull
    if first_valid:
        c_log2 = jnp.where(valid_row, c_log2, -MASK_BIAS)
    terms = jnp.concatenate(_split3(c_log2), axis=1)
    placed = jnp.dot(terms, pb_ref[...], preferred_element_type=F32)
    bias_ref[0] = jnp.where(lane == BIAS_ONE_LANE, 1.0, placed).astype(BF16)
    cum_t = cfull.T
    cc_ref[0] = jnp.where(is_f, cfull[CHUNK - 1:CHUNK, :], 0.0)
    dt_t = dt.T

    a_last = cum[CHUNK - 1:CHUNK, :]
    w_t = (dt * jnp.exp(a_last - cum)).T
    cd = jnp.where(is_dt, jnp.exp(a_last), 0.0)
    cd_e = _dot_exact_rhs(jnp.broadcast_to(cd, (8, LANES)), e_ref[...])[0:1]

    lo_half = lane < SSD_HEAD_DIM
    neg_inf = jnp.float32(-jnp.inf)

    def split_heads(x16):
        zero16 = jnp.zeros_like(x16)
        return jnp.concatenate([jnp.where(lo_half, x16, zero16),
                                jnp.where(lo_half, zero16, x16)], axis=0)

    for g in range(SSD_GROUPS):
        bm = act_ref[:, D_SSD + g * D_STATE:D_SSD + (g + 1) * D_STATE]
        cm = act_ref[:, D_SSD + SSD_GROUPS * D_STATE + g * D_STATE:
                     D_SSD + SSD_GROUPS * D_STATE + (g + 1) * D_STATE]
        cm16 = cm.astype(BF16)
        cb = lax.dot_general(cm16, bm.astype(BF16), _NT,
                             preferred_element_type=F32).astype(BF16)
        bm_t = bm.T.astype(BF16)
        for pr in range(HEADS_PER_GROUP // 2):
            psl = slice(g * GROUP_WIDTH + pr * LANES, g * GROUP_WIDTH + (pr + 1) * LANES)
            ssl = slice(pr * LANES, (pr + 1) * LANES)
            xs_p = act_ref[:, psl]
            hin = st_ref[0, g, :, ssl]
            intra, inter, bw = [], [], []
            for h in (g * HEADS_PER_GROUP + 2 * pr, g * HEADS_PER_GROUP + 2 * pr + 1):
                a_col = cum[:, h:h + 1]
                seg = a_col - cum_t[h:h + 1, :]
                dec = jnp.exp(jnp.where(tri, seg, neg_inf)).astype(BF16)
                intra.append(cb * dec * dt_t[h:h + 1, :].astype(BF16))
                ea = jnp.broadcast_to(jnp.exp(a_col), (CHUNK, D_STATE)).astype(BF16)
                inter.append(cm16 * ea)
                bw.append(bm_t * w_t[h:h + 1, :].astype(BF16))
            xs2 = split_heads(xs_p.astype(BF16))
            y_lhs = jnp.concatenate(intra + inter, axis=1)
            y_rhs = jnp.concatenate([xs2, split_heads(hin.astype(BF16))], axis=0)
            y_pair = jnp.dot(y_lhs, y_rhs, preferred_element_type=F32)
            y_ref[0, :, psl] = (y_pair + dskip_ref[:, psl] * xs_p).astype(y_ref.dtype)
            st_ref[0, g, :, ssl] = hin * cd_e[:, psl] + jnp.dot(
                jnp.concatenate(bw, axis=1), xs2, preferred_element_type=F32)


def _conv_shift_matrix():
    m = np.zeros(((CONV_K - 1) * CHUNK, TAIL_ROWS + CHUNK), np.float32)
    for k in range(CONV_K - 1):
        for t in range(CHUNK):
            m[k * CHUNK + t, TAIL_ROWS + t - (CONV_K - 1) + k] = 1.0
    return jnp.asarray(m, BF16)


def _ssd(p1, small, convw, convb, brow, alog_row, dskip_row, expand_mat, st0, tail0, c0,
         first_valid, name):
    b, s, _ = p1.shape
    nc = s // CHUNK
    const = lambda shape: pl.BlockSpec(shape, lambda bi, ci: (0,) * len(shape))
    in_specs = [
        pl.BlockSpec((1, CHUNK, 1024), lambda bi, ci: (bi, ci, 2)),
        pl.BlockSpec((1, CHUNK, 1024), lambda bi, ci: (bi, ci, 3)),
        pl.BlockSpec((1, CHUNK, 1024), lambda bi, ci: (bi, ci, 4)),
        pl.BlockSpec((1, CHUNK, LANES), lambda bi, ci: (bi, ci, 0)),
        const((CONV_K, CONV_DIM)), const((1, CONV_DIM)), const((1, LANES)), const((1, LANES)),
        const((1, D_SSD)), const((LANES, D_SSD)), const(((CONV_K - 1) * CHUNK, TAIL_ROWS + CHUNK)),
        const((3 * LANES, LANES)),
        const((SSD_GROUPS, D_STATE, GROUP_WIDTH)), const((TAIL_ROWS, CONV_DIM)), const((1, LANES)),
    ]
    out_shape = [
        jax.ShapeDtypeStruct((b, s, D_SSD), BF16),
        jax.ShapeDtypeStruct((b, s, LANES), BF16),
        jax.ShapeDtypeStruct((b, SSD_GROUPS, D_STATE, GROUP_WIDTH), F32),
        jax.ShapeDtypeStruct((b, TAIL_ROWS, CONV_DIM), BF16),
        jax.ShapeDtypeStruct((b, 1, LANES), F32),
    ]
    out_specs = [
        pl.BlockSpec((1, CHUNK, D_SSD), lambda bi, ci: (bi, ci, 0)),
        pl.BlockSpec((1, CHUNK, LANES), lambda bi, ci: (bi, ci, 0)),
        pl.BlockSpec((1, SSD_GROUPS, D_STATE, GROUP_WIDTH), lambda bi, ci: (bi, 0, 0, 0)),
        pl.BlockSpec((1, TAIL_ROWS, CONV_DIM), lambda bi, ci: (bi, 0, 0)),
        pl.BlockSpec((1, 1, LANES), lambda bi, ci: (bi, 0, 0)),
    ]
    return pl.pallas_call(
        functools.partial(_ssd_kernel, first_valid=first_valid),
        grid=(b, nc),
        in_specs=in_specs,
        out_specs=out_specs,
        out_shape=out_shape,
        scratch_shapes=[pltpu.VMEM((TAIL_ROWS + CHUNK, CONV_DIM), BF16),
                        pltpu.VMEM((CHUNK, CONV_DIM), F32)],
        compiler_params=pltpu.CompilerParams(
            dimension_semantics=("parallel", "arbitrary"),
            vmem_limit_bytes=VMEM_LIMIT),
        name=name,
    )(p1, p1, p1, small, convw, convb, brow, alog_row, dskip_row, expand_mat,
      _conv_shift_matrix(), _bias_placement_matrix(), st0, tail0, c0)


V_ROWS = 80
L_ROW = ATT_HEAD_DIM
STAT_ROWS = 8


def _placement_matrices():
    npairs = H_ATT // 2
    pq = np.zeros((npairs, 2, 2 * LANES, LANES), np.float32)
    pk = np.zeros((npairs, 2, 2 * LANES, LANES), np.float32)
    pv = np.zeros((2, V_ROWS, LANES), np.float32)
    for hp in range(npairs):
        for hh in range(2):
            own0 = hh * ATT_HEAD_DIM
            base = (1 - hh) * ATT_HEAD_DIM
            for l in range(own0, own0 + ATT_HEAD_DIM):
                pq[hp, hh, l, l] = 1.0
                pk[hp, hh, l, l] = 1.0
            for t in range(3):
                pq[hp, hh, LANES + 3 * (2 * hp + hh) + t, base + t] = 1.0
                pq[hp, hh, LANES + BIAS_ONE_LANE, base + 3 + t] = 1.0
                pk[hp, hh, LANES + BIAS_ONE_LANE, base + t] = 1.0
                pk[hp, hh, LANES + 3 * (2 * hp + hh) + t, base + 3 + t] = -1.0
    for hh in range(2):
        for r in range(ATT_HEAD_DIM):
            pv[hh, r, hh * ATT_HEAD_DIM + r] = 1.0
    return (jnp.asarray(pq.transpose(0, 1, 3, 2), BF16), jnp.asarray(pk, BF16),
            jnp.asarray(pv, BF16))


def _bias_placement_matrix():
    m = np.zeros((3 * LANES, LANES), np.float32)
    for h in range(H_ATT):
        for t in range(3):
            m[t * LANES + F_LANE0 + h, 3 * h + t] = 1.0
    return jnp.asarray(m, BF16)


def _pair_schedule(nq, unroll):
    assert nq % (2 * unroll) == 0
    slots = [[] for _ in range(unroll)]
    for i in range(nq // 2):
        slots[i % unroll] += [i, nq - 1 - i]
    full = [[(qb, li, kb) for li, qb in enumerate(qbs) for kb in range(qb + 1)] for qbs in slots]
    diag = [[(qb, li, qb + 1) for li, qb in enumerate(qbs)] for qbs in slots]
    g1, g2 = len(full[0]), len(diag[0])
    assert all(len(f) == g1 for f in full) and all(len(d) == g2 for d in diag)
    parts = []
    for sched in (full, diag):
        parts.append([qb for s in sched for (qb, _, _) in s])
        parts.append([li for s in sched for (_, li, _) in s])
        parts.append([kb for s in sched for (_, _, kb) in s])
    table = np.concatenate([np.asarray(p, np.int32) for p in parts])
    return slots, g1, g2, table


def _attn_kernel(tab_ref, q_ref, k_ref, v_ref, bias_ref, km_ref, vm_ref, bm_ref, pq_ref, pk_ref,
                 pv_ref, o_ref, kaug_scr, vsel_scr, qaug_scr, s_scr, p_scr, alpha_scr, m_scr,
                 acc_scr, *, slots, g1, g2):
    seq = q_ref.shape[1]
    blk = ATT_BLOCK
    unroll = len(slots)
    neg_inf = jnp.float32(-jnp.inf)
    vrow = lax.broadcasted_iota(jnp.int32, (V_ROWS, 1), 0)

    def keys_operand(kb, bias, hh):
        x = jnp.concatenate([kb, bias], axis=1)
        return jnp.dot(x, pk_ref[0, hh], preferred_element_type=F32).astype(BF16)

    def values_operand(vb, hh):
        vt = lax.dot_general(pv_ref[hh], vb, _NT, preferred_element_type=F32)
        return jnp.where(vrow == L_ROW, 1.0, vt).astype(BF16)

    meta_bias = bm_ref[...]
    pad_bias = jnp.broadcast_to(meta_bias[0:1, :], (CHUNK, LANES))
    zero_blk = jnp.zeros((CHUNK, LANES), BF16)
    for hh in range(2):
        kaug_scr[hh, 0:CHUNK, :] = keys_operand(km_ref[...], meta_bias, hh)
        kaug_scr[hh, CHUNK:blk, :] = keys_operand(zero_blk, pad_bias, hh)
        vsel_scr[hh, 0] = values_operand(jnp.concatenate([vm_ref[...], zero_blk], axis=0), hh)

    def prep_chunk(r, carry):
        rows = pl.ds(pl.multiple_of(r * blk, blk), blk)
        krows = pl.ds(pl.multiple_of((r + 1) * blk, blk), blk)
        bias = bias_ref[0, rows, :]
        xq = jnp.concatenate([q_ref[0, rows, :], bias], axis=1)
        kb = k_ref[0, rows, :]
        vb = v_ref[0, rows, :]
        for hh in range(2):
            qaug_scr[hh, r] = lax.dot_general(pq_ref[0, hh], xq, _NT,
                                              preferred_element_type=F32).astype(BF16)
            kaug_scr[hh, krows, :] = keys_operand(kb, bias, hh)
            vsel_scr[hh, r + 1] = values_operand(vb, hh)
        return carry

    lax.fori_loop(0, seq // blk, prep_chunk, 0, unroll=4)

    m_scr[...] = jnp.full(m_scr.shape, neg_inf, F32)
    acc_scr[...] = jnp.zeros(acc_scr.shape, F32)

    key_idx = lax.broadcasted_iota(jnp.int32, (blk, blk), 0)
    query_idx = lax.broadcasted_iota(jnp.int32, (blk, blk), 1)
    causal = key_idx <= query_idx
    stat_rows = m_scr.shape[-2]

    def run(base, ngroups, masked, s_scr, p_scr, alpha_scr):
        def entry(field, u, g):
            return tab_ref[base + (field * unroll + u) * ngroups + g]

        def stage_a(g):
            for u in range(unroll):
                qb = entry(0, u, g)
                krows = pl.ds(pl.multiple_of(entry(2, u, g) * blk, blk), blk)
                for hh in range(2):
                    s_scr[u, hh] = jnp.dot(kaug_scr[hh, krows, :], qaug_scr[hh, qb],
                                           preferred_element_type=F32)

        def stage_b(g):
            for u in range(unroll):
                li = entry(1, u, g)
                for hh in range(2):
                    s = s_scr[u, hh]
                    if masked:
                        s = jnp.where(causal, s, neg_inf)
                    m_prev = m_scr[u, hh, li][0:1]
                    m_next = jnp.maximum(m_prev, jnp.max(s, axis=0, keepdims=True))
                    alpha_scr[u, hh] = jnp.broadcast_to(jnp.exp2(m_prev - m_next),
                                                        (stat_rows, blk))
                    m_scr[u, hh, li] = jnp.broadcast_to(m_next, (stat_rows, blk))
                    p_scr[u, hh] = jnp.exp2(s - m_next).astype(BF16)

        def stage_c(g):
            for u in range(unroll):
                li = entry(1, u, g)
                kb = entry(2, u, g)
                for hh in range(2):
                    pv = jnp.dot(vsel_scr[hh, kb], p_scr[u, hh], preferred_element_type=F32)
                    acc_scr[u, hh, li] = acc_scr[u, hh, li] * alpha_scr[u, hh][0:1] + pv

        stage_a(0)
        stage_b(0)
        stage_a(min(1, ngroups - 1))

        def body(g, carry):
            stage_c(g - 1)
            stage_b(g)
            stage_a(jnp.minimum(g + 1, ngroups - 1))
            return carry

        lax.fori_loop(1, ngroups, body, 0)
        stage_c(ngroups - 1)

    run(0, g1, False, s_scr.at[0], p_scr.at[0], alpha_scr.at[0])
    run(3 * unroll * g1, g2, True, s_scr.at[1], p_scr.at[1], alpha_scr.at[1])

    for u, qbs in enumerate(slots):
        for li, qb in enumerate(qbs):
            outs = []
            for hh in range(2):
                a = acc_scr[u, hh, li]
                outs.append(a[0:ATT_HEAD_DIM] / a[L_ROW:L_ROW + 1])
            o_ref[0, qb * blk:(qb + 1) * blk, :] = jnp.concatenate(outs, axis=0).T.astype(o_ref.dtype)


def _attention(qkv, bias, qkv_meta, bias_meta, name):
    b, s, _ = qkv.shape
    npairs = H_ATT // 2
    blk = ATT_BLOCK
    slots, g1, g2, table = _pair_schedule(s // blk, ATT_UNROLL)
    nloc = len(slots[0])
    seq_blk = lambda col0: pl.BlockSpec((1, s, LANES), lambda bi, hp, tab: (bi, 0, col0 + hp))
    meta_blk = lambda col0: pl.BlockSpec((CHUNK, LANES), lambda bi, hp, tab: (0, col0 + hp))
    grid_spec = pltpu.PrefetchScalarGridSpec(
        num_scalar_prefetch=1,
        grid=(b, npairs),
        in_specs=[
            seq_blk(Q_BLOCK0), seq_blk(Q_BLOCK0 + npairs), seq_blk(Q_BLOCK0 + 2 * npairs),
            pl.BlockSpec((1, s, LANES), lambda bi, hp, tab: (bi, 0, 0)),
            meta_blk(Q_BLOCK0 + npairs), meta_blk(Q_BLOCK0 + 2 * npairs),
            pl.BlockSpec((CHUNK, LANES), lambda bi, hp, tab: (0, 0)),
            pl.BlockSpec((1, 2, LANES, 2 * LANES), lambda bi, hp, tab: (hp, 0, 0, 0)),
            pl.BlockSpec((1, 2, 2 * LANES, LANES), lambda bi, hp, tab: (hp, 0, 0, 0)),
            pl.BlockSpec((2, V_ROWS, LANES), lambda bi, hp, tab: (0, 0, 0)),
        ],
        out_specs=pl.BlockSpec((1, s, LANES), lambda bi, hp, tab: (bi, 0, hp)),
        scratch_shapes=[
            pltpu.VMEM((2, s + blk, LANES), BF16),
            pltpu.VMEM((2, s // blk + 1, V_ROWS, blk), BF16),
            pltpu.VMEM((2, s // blk, LANES, blk), BF16),
            pltpu.VMEM((2, ATT_UNROLL, 2, blk, blk), F32),
            pltpu.VMEM((2, ATT_UNROLL, 2, blk, blk), BF16),
            pltpu.VMEM((2, ATT_UNROLL, 2, STAT_ROWS, blk), F32),
            pltpu.VMEM((ATT_UNROLL, 2, nloc, STAT_ROWS, blk), F32),
            pltpu.VMEM((ATT_UNROLL, 2, nloc, V_ROWS, blk), F32),
        ])
    return pl.pallas_call(
        functools.partial(_attn_kernel, slots=slots, g1=g1, g2=g2),
        grid_spec=grid_spec,
        out_shape=jax.ShapeDtypeStruct((b, s, D_ATT), BF16),
        compiler_params=pltpu.CompilerParams(
            dimension_semantics=("parallel", "arbitrary"),
            vmem_limit_bytes=VMEM_LIMIT),
        name=name,
    )(jnp.asarray(table), qkv, qkv, qkv, bias, qkv_meta, qkv_meta, bias_meta,
      *_placement_matrices())


def _merge_kernel(y_ref, zs_ref, o_ref, za_ref, gs_ref, ga_ref, x_ref, wps_ref, wpa_ref,
                  wout_ref, snorm_ref, gbias_ref, npost_ref, out_ref):
    zs = zs_ref[...].astype(F32)
    u = y_ref[...].astype(F32) * (zs * jax.nn.sigmoid(zs))
    parts = []
    for g in range(SSD_GROUPS):
        ug = u[:, g * GROUP_WIDTH:(g + 1) * GROUP_WIDTH]
        ms = jnp.mean(ug * ug, axis=-1, keepdims=True)
        parts.append((ug * lax.rsqrt(ms + EPS)
                      * snorm_ref[:, g * GROUP_WIDTH:(g + 1) * GROUP_WIDTH]).astype(BF16))
    y_ssd = jnp.concatenate(parts, axis=1)
    t_ssd = jnp.dot(y_ssd, wps_ref[...], preferred_element_type=F32)
    za = za_ref[...].astype(F32)
    y_att = (o_ref[...].astype(F32) * (za * jax.nn.sigmoid(za))).astype(BF16)
    t_att = jnp.dot(y_att, wpa_ref[...], preferred_element_type=F32)
    g_ssd = jax.nn.sigmoid(gs_ref[...].astype(F32) + gbias_ref[:, 0:D_MODEL])
    g_att = jax.nn.sigmoid(ga_ref[...].astype(F32) + gbias_ref[:, D_MODEL:2 * D_MODEL])
    merged = (g_ssd * t_ssd + g_att * t_att).astype(BF16)
    t = jnp.dot(merged, wout_ref[...], preferred_element_type=F32)
    ms = jnp.mean(t * t, axis=-1, keepdims=True)
    out_ref[...] = x_ref[...] + t * lax.rsqrt(ms + EPS) * npost_ref[...]


def _merge(y2d, p1, o2d, x2d, wps, wpa, wout, snorm, gbias, npost, tm, name):
    m = x2d.shape[0]
    row_blk = lambda width, col: pl.BlockSpec((tm, width), lambda i: (i, col))
    const = lambda shape: pl.BlockSpec(shape, lambda i: (0, 0), pipeline_mode=pl.Buffered(1))
    in_specs = [
        row_blk(D_SSD, 0),
        row_blk(D_SSD, 0),
        row_blk(D_ATT, 0),
        row_blk(D_ATT, 5),
        row_blk(D_MODEL, 6),
        row_blk(D_MODEL, 7),
        row_blk(D_MODEL, 0),
        const((D_SSD, D_MODEL)), const((D_ATT, D_MODEL)), const((D_MODEL, D_MODEL)),
        const((1, D_SSD)), const((1, 2 * D_MODEL)), const((1, D_MODEL)),
    ]
    return pl.pallas_call(
        _merge_kernel,
        grid=(m // tm,),
        in_specs=in_specs,
        out_specs=pl.BlockSpec((tm, D_MODEL), lambda i: (i, 0)),
        out_shape=jax.ShapeDtypeStruct((m, D_MODEL), F32),
        compiler_params=pltpu.CompilerParams(
            dimension_semantics=("parallel",),
            vmem_limit_bytes=VMEM_LIMIT),
        name=name,
    )(y2d, p1, o2d, p1, p1, p1, x2d, wps, wpa, wout, snorm, gbias, npost)


def kernel(x, meta_tokens, norm_pre, w_in, conv_w, conv_b, dt_bias, a_log, d_skip, ssd_norm,
           fgate_bias, gate_bias, w_proj_ssd, w_proj_att, w_out, norm_post):
    bsz, seq, d = x.shape
    assert d == D_MODEL and seq % 1024 == 0 and norm_pre.shape[0] == 1

    w = jnp.swapaxes(w_in[0], 0, 1)
    o_xbc = D_SSD
    o_dt = o_xbc + CONV_DIM
    o_za = o_dt + H_SSD
    o_q = o_za + D_ATT
    o_k = o_q + D_ATT
    o_v = o_k + D_ATT
    o_f = o_v + D_ATT
    o_g = o_f + H_ATT
    att_scale = LOG2E / (ATT_HEAD_DIM ** 0.5)
    w_all = jnp.concatenate(
        [w[0:o_dt].astype(BF16), w[o_za:o_q].astype(BF16), w[o_g:].astype(BF16),
         (w[o_q:o_k] * att_scale).astype(BF16), w[o_k:o_f].astype(BF16)], axis=0)
    w_small = jnp.concatenate(
        [w[o_dt:o_za], w[o_f:o_g],
         jnp.zeros((LANES - H_SSD - H_ATT, d), F32)], axis=0).astype(BF16)
    g_pre = norm_pre[0][None, :]
    pad_lanes = jnp.zeros((LANES - H_SSD - H_ATT,), F32)
    brow = jnp.concatenate([dt_bias[0], fgate_bias[0], pad_lanes])[None, :]
    alog_row = jnp.concatenate([a_log[0], jnp.zeros((LANES - H_SSD,), F32)])[None, :]
    dskip_row = jnp.repeat(d_skip[0], SSD_HEAD_DIM)[None, :]
    head_of_lane = jnp.arange(D_SSD, dtype=jnp.int32) // SSD_HEAD_DIM
    expand_mat = (jnp.arange(LANES, dtype=jnp.int32)[:, None] == head_of_lane[None, :]).astype(BF16)
    convw = conv_w[0]
    convb = conv_b[0][None, :]
    wps = w_proj_ssd[0].astype(BF16)
    wpa = w_proj_att[0].astype(BF16)
    wout = w_out[0].astype(BF16)
    snorm = ssd_norm[0][None, :]
    gbias = gate_bias[0][None, :]
    npost = norm_post[0][None, :]

    meta_blk = jnp.concatenate([jnp.zeros((CHUNK - N_META, d), F32), meta_tokens.astype(F32)], axis=0)
    p_m, small_m = _in_proj(meta_blk, g_pre, w_all, w_small, BF16, CHUNK, P_TILE_N, "in_proj_meta")
    st_zero = jnp.zeros((SSD_GROUPS, D_STATE, GROUP_WIDTH), F32)
    tail_zero = jnp.zeros((TAIL_ROWS, CONV_DIM), BF16)
    c_zero = jnp.zeros((1, LANES), F32)
    _, bias_m, st_m, tail_m, cc_m = _ssd(
        p_m[None], small_m[None], convw, convb, brow, alog_row, dskip_row, expand_mat,
        st_zero, tail_zero, c_zero, CHUNK - N_META, "ssd_meta")

    x2d = x.reshape(bsz * seq, d)
    p, small = _in_proj(x2d, g_pre, w_all, w_small, BF16, 1024, P_TILE_N, "in_proj")
    p3d = p.reshape(bsz, seq, P_COLS)
    y, bias, _, _, _ = _ssd(
        p3d, small.reshape(bsz, seq, LANES), convw, convb, brow,
        alog_row, dskip_row, expand_mat, st_m[0], tail_m[0], cc_m[0], 0, "ssd")
    o = _attention(p3d, bias, p_m, bias_m[0], "fox_attention")
    out = _merge(y.reshape(bsz * seq, D_SSD), p, o.reshape(bsz * seq, D_ATT), x2d,
                 wps, wpa, wout, snorm, gbias, npost, 512, "merge_out")
    return out.reshape(bsz, seq, d)
```

```python
import functools

import numpy as np

import jax
import jax.numpy as jnp
from jax import lax
from jax.experimental import pallas as pl
from jax.experimental.pallas import tpu as pltpu

F32 = jnp.float32
BF16 = jnp.bfloat16

D_MODEL = 1024
N_META = 16
CHUNK = 128
LANES = 128
D_SSD = 2 * D_MODEL
SSD_HEAD_DIM = 64
H_SSD = D_SSD // SSD_HEAD_DIM
SSD_GROUPS = 4
HEADS_PER_GROUP = H_SSD // SSD_GROUPS
D_STATE = 128
GROUP_WIDTH = HEADS_PER_GROUP * SSD_HEAD_DIM
CONV_K = 4
CONV_DIM = D_SSD + 2 * SSD_GROUPS * D_STATE
H_ATT = 16
ATT_HEAD_DIM = 64
D_ATT = H_ATT * ATT_HEAD_DIM
EPS = 1e-6
LOG2E = 1.4426950408889634
TAIL_ROWS = 16
F_LANE0 = H_SSD
P_COLS = 11 * 1024
P_TILE_N = P_COLS // 4
Q_BLOCK0 = 8 * (1024 // LANES)
VMEM_LIMIT = 56 * 1024 * 1024
ATT_BLOCK = 256
ATT_UNROLL = 4
MASK_BIAS = -1.0e30
BIAS_ONE_LANE = 3 * H_ATT

_NT = (((1,), (1,)), ((), ()))


def _split3(v):
    hi = v.astype(BF16)
    r1 = v - hi.astype(F32)
    mid = r1.astype(BF16)
    lo = (r1 - mid.astype(F32)).astype(BF16)
    return hi, mid, lo


def _dot_exact_lhs(a_bf16, v):
    hi, mid, lo = _split3(v)
    d = functools.partial(jnp.dot, preferred_element_type=F32)
    return d(a_bf16, hi) + d(a_bf16, mid) + d(a_bf16, lo)


def _dot_exact_rhs(v, b_bf16):
    hi, mid, lo = _split3(v)
    d = functools.partial(jnp.dot, preferred_element_type=F32)
    return d(hi, b_bf16) + d(mid, b_bf16) + d(lo, b_bf16)


def _in_proj_kernel(x_ref, g_ref, w_ref, *rest, with_small):
    if with_small:
        ws_ref, o_ref, os_ref, u_ref = rest
    else:
        o_ref, u_ref = rest
    j = pl.program_id(1)

    @pl.when(j == 0)
    def _():
        x = x_ref[...]
        ms = jnp.mean(x * x, axis=-1, keepdims=True)
        u = (x * lax.rsqrt(ms + EPS) * g_ref[...]).astype(BF16)
        u_ref[...] = u
        if with_small:
            os_ref[...] = lax.dot_general(u, ws_ref[...], _NT, preferred_element_type=F32)

    o_ref[...] = lax.dot_general(u_ref[...], w_ref[...], _NT,
                                 preferred_element_type=F32).astype(o_ref.dtype)


def _in_proj(x2d, g_row, w_t, w_small_t, out_dtype, tm, tn, name):
    m, d = x2d.shape
    n = w_t.shape[0]
    with_small = w_small_t is not None
    in_specs = [pl.BlockSpec((tm, d), lambda i, j: (i, 0)),
                pl.BlockSpec((1, d), lambda i, j: (0, 0)),
                pl.BlockSpec((tn, d), lambda i, j: (j, 0))]
    out_shape = [jax.ShapeDtypeStruct((m, n), out_dtype)]
    out_specs = [pl.BlockSpec((tm, tn), lambda i, j: (i, j))]
    args = [x2d, g_row, w_t]
    w_small = w_small_t
    if with_small:
        in_specs.append(pl.BlockSpec((LANES, d), lambda i, j: (0, 0)))
        out_shape.append(jax.ShapeDtypeStruct((m, LANES), F32))
        out_specs.append(pl.BlockSpec((tm, LANES), lambda i, j: (i, 0)))
        args.append(w_small)
    return pl.pallas_call(
        functools.partial(_in_proj_kernel, with_small=with_small),
        grid=(m // tm, n // tn),
        in_specs=in_specs,
        out_specs=out_specs,
        out_shape=out_shape,
        scratch_shapes=[pltpu.VMEM((tm, d), BF16)],
        compiler_params=pltpu.CompilerParams(
            dimension_semantics=("parallel", "arbitrary"),
            vmem_limit_bytes=VMEM_LIMIT),
        name=name,
    )(*args)


def _ssd_kernel(x0_ref, x1_ref, bc_ref, s_ref, convw_ref, convb_ref, brow_ref, alog_ref,
                dskip_ref, e_ref, shift_ref, pb_ref, st0_ref, tail0_ref, c0_ref,
                y_ref, bias_ref, st_ref, tail_ref, cc_ref,
                ext_ref, act_ref, *, first_valid):
    ci = pl.program_id(1)

    @pl.when(ci == 0)
    def _():
        st_ref[0] = st0_ref[...]
        tail_ref[0] = tail0_ref[...]
        cc_ref[0] = c0_ref[...]

    ext_ref[0:TAIL_ROWS, :] = tail_ref[0]
    ext_ref[TAIL_ROWS:TAIL_ROWS + CHUNK, 0:1024] = x0_ref[0]
    ext_ref[TAIL_ROWS:TAIL_ROWS + CHUNK, 1024:2048] = x1_ref[0]
    ext_ref[TAIL_ROWS:TAIL_ROWS + CHUNK, 2048:3072] = bc_ref[0]
    tail_ref[0] = ext_ref[CHUNK:CHUNK + TAIL_ROWS, :]

    row = lax.broadcasted_iota(jnp.int32, (CHUNK, 1), 0)
    valid_row = row >= first_valid
    shift = shift_ref[...]
    for j in range(CONV_DIM // GROUP_WIDTH):
        sl = slice(j * GROUP_WIDTH, (j + 1) * GROUP_WIDTH)
        taps = jnp.dot(shift, ext_ref[:, sl], preferred_element_type=F32)
        acc = (convb_ref[:, sl] + convw_ref[CONV_K - 1:CONV_K, sl]
               * ext_ref[TAIL_ROWS:TAIL_ROWS + CHUNK, sl].astype(F32))
        for k in range(CONV_K - 1):
            acc = acc + convw_ref[k:k + 1, sl] * taps[k * CHUNK:(k + 1) * CHUNK]
        a = acc * jax.nn.sigmoid(acc)
        if first_valid:
            a = jnp.where(valid_row, a, 0.0)
        act_ref[:, sl] = a

    lane = lax.broadcasted_iota(jnp.int32, (1, LANES), 1)
    is_dt = lane < H_SSD
    is_f = (lane >= F_LANE0) & (lane < F_LANE0 + H_ATT)
    sv = s_ref[0] + brow_ref[...]
    z = jnp.where(is_dt, sv, -sv)
    sp = jnp.maximum(z, 0.0) + jnp.log(1.0 + jnp.exp(-jnp.abs(z)))
    a_row = -jnp.exp(alog_ref[...])
    vals = jnp.where(is_dt, sp * a_row, jnp.where(is_f, -sp, 0.0))
    dt = jnp.where(is_dt, sp, 0.0)
    if first_valid:
        vals = jnp.where(valid_row, vals, 0.0)
        dt = jnp.where(valid_row, dt, 0.0)

    ri = lax.broadcasted_iota(jnp.int32, (CHUNK, CHUNK), 0)
    cj = lax.broadcasted_iota(jnp.int32, (CHUNK, CHUNK), 1)
    tri = cj <= ri
    cum = _dot_exact_lhs(tri.astype(BF16), vals)
    cfull = cum + cc_ref[0]
    c_log2 = LOG2E * cfull
    if first_valid:
        c_log2 = jnp.where(valid_row, c_log2, -MASK_BIAS)
    terms = jnp.concatenate(_split3(c_log2), axis=1)
    placed = jnp.dot(terms, pb_ref[...], preferred_element_type=F32)
    bias_ref[0] = jnp.where(lane == BIAS_ONE_LANE, 1.0, placed).astype(BF16)
    cum_t = cfull.T
    cc_ref[0] = jnp.where(is_f, cfull[CHUNK - 1:CHUNK, :], 0.0)
    dt_t = dt.T

    a_last = cum[CHUNK - 1:CHUNK, :]
    w_t = (dt * jnp.exp(a_last - cum)).T
    cd = jnp.where(is_dt, jnp.exp(a_last), 0.0)
    cd_e = _dot_exact_rhs(jnp.broadcast_to(cd, (8, LANES)), e_ref[...])[0:1]

    lo_half = lane < SSD_HEAD_DIM
    neg_inf = jnp.float32(-jnp.inf)

    def split_heads(x16):
        zero16 = jnp.zeros_like(x16)
        return jnp.concatenate([jnp.where(lo_half, x16, zero16),
                                jnp.where(lo_half, zero16, x16)], axis=0)

    for g in range(SSD_GROUPS):
        bm = act_ref[:, D_SSD + g * D_STATE:D_SSD + (g + 1) * D_STATE]
        cm = act_ref[:, D_SSD + SSD_GROUPS * D_STATE + g * D_STATE:
                     D_SSD + SSD_GROUPS * D_STATE + (g + 1) * D_STATE]
        cm16 = cm.astype(BF16)
        cb = lax.dot_general(cm16, bm.astype(BF16), _NT,
                             preferred_element_type=F32).astype(BF16)
        bm_t = bm.T.astype(BF16)
        for pr in range(HEADS_PER_GROUP // 2):
            psl = slice(g * GROUP_WIDTH + pr * LANES, g * GROUP_WIDTH + (pr + 1) * LANES)
            ssl = slice(pr * LANES, (pr + 1) * LANES)
            xs_p = act_ref[:, psl]
            hin = st_ref[0, g, :, ssl]
            intra, inter, bw = [], [], []
            for h in (g * HEADS_PER_GROUP + 2 * pr, g * HEADS_PER_GROUP + 2 * pr + 1):
                a_col = cum[:, h:h + 1]
                seg = a_col - cum_t[h:h + 1, :]
                dec = jnp.exp(jnp.where(tri, seg, neg_inf)).astype(BF16)
                intra.append(cb * dec * dt_t[h:h + 1, :].astype(BF16))
                ea = jnp.broadcast_to(jnp.exp(a_col), (CHUNK, D_STATE)).astype(BF16)
                inter.append(cm16 * ea)
                bw.append(bm_t * w_t[h:h + 1, :].astype(BF16))
            xs2 = split_heads(xs_p.astype(BF16))
            y_lhs = jnp.concatenate(intra + inter, axis=1)
            y_rhs = jnp.concatenate([xs2, split_heads(hin.astype(BF16))], axis=0)
            y_pair = jnp.dot(y_lhs, y_rhs, preferred_element_type=F32)
            y_ref[0, :, psl] = (y_pair + dskip_ref[:, psl] * xs_p).astype(y_ref.dtype)
            st_ref[0, g, :, ssl] = hin * cd_e[:, psl] + jnp.dot(
                jnp.concatenate(bw, axis=1), xs2, preferred_element_type=F32)


def _conv_shift_matrix():
    m = np.zeros(((CONV_K - 1) * CHUNK, TAIL_ROWS + CHUNK), np.float32)
    for k in range(CONV_K - 1):
        for t in range(CHUNK):
            m[k * CHUNK + t, TAIL_ROWS + t - (CONV_K - 1) + k] = 1.0
    return jnp.asarray(m, BF16)


def _ssd(p1, small, convw, convb, brow, alog_row, dskip_row, expand_mat, st0, tail0, c0,
         first_valid, name):
    b, s, _ = p1.shape
    nc = s // CHUNK
    const = lambda shape: pl.BlockSpec(shape, lambda bi, ci: (0,) * len(shape))
    in_specs = [
        pl.BlockSpec((1, CHUNK, 1024), lambda bi, ci: (bi, ci, 2)),
        pl.BlockSpec((1, CHUNK, 1024), lambda bi, ci: (bi, ci, 3)),
        pl.BlockSpec((1, CHUNK, 1024), lambda bi, ci: (bi, ci, 4)),
        pl.BlockSpec((1, CHUNK, LANES), lambda bi, ci: (bi, ci, 0)),
        const((CONV_K, CONV_DIM)), const((1, CONV_DIM)), const((1, LANES)), const((1, LANES)),
        const((1, D_SSD)), const((LANES, D_SSD)), const(((CONV_K - 1) * CHUNK, TAIL_ROWS + CHUNK)),
        const((3 * LANES, LANES)),
        const((SSD_GROUPS, D_STATE, GROUP_WIDTH)), const((TAIL_ROWS, CONV_DIM)), const((1, LANES)),
    ]
    out_shape = [
        jax.ShapeDtypeStruct((b, s, D_SSD), BF16),
        jax.ShapeDtypeStruct((b, s, LANES), BF16),
        jax.ShapeDtypeStruct((b, SSD_GROUPS, D_STATE, GROUP_WIDTH), F32),
        jax.ShapeDtypeStruct((b, TAIL_ROWS, CONV_DIM), BF16),
        jax.ShapeDtypeStruct((b, 1, LANES), F32),
    ]
    out_specs = [
        pl.BlockSpec((1, CHUNK, D_SSD), lambda bi, ci: (bi, ci, 0)),
        pl.BlockSpec((1, CHUNK, LANES), lambda bi, ci: (bi, ci, 0)),
        pl.BlockSpec((1, SSD_GROUPS, D_STATE, GROUP_WIDTH), lambda bi, ci: (bi, 0, 0, 0)),
        pl.BlockSpec((1, TAIL_ROWS, CONV_DIM), lambda bi, ci: (bi, 0, 0)),
        pl.BlockSpec((1, 1, LANES), lambda bi, ci: (bi, 0, 0)),
    ]
    return pl.pallas_call(
        functools.partial(_ssd_kernel, first_valid=first_valid),
        grid=(b, nc),
        in_specs=in_specs,
        out_specs=out_specs,
        out_shape=out_shape,
        scratch_shapes=[pltpu.VMEM((TAIL_ROWS + CHUNK, CONV_DIM), BF16),
                        pltpu.VMEM((CHUNK, CONV_DIM), F32)],
        compiler_params=pltpu.CompilerParams(
            dimension_semantics=("parallel", "arbitrary"),
            vmem_limit_bytes=VMEM_LIMIT),
        name=name,
    )(p1, p1, p1, small, convw, convb, brow, alog_row, dskip_row, expand_mat,
      _conv_shift_matrix(), _bias_placement_matrix(), st0, tail0, c0)


V_ROWS = 80
L_ROW = ATT_HEAD_DIM
STAT_ROWS = 8


def _placement_matrices():
    npairs = H_ATT // 2
    pq = np.zeros((npairs, 2, 2 * LANES, LANES), np.float32)
    pk = np.zeros((npairs, 2, 2 * LANES, LANES), np.float32)
    pv = np.zeros((2, V_ROWS, LANES), np.float32)
    for hp in range(npairs):
        for hh in range(2):
            own0 = hh * ATT_HEAD_DIM
            base = (1 - hh) * ATT_HEAD_DIM
            for l in range(own0, own0 + ATT_HEAD_DIM):
                pq[hp, hh, l, l] = 1.0
                pk[hp, hh, l, l] = 1.0
            for t in range(3):
                pq[hp, hh, LANES + 3 * (2 * hp + hh) + t, base + t] = 1.0
                pq[hp, hh, LANES + BIAS_ONE_LANE, base + 3 + t] = 1.0
                pk[hp, hh, LANES + BIAS_ONE_LANE, base + t] = 1.0
                pk[hp, hh, LANES + 3 * (2 * hp + hh) + t, base + 3 + t] = -1.0
    for hh in range(2):
        for r in range(ATT_HEAD_DIM):
            pv[hh, r, hh * ATT_HEAD_DIM + r] = 1.0
    return (jnp.asarray(pq.transpose(0, 1, 3, 2), BF16), jnp.asarray(pk, BF16),
            jnp.asarray(pv, BF16))


def _bias_placement_matrix():
    m = np.zeros((3 * LANES, LANES), np.float32)
    for h in range(H_ATT):
        for t in range(3):
            m[t * LANES + F_LANE0 + h, 3 * h + t] = 1.0
    return jnp.asarray(m, BF16)


def _pair_schedule(nq, unroll):
    assert nq % (2 * unroll) == 0
    slots = [[] for _ in range(unroll)]
    for i in range(nq // 2):
        slots[i % unroll] += [i, nq - 1 - i]
    full = [[(qb, li, kb) for li, qb in enumerate(qbs) for kb in range(1, qb + 1)]
            for qbs in slots]
    diag = [[(qb, li, qb + 1) for li, qb in enumerate(qbs)] for qbs in slots]
    g1, g2 = len(full[0]), len(diag[0])
    assert all(len(f) == g1 for f in full) and all(len(d) == g2 for d in diag)
    parts = []
    for sched in (full, diag):
        parts.append([qb for s in sched for (qb, _, _) in s])
        parts.append([li for s in sched for (_, li, _) in s])
        parts.append([kb for s in sched for (_, _, kb) in s])
    table = np.concatenate([np.asarray(p, np.int32) for p in parts])
    return slots, g1, g2, table


def _attn_kernel(tab_ref, q_ref, k_ref, v_ref, bias_ref, km_ref, vm_ref, bm_ref, pq_ref, pk_ref,
                 pv_ref, o_ref, kaug_scr, vsel_scr, qaug_scr, s_scr, p_scr, alpha_scr, m_scr,
                 acc_scr, *, slots, g1, g2):
    seq = q_ref.shape[1]
    blk = ATT_BLOCK
    unroll = len(slots)
    neg_inf = jnp.float32(-jnp.inf)
    vrow = lax.broadcasted_iota(jnp.int32, (V_ROWS, 1), 0)

    def keys_operand(kb, bias, hh):
        x = jnp.concatenate([kb, bias], axis=1)
        return jnp.dot(x, pk_ref[0, hh], preferred_element_type=F32).astype(BF16)

    def values_operand(vb, hh):
        vt = lax.dot_general(pv_ref[hh], vb, _NT, preferred_element_type=F32)
        return jnp.where(vrow == L_ROW, 1.0, vt).astype(BF16)

    def prep_chunk(r, carry):
        rows = pl.ds(pl.multiple_of(r * blk, blk), blk)
        krows = pl.ds(pl.multiple_of((r + 1) * blk, blk), blk)
        bias = bias_ref[0, rows, :]
        xq = jnp.concatenate([q_ref[0, rows, :], bias], axis=1)
        kb = k_ref[0, rows, :]
        vb = v_ref[0, rows, :]
        for hh in range(2):
            qaug_scr[hh, r] = lax.dot_general(pq_ref[0, hh], xq, _NT,
                                              preferred_element_type=F32).astype(BF16)
            kaug_scr[hh, krows, :] = keys_operand(kb, bias, hh)
            vsel_scr[hh, r + 1] = values_operand(vb, hh)
        return carry

    lax.fori_loop(0, seq // blk, prep_chunk, 0, unroll=4)

    meta_rows = slice(CHUNK - N_META, CHUNK)
    where = {qb: (u, li) for u, qbs in enumerate(slots) for li, qb in enumerate(qbs)}
    group = 4
    for hh in range(2):
        k_meta = keys_operand(km_ref[meta_rows, :], bm_ref[meta_rows, :], hh)
        v_meta = values_operand(vm_ref[meta_rows, :], hh)
        for q0 in range(0, seq // blk, group):
            q_t = jnp.concatenate([qaug_scr[hh, qb] for qb in range(q0, q0 + group)], axis=1)
            s = jnp.dot(k_meta, q_t, preferred_element_type=F32)
            m0 = jnp.max(s, axis=0, keepdims=True)
            acc0 = jnp.dot(v_meta, jnp.exp2(s - m0).astype(BF16), preferred_element_type=F32)
            for i in range(group):
                u, li = where[q0 + i]
                cols = slice(i * blk, (i + 1) * blk)
                m_scr[u, hh, li] = jnp.broadcast_to(m0[:, cols], m_scr.shape[-2:])
                acc_scr[u, hh, li] = acc0[:, cols]

    key_idx = lax.broadcasted_iota(jnp.int32, (blk, blk), 0)
    query_idx = lax.broadcasted_iota(jnp.int32, (blk, blk), 1)
    causal = key_idx <= query_idx
    stat_rows = m_scr.shape[-2]

    def run(base, ngroups, masked, s_scr, p_scr, alpha_scr):
        def entry(field, u, g):
            return tab_ref[base + (field * unroll + u) * ngroups + g]

        def stage_a(g):
            for u in range(unroll):
                qb = entry(0, u, g)
                krows = pl.ds(pl.multiple_of(entry(2, u, g) * blk, blk), blk)
                for hh in range(2):
                    s_scr[u, hh] = jnp.dot(kaug_scr[hh, krows, :], qaug_scr[hh, qb],
                                           preferred_element_type=F32)

        def stage_b(g):
            for u in range(unroll):
                li = entry(1, u, g)
                for hh in range(2):
                    s = s_scr[u, hh]
                    if masked:
                        s = jnp.where(causal, s, neg_inf)
                    m_prev = m_scr[u, hh, li][0:1]
                    m_next = jnp.maximum(m_prev, jnp.max(s, axis=0, keepdims=True))
                    alpha_scr[u, hh] = jnp.broadcast_to(jnp.exp2(m_prev - m_next),
                                                        (stat_rows, blk))
                    m_scr[u, hh, li] = jnp.broadcast_to(m_next, (stat_rows, blk))
                    p_scr[u, hh] = jnp.exp2(s - m_next).astype(BF16)

        def stage_c(g):
            for u in range(unroll):
                li = entry(1, u, g)
                kb = entry(2, u, g)
                for hh in range(2):
                    pv = jnp.dot(vsel_scr[hh, kb], p_scr[u, hh], preferred_element_type=F32)
                    acc_scr[u, hh, li] = acc_scr[u, hh, li] * alpha_scr[u, hh][0:1] + pv

        stage_a(0)
        stage_b(0)
        stage_a(min(1, ngroups - 1))

        def body(g, carry):
            stage_c(g - 1)
            stage_b(g)
            stage_a(jnp.minimum(g + 1, ngroups - 1))
            return carry

        lax.fori_loop(1, ngroups, body, 0)
        stage_c(ngroups - 1)

    run(0, g1, False, s_scr.at[0], p_scr.at[0], alpha_scr.at[0])
    run(3 * unroll * g1, g2, True, s_scr.at[1], p_scr.at[1], alpha_scr.at[1])

    for u, qbs in enumerate(slots):
        for li, qb in enumerate(qbs):
            outs = []
            for hh in range(2):
                a = acc_scr[u, hh, li]
                outs.append(a[0:ATT_HEAD_DIM] / a[L_ROW:L_ROW + 1])
            o_ref[0, qb * blk:(qb + 1) * blk, :] = jnp.concatenate(outs, axis=0).T.astype(o_ref.dtype)


def _attention(qkv, bias, qkv_meta, bias_meta, name):
    b, s, _ = qkv.shape
    npairs = H_ATT // 2
    blk = ATT_BLOCK
    slots, g1, g2, table = _pair_schedule(s // blk, ATT_UNROLL)
    nloc = len(slots[0])
    seq_blk = lambda col0: pl.BlockSpec((1, s, LANES), lambda bi, hp, tab: (bi, 0, col0 + hp))
    meta_blk = lambda col0: pl.BlockSpec((CHUNK, LANES), lambda bi, hp, tab: (0, col0 + hp))
    grid_spec = pltpu.PrefetchScalarGridSpec(
        num_scalar_prefetch=1,
        grid=(b, npairs),
        in_specs=[
            seq_blk(Q_BLOCK0), seq_blk(Q_BLOCK0 + npairs), seq_blk(Q_BLOCK0 + 2 * npairs),
            pl.BlockSpec((1, s, LANES), lambda bi, hp, tab: (bi, 0, 0)),
            meta_blk(Q_BLOCK0 + npairs), meta_blk(Q_BLOCK0 + 2 * npairs),
            pl.BlockSpec((CHUNK, LANES), lambda bi, hp, tab: (0, 0)),
            pl.BlockSpec((1, 2, LANES, 2 * LANES), lambda bi, hp, tab: (hp, 0, 0, 0)),
            pl.BlockSpec((1, 2, 2 * LANES, LANES), lambda bi, hp, tab: (hp, 0, 0, 0)),
            pl.BlockSpec((2, V_ROWS, LANES), lambda bi, hp, tab: (0, 0, 0)),
        ],
        out_specs=pl.BlockSpec((1, s, LANES), lambda bi, hp, tab: (bi, 0, hp)),
        scratch_shapes=[
            pltpu.VMEM((2, s + blk, LANES), BF16),
            pltpu.VMEM((2, s // blk + 1, V_ROWS, blk), BF16),
            pltpu.VMEM((2, s // blk, LANES, blk), BF16),
            pltpu.VMEM((2, ATT_UNROLL, 2, blk, blk), F32),
            pltpu.VMEM((2, ATT_UNROLL, 2, blk, blk), BF16),
            pltpu.VMEM((2, ATT_UNROLL, 2, STAT_ROWS, blk), F32),
            pltpu.VMEM((ATT_UNROLL, 2, nloc, STAT_ROWS, blk), F32),
            pltpu.VMEM((ATT_UNROLL, 2, nloc, V_ROWS, blk), F32),
        ])
    return pl.pallas_call(
        functools.partial(_attn_kernel, slots=slots, g1=g1, g2=g2),
        grid_spec=grid_spec,
        out_shape=jax.ShapeDtypeStruct((b, s, D_ATT), BF16),
        compiler_params=pltpu.CompilerParams(
            dimension_semantics=("parallel", "arbitrary"),
            vmem_limit_bytes=VMEM_LIMIT),
        name=name,
    )(jnp.asarray(table), qkv, qkv, qkv, bias, qkv_meta, qkv_meta, bias_meta,
      *_placement_matrices())


def _merge_kernel(y_ref, zs_ref, o_ref, za_ref, gs_ref, ga_ref, x_ref, wps_ref, wpa_ref,
                  wout_ref, snorm_ref, gbias_ref, npost_ref, out_ref):
    zs = zs_ref[...].astype(F32)
    u = y_ref[...].astype(F32) * (zs * jax.nn.sigmoid(zs))
    parts = []
    for g in range(SSD_GROUPS):
        ug = u[:, g * GROUP_WIDTH:(g + 1) * GROUP_WIDTH]
        ms = jnp.mean(ug * ug, axis=-1, keepdims=True)
        parts.append((ug * lax.rsqrt(ms + EPS)
                      * snorm_ref[:, g * GROUP_WIDTH:(g + 1) * GROUP_WIDTH]).astype(BF16))
    y_ssd = jnp.concatenate(parts, axis=1)
    t_ssd = jnp.dot(y_ssd, wps_ref[...], preferred_element_type=F32)
    za = za_ref[...].astype(F32)
    y_att = (o_ref[...].astype(F32) * (za * jax.nn.sigmoid(za))).astype(BF16)
    t_att = jnp.dot(y_att, wpa_ref[...], preferred_element_type=F32)
    g_ssd = jax.nn.sigmoid(gs_ref[...].astype(F32) + gbias_ref[:, 0:D_MODEL])
    g_att = jax.nn.sigmoid(ga_ref[...].astype(F32) + gbias_ref[:, D_MODEL:2 * D_MODEL])
    merged = (g_ssd * t_ssd + g_att * t_att).astype(BF16)
    t = jnp.dot(merged, wout_ref[...], preferred_element_type=F32)
    ms = jnp.mean(t * t, axis=-1, keepdims=True)
    out_ref[...] = x_ref[...] + t * lax.rsqrt(ms + EPS) * npost_ref[...]


def _merge(y2d, p1, o2d, x2d, wps, wpa, wout, snorm, gbias, npost, tm, name):
    m = x2d.shape[0]
    row_blk = lambda width, col: pl.BlockSpec((tm, width), lambda i: (i, col))
    const = lambda shape: pl.BlockSpec(shape, lambda i: (0, 0), pipeline_mode=pl.Buffered(1))
    in_specs = [
        row_blk(D_SSD, 0),
        row_blk(D_SSD, 0),
        row_blk(D_ATT, 0),
        row_blk(D_ATT, 5),
        row_blk(D_MODEL, 6),
        row_blk(D_MODEL, 7),
        row_blk(D_MODEL, 0),
        const((D_SSD, D_MODEL)), const((D_ATT, D_MODEL)), const((D_MODEL, D_MODEL)),
        const((1, D_SSD)), const((1, 2 * D_MODEL)), const((1, D_MODEL)),
    ]
    return pl.pallas_call(
        _merge_kernel,
        grid=(m // tm,),
        in_specs=in_specs,
        out_specs=pl.BlockSpec((tm, D_MODEL), lambda i: (i, 0)),
        out_shape=jax.ShapeDtypeStruct((m, D_MODEL), F32),
        compiler_params=pltpu.CompilerParams(
            dimension_semantics=("parallel",),
            vmem_limit_bytes=VMEM_LIMIT),
        name=name,
    )(y2d, p1, o2d, p1, p1, p1, x2d, wps, wpa, wout, snorm, gbias, npost)


def kernel(x, meta_tokens, norm_pre, w_in, conv_w, conv_b, dt_bias, a_log, d_skip, ssd_norm,
           fgate_bias, gate_bias, w_proj_ssd, w_proj_att, w_out, norm_post):
    bsz, seq, d = x.shape
    assert d == D_MODEL and seq % 1024 == 0 and norm_pre.shape[0] == 1

    w = jnp.swapaxes(w_in[0], 0, 1)
    o_xbc = D_SSD
    o_dt = o_xbc + CONV_DIM
    o_za = o_dt + H_SSD
    o_q = o_za + D_ATT
    o_k = o_q + D_ATT
    o_v = o_k + D_ATT
    o_f = o_v + D_ATT
    o_g = o_f + H_ATT
    att_scale = LOG2E / (ATT_HEAD_DIM ** 0.5)
    w_all = jnp.concatenate(
        [w[0:o_dt].astype(BF16), w[o_za:o_q].astype(BF16), w[o_g:].astype(BF16),
         (w[o_q:o_k] * att_scale).astype(BF16), w[o_k:o_f].astype(BF16)], axis=0)
    w_small = jnp.concatenate(
        [w[o_dt:o_za], w[o_f:o_g],
         jnp.zeros((LANES - H_SSD - H_ATT, d), F32)], axis=0).astype(BF16)
    g_pre = norm_pre[0][None, :]
    pad_lanes = jnp.zeros((LANES - H_SSD - H_ATT,), F32)
    brow = jnp.concatenate([dt_bias[0], fgate_bias[0], pad_lanes])[None, :]
    alog_row = jnp.concatenate([a_log[0], jnp.zeros((LANES - H_SSD,), F32)])[None, :]
    dskip_row = jnp.repeat(d_skip[0], SSD_HEAD_DIM)[None, :]
    head_of_lane = jnp.arange(D_SSD, dtype=jnp.int32) // SSD_HEAD_DIM
    expand_mat = (jnp.arange(LANES, dtype=jnp.int32)[:, None] == head_of_lane[None, :]).astype(BF16)
    convw = conv_w[0]
    convb = conv_b[0][None, :]
    wps = w_proj_ssd[0].astype(BF16)
    wpa = w_proj_att[0].astype(BF16)
    wout = w_out[0].astype(BF16)
    snorm = ssd_norm[0][None, :]
    gbias = gate_bias[0][None, :]
    npost = norm_post[0][None, :]

    meta_blk = jnp.concatenate([jnp.zeros((CHUNK - N_META, d), F32), meta_tokens.astype(F32)], axis=0)
    p_m, small_m = _in_proj(meta_blk, g_pre, w_all, w_small, BF16, CHUNK, P_TILE_N, "in_proj_meta")
    st_zero = jnp.zeros((SSD_GROUPS, D_STATE, GROUP_WIDTH), F32)
    tail_zero = jnp.zeros((TAIL_ROWS, CONV_DIM), BF16)
    c_zero = jnp.zeros((1, LANES), F32)
    _, bias_m, st_m, tail_m, cc_m = _ssd(
        p_m[None], small_m[None], convw, convb, brow, alog_row, dskip_row, expand_mat,
        st_zero, tail_zero, c_zero, CHUNK - N_META, "ssd_meta")

    x2d = x.reshape(bsz * seq, d)
    p, small = _in_proj(x2d, g_pre, w_all, w_small, BF16, 1024, P_TILE_N, "in_proj")
    p3d = p.reshape(bsz, seq, P_COLS)
    y, bias, _, _, _ = _ssd(
        p3d, small.reshape(bsz, seq, LANES), convw, convb, brow,
        alog_row, dskip_row, expand_mat, st_m[0], tail_m[0], cc_m[0], 0, "ssd")
    o = _attention(p3d, bias, p_m, bias_m[0], "fox_attention")
    out = _merge(y.reshape(bsz * seq, D_SSD), p, o.reshape(bsz * seq, D_ATT), x2d,
                 wps, wpa, wout, snorm, gbias, npost, 512, "merge_out")
    return out.reshape(bsz, seq, d)
```

```python
import functools

import numpy as np

import jax
import jax.numpy as jnp
from jax import lax
from jax.experimental import pallas as pl
from jax.experimental.pallas import tpu as pltpu

F32 = jnp.float32
BF16 = jnp.bfloat16

D_MODEL = 1024
N_META = 16
CHUNK = 128
LANES = 128
D_SSD = 2 * D_MODEL
SSD_HEAD_DIM = 64
H_SSD = D_SSD // SSD_HEAD_DIM
SSD_GROUPS = 4
HEADS_PER_GROUP = H_SSD // SSD_GROUPS
D_STATE = 128
GROUP_WIDTH = HEADS_PER_GROUP * SSD_HEAD_DIM
CONV_K = 4
CONV_DIM = D_SSD + 2 * SSD_GROUPS * D_STATE
H_ATT = 16
ATT_HEAD_DIM = 64
D_ATT = H_ATT * ATT_HEAD_DIM
EPS = 1e-6
LOG2E = 1.4426950408889634
TAIL_ROWS = 16
F_LANE0 = H_SSD
P_COLS = 11 * 1024
P_TILE_N = P_COLS // 4
Q_BLOCK0 = 8 * (1024 // LANES)
VMEM_LIMIT = 56 * 1024 * 1024
ATT_BLOCK = 256
ATT_UNROLL = 4
MASK_BIAS = -1.0e30
BIAS_ONE_LANE = 3 * H_ATT

_NT = (((1,), (1,)), ((), ()))


def _split3(v):
    hi = v.astype(BF16)
    r1 = v - hi.astype(F32)
    mid = r1.astype(BF16)
    lo = (r1 - mid.astype(F32)).astype(BF16)
    return hi, mid, lo


def _dot_exact_lhs(a_bf16, v):
    hi, mid, lo = _split3(v)
    d = functools.partial(jnp.dot, preferred_element_type=F32)
    return d(a_bf16, hi) + d(a_bf16, mid) + d(a_bf16, lo)


def _dot_exact_rhs(v, b_bf16):
    hi, mid, lo = _split3(v)
    d = functools.partial(jnp.dot, preferred_element_type=F32)
    return d(hi, b_bf16) + d(mid, b_bf16) + d(lo, b_bf16)


def _in_proj_kernel(x_ref, g_ref, w_ref, *rest, with_small):
    if with_small:
        ws_ref, o_ref, os_ref, u_ref = rest
    else:
        o_ref, u_ref = rest
    j = pl.program_id(1)

    @pl.when(j == 0)
    def _():
        x = x_ref[...]
        ms = jnp.mean(x * x, axis=-1, keepdims=True)
        u = (x * lax.rsqrt(ms + EPS) * g_ref[...]).astype(BF16)
        u_ref[...] = u
        if with_small:
            os_ref[...] = lax.dot_general(u, ws_ref[...], _NT, preferred_element_type=F32)

    o_ref[...] = lax.dot_general(u_ref[...], w_ref[...], _NT,
                                 preferred_element_type=F32).astype(o_ref.dtype)


def _in_proj(x2d, g_row, w_t, w_small_t, out_dtype, tm, tn, name):
    m, d = x2d.shape
    n = w_t.shape[0]
    with_small = w_small_t is not None
    in_specs = [pl.BlockSpec((tm, d), lambda i, j: (i, 0)),
                pl.BlockSpec((1, d), lambda i, j: (0, 0)),
                pl.BlockSpec((tn, d), lambda i, j: (j, 0))]
    out_shape = [jax.ShapeDtypeStruct((m, n), out_dtype)]
    out_specs = [pl.BlockSpec((tm, tn), lambda i, j: (i, j))]
    args = [x2d, g_row, w_t]
    w_small = w_small_t
    if with_small:
        in_specs.append(pl.BlockSpec((LANES, d), lambda i, j: (0, 0)))
        out_shape.append(jax.ShapeDtypeStruct((m, LANES), F32))
        out_specs.append(pl.BlockSpec((tm, LANES), lambda i, j: (i, 0)))
        args.append(w_small)
    return pl.pallas_call(
        functools.partial(_in_proj_kernel, with_small=with_small),
        grid=(m // tm, n // tn),
        in_specs=in_specs,
        out_specs=out_specs,
        out_shape=out_shape,
        scratch_shapes=[pltpu.VMEM((tm, d), BF16)],
        compiler_params=pltpu.CompilerParams(
            dimension_semantics=("parallel", "arbitrary"),
            vmem_limit_bytes=VMEM_LIMIT),
        name=name,
    )(*args)


def _ssd_kernel(x0_ref, x1_ref, bc_ref, s_ref, convw_ref, convb_ref, brow_ref, alog_ref,
                dskip_ref, e_ref, shift_ref, pb_ref, st0_ref, tail0_ref, c0_ref,
                y_ref, bias_ref, st_ref, tail_ref, cc_ref,
                ext_ref, act_ref, *, first_valid):
    ci = pl.program_id(1)

    @pl.when(ci == 0)
    def _():
        st_ref[0] = st0_ref[...]
        tail_ref[0] = tail0_ref[...]
        cc_ref[0] = c0_ref[...]

    ext_ref[0:TAIL_ROWS, :] = tail_ref[0]
    ext_ref[TAIL_ROWS:TAIL_ROWS + CHUNK, 0:1024] = x0_ref[0]
    ext_ref[TAIL_ROWS:TAIL_ROWS + CHUNK, 1024:2048] = x1_ref[0]
    ext_ref[TAIL_ROWS:TAIL_ROWS + CHUNK, 2048:3072] = bc_ref[0]
    tail_ref[0] = ext_ref[CHUNK:CHUNK + TAIL_ROWS, :]

    row = lax.broadcasted_iota(jnp.int32, (CHUNK, 1), 0)
    valid_row = row >= first_valid
    shift = shift_ref[...]
    for j in range(CONV_DIM // GROUP_WIDTH):
        sl = slice(j * GROUP_WIDTH, (j + 1) * GROUP_WIDTH)
        taps = jnp.dot(shift, ext_ref[:, sl], preferred_element_type=F32)
        acc = (convb_ref[:, sl] + convw_ref[CONV_K - 1:CONV_K, sl]
               * ext_ref[TAIL_ROWS:TAIL_ROWS + CHUNK, sl].astype(F32))
        for k in range(CONV_K - 1):
            acc = acc + convw_ref[k:k + 1, sl] * taps[k * CHUNK:(k + 1) * CHUNK]
        a = acc * jax.nn.sigmoid(acc)
        if first_valid:
            a = jnp.where(valid_row, a, 0.0)
        act_ref[:, sl] = a

    lane = lax.broadcasted_iota(jnp.int32, (1, LANES), 1)
    is_dt = lane < H_SSD
    is_f = (lane >= F_LANE0) & (lane < F_LANE0 + H_ATT)
    sv = s_ref[0] + brow_ref[...]
    z = jnp.where(is_dt, sv, -sv)
    sp = jnp.maximum(z, 0.0) + jnp.log(1.0 + jnp.exp(-jnp.abs(z)))
    a_row = -jnp.exp(alog_ref[...])
    vals = jnp.where(is_dt, sp * a_row, jnp.where(is_f, -sp, 0.0))
    dt = jnp.where(is_dt, sp, 0.0)
    if first_valid:
        vals = jnp.where(valid_row, vals, 0.0)
        dt = jnp.where(valid_row, dt, 0.0)

    ri = lax.broadcasted_iota(jnp.int32, (CHUNK, CHUNK), 0)
    cj = lax.broadcasted_iota(jnp.int32, (CHUNK, CHUNK), 1)
    tri = cj <= ri
    cum = _dot_exact_lhs(tri.astype(BF16), vals)
    cfull = cum + cc_ref[0]
    c_log2 = LOG2E * cfull
    if first_valid:
        c_log2 = jnp.where(valid_row, c_log2, -MASK_BIAS)
    terms = jnp.concatenate(_split3(c_log2), axis=1)
    placed = jnp.dot(terms, pb_ref[...], preferred_element_type=F32)
    bias_ref[0] = jnp.where(lane == BIAS_ONE_LANE, 1.0, placed).astype(BF16)
    cum_t = cfull.T
    cc_ref[0] = jnp.where(is_f, cfull[CHUNK - 1:CHUNK, :], 0.0)
    dt_t = dt.T

    a_last = cum[CHUNK - 1:CHUNK, :]
    w_t = (dt * jnp.exp(a_last - cum)).T
    cd = jnp.where(is_dt, jnp.exp(a_last), 0.0)
    cd_e = _dot_exact_rhs(jnp.broadcast_to(cd, (8, LANES)), e_ref[...])[0:1]

    lo_half = lane < SSD_HEAD_DIM
    neg_inf = jnp.float32(-jnp.inf)

    def split_heads(x16):
        zero16 = jnp.zeros_like(x16)
        return jnp.concatenate([jnp.where(lo_half, x16, zero16),
                                jnp.where(lo_half, zero16, x16)], axis=0)

    for g in range(SSD_GROUPS):
        bm = act_ref[:, D_SSD + g * D_STATE:D_SSD + (g + 1) * D_STATE]
        cm = act_ref[:, D_SSD + SSD_GROUPS * D_STATE + g * D_STATE:
                     D_SSD + SSD_GROUPS * D_STATE + (g + 1) * D_STATE]
        cm16 = cm.astype(BF16)
        cb = lax.dot_general(cm16, bm.astype(BF16), _NT,
                             preferred_element_type=F32).astype(BF16)
        bm_t = bm.T.astype(BF16)
        for pr in range(HEADS_PER_GROUP // 2):
            psl = slice(g * GROUP_WIDTH + pr * LANES, g * GROUP_WIDTH + (pr + 1) * LANES)
            ssl = slice(pr * LANES, (pr + 1) * LANES)
            xs_p = act_ref[:, psl]
            hin = st_ref[0, g, :, ssl]
            intra, inter, bw = [], [], []
            for h in (g * HEADS_PER_GROUP + 2 * pr, g * HEADS_PER_GROUP + 2 * pr + 1):
                a_col = cum[:, h:h + 1]
                seg = a_col - cum_t[h:h + 1, :]
                dec = jnp.exp(jnp.where(tri, seg, neg_inf)).astype(BF16)
                intra.append(cb * dec * dt_t[h:h + 1, :].astype(BF16))
                ea = jnp.broadcast_to(jnp.exp(a_col), (CHUNK, D_STATE)).astype(BF16)
                inter.append(cm16 * ea)
                bw.append(bm_t * w_t[h:h + 1, :].astype(BF16))
            xs2 = split_heads(xs_p.astype(BF16))
            y_lhs = jnp.concatenate(intra + inter, axis=1)
            y_rhs = jnp.concatenate([xs2, split_heads(hin.astype(BF16))], axis=0)
            y_pair = jnp.dot(y_lhs, y_rhs, preferred_element_type=F32)
            y_ref[0, :, psl] = (y_pair + dskip_ref[:, psl] * xs_p).astype(y_ref.dtype)
            st_ref[0, g, :, ssl] = hin * cd_e[:, psl] + jnp.dot(
                jnp.concatenate(bw, axis=1), xs2, preferred_element_type=F32)


def _conv_shift_matrix():
    m = np.zeros(((CONV_K - 1) * CHUNK, TAIL_ROWS + CHUNK), np.float32)
    for k in range(CONV_K - 1):
        for t in range(CHUNK):
            m[k * CHUNK + t, TAIL_ROWS + t - (CONV_K - 1) + k] = 1.0
    return jnp.asarray(m, BF16)


def _ssd(p1, small, convw, convb, brow, alog_row, dskip_row, expand_mat, st0, tail0, c0,
         first_valid, name):
    b, s, _ = p1.shape
    nc = s // CHUNK
    const = lambda shape: pl.BlockSpec(shape, lambda bi, ci: (0,) * len(shape))
    in_specs = [
        pl.BlockSpec((1, CHUNK, 1024), lambda bi, ci: (bi, ci, 2)),
        pl.BlockSpec((1, CHUNK, 1024), lambda bi, ci: (bi, ci, 3)),
        pl.BlockSpec((1, CHUNK, 1024), lambda bi, ci: (bi, ci, 4)),
        pl.BlockSpec((1, CHUNK, LANES), lambda bi, ci: (bi, ci, 0)),
        const((CONV_K, CONV_DIM)), const((1, CONV_DIM)), const((1, LANES)), const((1, LANES)),
        const((1, D_SSD)), const((LANES, D_SSD)), const(((CONV_K - 1) * CHUNK, TAIL_ROWS + CHUNK)),
        const((3 * LANES, LANES)),
        const((SSD_GROUPS, D_STATE, GROUP_WIDTH)), const((TAIL_ROWS, CONV_DIM)), const((1, LANES)),
    ]
    out_shape = [
        jax.ShapeDtypeStruct((b, s, D_SSD), BF16),
        jax.ShapeDtypeStruct((b, s, LANES), BF16),
        jax.ShapeDtypeStruct((b, SSD_GROUPS, D_STATE, GROUP_WIDTH), F32),
        jax.ShapeDtypeStruct((b, TAIL_ROWS, CONV_DIM), BF16),
        jax.ShapeDtypeStruct((b, 1, LANES), F32),
    ]
    out_specs = [
        pl.BlockSpec((1, CHUNK, D_SSD), lambda bi, ci: (bi, ci, 0)),
        pl.BlockSpec((1, CHUNK, LANES), lambda bi, ci: (bi, ci, 0)),
        pl.BlockSpec((1, SSD_GROUPS, D_STATE, GROUP_WIDTH), lambda bi, ci: (bi, 0, 0, 0)),
        pl.BlockSpec((1, TAIL_ROWS, CONV_DIM), lambda bi, ci: (bi, 0, 0)),
        pl.BlockSpec((1, 1, LANES), lambda bi, ci: (bi, 0, 0)),
    ]
    return pl.pallas_call(
        functools.partial(_ssd_kernel, first_valid=first_valid),
        grid=(b, nc),
        in_specs=in_specs,
        out_specs=out_specs,
        out_shape=out_shape,
        scratch_shapes=[pltpu.VMEM((TAIL_ROWS + CHUNK, CONV_DIM), BF16),
                        pltpu.VMEM((CHUNK, CONV_DIM), F32)],
        compiler_params=pltpu.CompilerParams(
            dimension_semantics=("parallel", "arbitrary"),
            vmem_limit_bytes=VMEM_LIMIT),
        name=name,
    )(p1, p1, p1, small, convw, convb, brow, alog_row, dskip_row, expand_mat,
      _conv_shift_matrix(), _bias_placement_matrix(), st0, tail0, c0)


V_ROWS = 80
L_ROW = ATT_HEAD_DIM
STAT_ROWS = 8


def _placement_matrices():
    npairs = H_ATT // 2
    pq = np.zeros((npairs, 2, 2 * LANES, LANES), np.float32)
    pk = np.zeros((npairs, LANES, LANES), np.float32)
    pv = np.zeros((2, V_ROWS, LANES), np.float32)
    for hp in range(npairs):
        for hh in range(2):
            own0 = hh * ATT_HEAD_DIM
            base = (1 - hh) * ATT_HEAD_DIM
            for l in range(own0, own0 + ATT_HEAD_DIM):
                pq[hp, hh, l, l] = 1.0
            for t in range(3):
                pq[hp, hh, LANES + 3 * (2 * hp + hh) + t, base + t] = 1.0
                pq[hp, hh, LANES + BIAS_ONE_LANE, base + 3 + t] = 1.0
                pk[hp, BIAS_ONE_LANE, base + t] = 1.0
                pk[hp, 3 * (2 * hp + hh) + t, base + 3 + t] = -1.0
    for hh in range(2):
        for r in range(ATT_HEAD_DIM):
            pv[hh, r, hh * ATT_HEAD_DIM + r] = 1.0
    return (jnp.asarray(pq.transpose(0, 1, 3, 2), BF16), jnp.asarray(pk, BF16),
            jnp.asarray(pv, BF16))


def _bias_placement_matrix():
    m = np.zeros((3 * LANES, LANES), np.float32)
    for h in range(H_ATT):
        for t in range(3):
            m[t * LANES + F_LANE0 + h, 3 * h + t] = 1.0
    return jnp.asarray(m, BF16)


def _pair_schedule(nq, unroll):
    assert nq % (2 * unroll) == 0
    slots = [[] for _ in range(unroll)]
    for i in range(nq // 2):
        slots[i % unroll] += [i, nq - 1 - i]
    full = [[(qb, li, kb) for li, qb in enumerate(qbs) for kb in range(1, qb + 1)]
            for qbs in slots]
    diag = [[(qb, li, qb + 1) for li, qb in enumerate(qbs)] for qbs in slots]
    g1, g2 = len(full[0]), len(diag[0])
    assert all(len(f) == g1 for f in full) and all(len(d) == g2 for d in diag)
    parts = []
    for sched in (full, diag):
        parts.append([qb for s in sched for (qb, _, _) in s])
        parts.append([li for s in sched for (_, li, _) in s])
        parts.append([kb for s in sched for (_, _, kb) in s])
    table = np.concatenate([np.asarray(p, np.int32) for p in parts])
    return slots, g1, g2, table


def _attn_kernel(tab_ref, q_ref, k_ref, v_ref, bias_ref, km_ref, vm_ref, bm_ref, pq_ref, pk_ref,
                 pv_ref, o_ref, kaug_scr, vsel_scr, qaug_scr, s_scr, p_scr, alpha_scr, m_scr,
                 acc_scr, *, slots, g1, g2):
    seq = q_ref.shape[1]
    blk = ATT_BLOCK
    unroll = len(slots)
    neg_inf = jnp.float32(-jnp.inf)
    vrow = lax.broadcasted_iota(jnp.int32, (V_ROWS, 1), 0)

    lane = lax.broadcasted_iota(jnp.int32, (1, LANES), 1)
    own = (lane < ATT_HEAD_DIM, lane >= ATT_HEAD_DIM)

    def keys_operands(kb, bias):
        placed = jnp.dot(bias, pk_ref[0], preferred_element_type=F32).astype(BF16)
        return [jnp.where(own[hh], kb, placed) for hh in range(2)]

    def values_operand(vb, hh):
        vt = lax.dot_general(pv_ref[hh], vb, _NT, preferred_element_type=F32)
        return jnp.where(vrow == L_ROW, 1.0, vt).astype(BF16)

    def prep_chunk(r, carry):
        rows = pl.ds(pl.multiple_of(r * blk, blk), blk)
        krows = pl.ds(pl.multiple_of((r + 1) * blk, blk), blk)
        bias = bias_ref[0, rows, :]
        xq = jnp.concatenate([q_ref[0, rows, :], bias], axis=1)
        k_ops = keys_operands(k_ref[0, rows, :], bias)
        vb = v_ref[0, rows, :]
        for hh in range(2):
            qaug_scr[hh, r] = lax.dot_general(pq_ref[0, hh], xq, _NT,
                                              preferred_element_type=F32).astype(BF16)
            kaug_scr[hh, krows, :] = k_ops[hh]
            vsel_scr[hh, r + 1] = values_operand(vb, hh)
        return carry

    lax.fori_loop(0, seq // blk, prep_chunk, 0, unroll=8)

    meta_rows = slice(CHUNK - N_META, CHUNK)
    where = {qb: (u, li) for u, qbs in enumerate(slots) for li, qb in enumerate(qbs)}
    group = 4
    k_metas = keys_operands(km_ref[meta_rows, :], bm_ref[meta_rows, :])
    for hh in range(2):
        k_meta = k_metas[hh]
        v_meta = values_operand(vm_ref[meta_rows, :], hh)
        for q0 in range(0, seq // blk, group):
            q_t = jnp.concatenate([qaug_scr[hh, qb] for qb in range(q0, q0 + group)], axis=1)
            s = jnp.dot(k_meta, q_t, preferred_element_type=F32)
            m0 = jnp.max(s, axis=0, keepdims=True)
            acc0 = jnp.dot(v_meta, jnp.exp2(s - m0).astype(BF16), preferred_element_type=F32)
            for i in range(group):
                u, li = where[q0 + i]
                cols = slice(i * blk, (i + 1) * blk)
                m_scr[u, hh, li] = jnp.broadcast_to(m0[:, cols], m_scr.shape[-2:])
                acc_scr[u, hh, li] = acc0[:, cols]

    key_idx = lax.broadcasted_iota(jnp.int32, (blk, blk), 0)
    query_idx = lax.broadcasted_iota(jnp.int32, (blk, blk), 1)
    causal = key_idx <= query_idx
    stat_rows = m_scr.shape[-2]

    def run(base, ngroups, masked, s_scr, p_scr, alpha_scr):
        def entry(field, u, g):
            return tab_ref[base + (field * unroll + u) * ngroups + g]

        def stage_a(g):
            for u in range(unroll):
                qb = entry(0, u, g)
                krows = pl.ds(pl.multiple_of(entry(2, u, g) * blk, blk), blk)
                for hh in range(2):
                    s_scr[u, hh] = jnp.dot(kaug_scr[hh, krows, :], qaug_scr[hh, qb],
                                           preferred_element_type=F32)

        def stage_b(g):
            for u in range(unroll):
                li = entry(1, u, g)
                for hh in range(2):
                    s = s_scr[u, hh]
                    if masked:
                        s = jnp.where(causal, s, neg_inf)
                    m_prev = m_scr[u, hh, li][0:1]
                    m_next = jnp.maximum(m_prev, jnp.max(s, axis=0, keepdims=True))
                    alpha_scr[u, hh] = jnp.broadcast_to(jnp.exp2(m_prev - m_next),
                                                        (stat_rows, blk))
                    m_scr[u, hh, li] = jnp.broadcast_to(m_next, (stat_rows, blk))
                    p_scr[u, hh] = jnp.exp2(s - m_next).astype(BF16)

        def stage_c(g):
            for u in range(unroll):
                li = entry(1, u, g)
                kb = entry(2, u, g)
                for hh in range(2):
                    pv = jnp.dot(vsel_scr[hh, kb], p_scr[u, hh], preferred_element_type=F32)
                    acc_scr[u, hh, li] = acc_scr[u, hh, li] * alpha_scr[u, hh][0:1] + pv

        stage_a(0)
        stage_b(0)
        stage_a(min(1, ngroups - 1))

        def body(g, carry):
            stage_c(g - 1)
            stage_b(g)
            stage_a(jnp.minimum(g + 1, ngroups - 1))
            return carry

        lax.fori_loop(1, ngroups, body, 0)
        stage_c(ngroups - 1)

    run(0, g1, False, s_scr.at[0], p_scr.at[0], alpha_scr.at[0])
    run(3 * unroll * g1, g2, True, s_scr.at[1], p_scr.at[1], alpha_scr.at[1])

    for u, qbs in enumerate(slots):
        for li, qb in enumerate(qbs):
            outs = []
            for hh in range(2):
                a = acc_scr[u, hh, li]
                outs.append(a[0:ATT_HEAD_DIM] / a[L_ROW:L_ROW + 1])
            o_ref[0, qb * blk:(qb + 1) * blk, :] = jnp.concatenate(outs, axis=0).T.astype(o_ref.dtype)


def _attention(qkv, bias, qkv_meta, bias_meta, name):
    b, s, _ = qkv.shape
    npairs = H_ATT // 2
    blk = ATT_BLOCK
    slots, g1, g2, table = _pair_schedule(s // blk, ATT_UNROLL)
    nloc = len(slots[0])
    seq_blk = lambda col0: pl.BlockSpec((1, s, LANES), lambda bi, hp, tab: (bi, 0, col0 + hp))
    meta_blk = lambda col0: pl.BlockSpec((CHUNK, LANES), lambda bi, hp, tab: (0, col0 + hp))
    grid_spec = pltpu.PrefetchScalarGridSpec(
        num_scalar_prefetch=1,
        grid=(b, npairs),
        in_specs=[
            seq_blk(Q_BLOCK0), seq_blk(Q_BLOCK0 + npairs), seq_blk(Q_BLOCK0 + 2 * npairs),
            pl.BlockSpec((1, s, LANES), lambda bi, hp, tab: (bi, 0, 0)),
            meta_blk(Q_BLOCK0 + npairs), meta_blk(Q_BLOCK0 + 2 * npairs),
            pl.BlockSpec((CHUNK, LANES), lambda bi, hp, tab: (0, 0)),
            pl.BlockSpec((1, 2, LANES, 2 * LANES), lambda bi, hp, tab: (hp, 0, 0, 0)),
            pl.BlockSpec((1, LANES, LANES), lambda bi, hp, tab: (hp, 0, 0)),
            pl.BlockSpec((2, V_ROWS, LANES), lambda bi, hp, tab: (0, 0, 0)),
        ],
        out_specs=pl.BlockSpec((1, s, LANES), lambda bi, hp, tab: (bi, 0, hp)),
        scratch_shapes=[
            pltpu.VMEM((2, s + blk, LANES), BF16),
            pltpu.VMEM((2, s // blk + 1, V_ROWS, blk), BF16),
            pltpu.VMEM((2, s // blk, LANES, blk), BF16),
            pltpu.VMEM((2, ATT_UNROLL, 2, blk, blk), F32),
            pltpu.VMEM((2, ATT_UNROLL, 2, blk, blk), BF16),
            pltpu.VMEM((2, ATT_UNROLL, 2, STAT_ROWS, blk), F32),
            pltpu.VMEM((ATT_UNROLL, 2, nloc, STAT_ROWS, blk), F32),
            pltpu.VMEM((ATT_UNROLL, 2, nloc, V_ROWS, blk), F32),
        ])
    return pl.pallas_call(
        functools.partial(_attn_kernel, slots=slots, g1=g1, g2=g2),
        grid_spec=grid_spec,
        out_shape=jax.ShapeDtypeStruct((b, s, D_ATT), BF16),
        compiler_params=pltpu.CompilerParams(
            dimension_semantics=("parallel", "arbitrary"),
            vmem_limit_bytes=VMEM_LIMIT),
        name=name,
    )(jnp.asarray(table), qkv, qkv, qkv, bias, qkv_meta, qkv_meta, bias_meta,
      *_placement_matrices())


def _merge_kernel(y_ref, zs_ref, o_ref, za_ref, gs_ref, ga_ref, x_ref, wps_ref, wpa_ref,
                  wout_ref, snorm_ref, gbias_ref, npost_ref, out_ref):
    zs = zs_ref[...].astype(F32)
    u = y_ref[...].astype(F32) * (zs * jax.nn.sigmoid(zs))
    parts = []
    for g in range(SSD_GROUPS):
        ug = u[:, g * GROUP_WIDTH:(g + 1) * GROUP_WIDTH]
        ms = jnp.mean(ug * ug, axis=-1, keepdims=True)
        parts.append((ug * lax.rsqrt(ms + EPS)
                      * snorm_ref[:, g * GROUP_WIDTH:(g + 1) * GROUP_WIDTH]).astype(BF16))
    y_ssd = jnp.concatenate(parts, axis=1)
    t_ssd = jnp.dot(y_ssd, wps_ref[...], preferred_element_type=F32)
    za = za_ref[...].astype(F32)
    y_att = (o_ref[...].astype(F32) * (za * jax.nn.sigmoid(za))).astype(BF16)
    t_att = jnp.dot(y_att, wpa_ref[...], preferred_element_type=F32)
    g_ssd = jax.nn.sigmoid(gs_ref[...].astype(F32) + gbias_ref[:, 0:D_MODEL])
    g_att = jax.nn.sigmoid(ga_ref[...].astype(F32) + gbias_ref[:, D_MODEL:2 * D_MODEL])
    merged = (g_ssd * t_ssd + g_att * t_att).astype(BF16)
    t = jnp.dot(merged, wout_ref[...], preferred_element_type=F32)
    ms = jnp.mean(t * t, axis=-1, keepdims=True)
    out_ref[...] = x_ref[...] + t * lax.rsqrt(ms + EPS) * npost_ref[...]


def _merge(y2d, p1, o2d, x2d, wps, wpa, wout, snorm, gbias, npost, tm, name):
    m = x2d.shape[0]
    row_blk = lambda width, col: pl.BlockSpec((tm, width), lambda i: (i, col))
    const = lambda shape: pl.BlockSpec(shape, lambda i: (0, 0), pipeline_mode=pl.Buffered(1))
    in_specs = [
        row_blk(D_SSD, 0),
        row_blk(D_SSD, 0),
        row_blk(D_ATT, 0),
        row_blk(D_ATT, 5),
        row_blk(D_MODEL, 6),
        row_blk(D_MODEL, 7),
        row_blk(D_MODEL, 0),
        const((D_SSD, D_MODEL)), const((D_ATT, D_MODEL)), const((D_MODEL, D_MODEL)),
        const((1, D_SSD)), const((1, 2 * D_MODEL)), const((1, D_MODEL)),
    ]
    return pl.pallas_call(
        _merge_kernel,
        grid=(m // tm,),
        in_specs=in_specs,
        out_specs=pl.BlockSpec((tm, D_MODEL), lambda i: (i, 0)),
        out_shape=jax.ShapeDtypeStruct((m, D_MODEL), F32),
        compiler_params=pltpu.CompilerParams(
            dimension_semantics=("parallel",),
            vmem_limit_bytes=VMEM_LIMIT),
        name=name,
    )(y2d, p1, o2d, p1, p1, p1, x2d, wps, wpa, wout, snorm, gbias, npost)


def kernel(x, meta_tokens, norm_pre, w_in, conv_w, conv_b, dt_bias, a_log, d_skip, ssd_norm,
           fgate_bias, gate_bias, w_proj_ssd, w_proj_att, w_out, norm_post):
    bsz, seq, d = x.shape
    assert d == D_MODEL and seq % 1024 == 0 and norm_pre.shape[0] == 1

    w = jnp.swapaxes(w_in[0], 0, 1)
    o_xbc = D_SSD
    o_dt = o_xbc + CONV_DIM
    o_za = o_dt + H_SSD
    o_q = o_za + D_ATT
    o_k = o_q + D_ATT
    o_v = o_k + D_ATT
    o_f = o_v + D_ATT
    o_g = o_f + H_ATT
    att_scale = LOG2E / (ATT_HEAD_DIM ** 0.5)
    w_all = jnp.concatenate(
        [w[0:o_dt].astype(BF16), w[o_za:o_q].astype(BF16), w[o_g:].astype(BF16),
         (w[o_q:o_k] * att_scale).astype(BF16), w[o_k:o_f].astype(BF16)], axis=0)
    w_small = jnp.concatenate(
        [w[o_dt:o_za], w[o_f:o_g],
         jnp.zeros((LANES - H_SSD - H_ATT, d), F32)], axis=0).astype(BF16)
    g_pre = norm_pre[0][None, :]
    pad_lanes = jnp.zeros((LANES - H_SSD - H_ATT,), F32)
    brow = jnp.concatenate([dt_bias[0], fgate_bias[0], pad_lanes])[None, :]
    alog_row = jnp.concatenate([a_log[0], jnp.zeros((LANES - H_SSD,), F32)])[None, :]
    dskip_row = jnp.repeat(d_skip[0], SSD_HEAD_DIM)[None, :]
    head_of_lane = jnp.arange(D_SSD, dtype=jnp.int32) // SSD_HEAD_DIM
    expand_mat = (jnp.arange(LANES, dtype=jnp.int32)[:, None] == head_of_lane[None, :]).astype(BF16)
    convw = conv_w[0]
    convb = conv_b[0][None, :]
    wps = w_proj_ssd[0].astype(BF16)
    wpa = w_proj_att[0].astype(BF16)
    wout = w_out[0].astype(BF16)
    snorm = ssd_norm[0][None, :]
    gbias = gate_bias[0][None, :]
    npost = norm_post[0][None, :]

    meta_blk = jnp.concatenate([jnp.zeros((CHUNK - N_META, d), F32), meta_tokens.astype(F32)], axis=0)
    p_m, small_m = _in_proj(meta_blk, g_pre, w_all, w_small, BF16, CHUNK, P_TILE_N, "in_proj_meta")
    st_zero = jnp.zeros((SSD_GROUPS, D_STATE, GROUP_WIDTH), F32)
    tail_zero = jnp.zeros((TAIL_ROWS, CONV_DIM), BF16)
    c_zero = jnp.zeros((1, LANES), F32)
    _, bias_m, st_m, tail_m, cc_m = _ssd(
        p_m[None], small_m[None], convw, convb, brow, alog_row, dskip_row, expand_mat,
        st_zero, tail_zero, c_zero, CHUNK - N_META, "ssd_meta")

    x2d = x.reshape(bsz * seq, d)
    p, small = _in_proj(x2d, g_pre, w_all, w_small, BF16, 1024, P_TILE_N, "in_proj")
    p3d = p.reshape(bsz, seq, P_COLS)
    y, bias, _, _, _ = _ssd(
        p3d, small.reshape(bsz, seq, LANES), convw, convb, brow,
        alog_row, dskip_row, expand_mat, st_m[0], tail_m[0], cc_m[0], 0, "ssd")
    o = _attention(p3d, bias, p_m, bias_m[0], "fox_attention")
    out = _merge(y.reshape(bsz * seq, D_SSD), p, o.reshape(bsz * seq, D_ATT), x2d,
                 wps, wpa, wout, snorm, gbias, npost, 512, "merge_out")
    return out.reshape(bsz, seq, d)
```

```python
import functools

import numpy as np

import jax
import jax.numpy as jnp
from jax import lax
from jax.experimental import pallas as pl
from jax.experimental.pallas import tpu as pltpu

F32 = jnp.float32
BF16 = jnp.bfloat16

D_MODEL = 1024
N_META = 16
CHUNK = 128
LANES = 128
D_SSD = 2 * D_MODEL
SSD_HEAD_DIM = 64
H_SSD = D_SSD // SSD_HEAD_DIM
SSD_GROUPS = 4
HEADS_PER_GROUP = H_SSD // SSD_GROUPS
D_STATE = 128
GROUP_WIDTH = HEADS_PER_GROUP * SSD_HEAD_DIM
CONV_K = 4
CONV_DIM = D_SSD + 2 * SSD_GROUPS * D_STATE
H_ATT = 16
ATT_HEAD_DIM = 64
D_ATT = H_ATT * ATT_HEAD_DIM
EPS = 1e-6
LOG2E = 1.4426950408889634
TAIL_ROWS = 16
F_LANE0 = H_SSD
P_COLS = 11 * 1024
P_TILE_N = P_COLS // 4
Q_BLOCK0 = 8 * (1024 // LANES)
VMEM_LIMIT = 56 * 1024 * 1024
ATT_BLOCK = 256
ATT_UNROLL = 4
MASK_BIAS = -1.0e30
BIAS_ONE_LANE = 3 * H_ATT

_NT = (((1,), (1,)), ((), ()))


def _split3(v):
    hi = v.astype(BF16)
    r1 = v - hi.astype(F32)
    mid = r1.astype(BF16)
    lo = (r1 - mid.astype(F32)).astype(BF16)
    return hi, mid, lo


def _dot_exact_lhs(a_bf16, v):
    hi, mid, lo = _split3(v)
    d = functools.partial(jnp.dot, preferred_element_type=F32)
    return d(a_bf16, hi) + d(a_bf16, mid) + d(a_bf16, lo)


def _dot_exact_rhs(v, b_bf16):
    hi, mid, lo = _split3(v)
    d = functools.partial(jnp.dot, preferred_element_type=F32)
    return d(hi, b_bf16) + d(mid, b_bf16) + d(lo, b_bf16)


def _in_proj_kernel(x_ref, g_ref, w_ref, *rest, with_small):
    if with_small:
        ws_ref, o_ref, os_ref, u_ref = rest
    else:
        o_ref, u_ref = rest
    j = pl.program_id(1)

    @pl.when(j == 0)
    def _():
        x = x_ref[...]
        ms = jnp.mean(x * x, axis=-1, keepdims=True)
        u = (x * lax.rsqrt(ms + EPS) * g_ref[...]).astype(BF16)
        u_ref[...] = u
        if with_small:
            os_ref[...] = lax.dot_general(u, ws_ref[...], _NT, preferred_element_type=F32)

    o_ref[...] = lax.dot_general(u_ref[...], w_ref[...], _NT,
                                 preferred_element_type=F32).astype(o_ref.dtype)


def _in_proj(x2d, g_row, w_t, w_small_t, out_dtype, tm, tn, name):
    m, d = x2d.shape
    n = w_t.shape[0]
    with_small = w_small_t is not None
    in_specs = [pl.BlockSpec((tm, d), lambda i, j: (i, 0)),
                pl.BlockSpec((1, d), lambda i, j: (0, 0)),
                pl.BlockSpec((tn, d), lambda i, j: (j, 0))]
    out_shape = [jax.ShapeDtypeStruct((m, n), out_dtype)]
    out_specs = [pl.BlockSpec((tm, tn), lambda i, j: (i, j))]
    args = [x2d, g_row, w_t]
    w_small = w_small_t
    if with_small:
        in_specs.append(pl.BlockSpec((LANES, d), lambda i, j: (0, 0)))
        out_shape.append(jax.ShapeDtypeStruct((m, LANES), F32))
        out_specs.append(pl.BlockSpec((tm, LANES), lambda i, j: (i, 0)))
        args.append(w_small)
    return pl.pallas_call(
        functools.partial(_in_proj_kernel, with_small=with_small),
        grid=(m // tm, n // tn),
        in_specs=in_specs,
        out_specs=out_specs,
        out_shape=out_shape,
        scratch_shapes=[pltpu.VMEM((tm, d), BF16)],
        compiler_params=pltpu.CompilerParams(
            dimension_semantics=("parallel", "arbitrary"),
            vmem_limit_bytes=VMEM_LIMIT),
        name=name,
    )(*args)


def _ssd_kernel(x0_ref, x1_ref, bc_ref, s_ref, convw_ref, convb_ref, brow_ref, alog_ref,
                dskip_ref, e_ref, shift_ref, pb_ref, st0_ref, tail0_ref, c0_ref,
                y_ref, bias_ref, st_ref, tail_ref, cc_ref,
                ext_ref, act_ref, *, first_valid):
    ci = pl.program_id(1)

    @pl.when(ci == 0)
    def _():
        st_ref[0] = st0_ref[...]
        tail_ref[0] = tail0_ref[...]
        cc_ref[0] = c0_ref[...]

    ext_ref[0:TAIL_ROWS, :] = tail_ref[0]
    ext_ref[TAIL_ROWS:TAIL_ROWS + CHUNK, 0:1024] = x0_ref[0]
    ext_ref[TAIL_ROWS:TAIL_ROWS + CHUNK, 1024:2048] = x1_ref[0]
    ext_ref[TAIL_ROWS:TAIL_ROWS + CHUNK, 2048:3072] = bc_ref[0]
    tail_ref[0] = ext_ref[CHUNK:CHUNK + TAIL_ROWS, :]

    row = lax.broadcasted_iota(jnp.int32, (CHUNK, 1), 0)
    valid_row = row >= first_valid
    shift = shift_ref[...]
    for j in range(CONV_DIM // GROUP_WIDTH):
        sl = slice(j * GROUP_WIDTH, (j + 1) * GROUP_WIDTH)
        taps = jnp.dot(shift, ext_ref[:, sl], preferred_element_type=F32)
        acc = (convb_ref[:, sl] + convw_ref[CONV_K - 1:CONV_K, sl]
               * ext_ref[TAIL_ROWS:TAIL_ROWS + CHUNK, sl].astype(F32))
        for k in range(CONV_K - 1):
            acc = acc + convw_ref[k:k + 1, sl] * taps[k * CHUNK:(k + 1) * CHUNK]
        a = acc * jax.nn.sigmoid(acc)
        if first_valid:
            a = jnp.where(valid_row, a, 0.0)
        act_ref[:, sl] = a

    lane = lax.broadcasted_iota(jnp.int32, (1, LANES), 1)
    is_dt = lane < H_SSD
    is_f = (lane >= F_LANE0) & (lane < F_LANE0 + H_ATT)
    sv = s_ref[0] + brow_ref[...]
    z = jnp.where(is_dt, sv, -sv)
    sp = jnp.maximum(z, 0.0) + jnp.log(1.0 + jnp.exp(-jnp.abs(z)))
    a_row = -jnp.exp(alog_ref[...])
    vals = jnp.where(is_dt, sp * a_row, jnp.where(is_f, -sp, 0.0))
    dt = jnp.where(is_dt, sp, 0.0)
    if first_valid:
        vals = jnp.where(valid_row, vals, 0.0)
        dt = jnp.where(valid_row, dt, 0.0)

    ri = lax.broadcasted_iota(jnp.int32, (CHUNK, CHUNK), 0)
    cj = lax.broadcasted_iota(jnp.int32, (CHUNK, CHUNK), 1)
    tri = cj <= ri
    cum = _dot_exact_lhs(tri.astype(BF16), vals)
    cfull = cum + cc_ref[0]
    c_log2 = LOG2E * cfull
    if first_valid:
        c_log2 = jnp.where(valid_row, c_log2, -MASK_BIAS)
    terms = jnp.concatenate(_split3(c_log2), axis=1)
    placed = jnp.dot(terms, pb_ref[...], preferred_element_type=F32)
    bias_ref[0] = jnp.where(lane == BIAS_ONE_LANE, 1.0, placed).astype(BF16)
    cum_t = cfull.T
    cc_ref[0] = jnp.where(is_f, cfull[CHUNK - 1:CHUNK, :], 0.0)
    dt_t = dt.T

    a_last = cum[CHUNK - 1:CHUNK, :]
    w_t = (dt * jnp.exp(a_last - cum)).T
    cd = jnp.where(is_dt, jnp.exp(a_last), 0.0)
    cd_e = _dot_exact_rhs(jnp.broadcast_to(cd, (8, LANES)), e_ref[...])[0:1]

    lo_half = lane < SSD_HEAD_DIM
    neg_inf = jnp.float32(-jnp.inf)

    def split_heads(x16):
        zero16 = jnp.zeros_like(x16)
        return jnp.concatenate([jnp.where(lo_half, x16, zero16),
                                jnp.where(lo_half, zero16, x16)], axis=0)

    for g in range(SSD_GROUPS):
        bm = act_ref[:, D_SSD + g * D_STATE:D_SSD + (g + 1) * D_STATE]
        cm = act_ref[:, D_SSD + SSD_GROUPS * D_STATE + g * D_STATE:
                     D_SSD + SSD_GROUPS * D_STATE + (g + 1) * D_STATE]
        cm16 = cm.astype(BF16)
        cb = lax.dot_general(cm16, bm.astype(BF16), _NT,
                             preferred_element_type=F32).astype(BF16)
        bm_t = bm.T.astype(BF16)
        for pr in range(HEADS_PER_GROUP // 2):
            psl = slice(g * GROUP_WIDTH + pr * LANES, g * GROUP_WIDTH + (pr + 1) * LANES)
            ssl = slice(pr * LANES, (pr + 1) * LANES)
            xs_p = act_ref[:, psl]
            hin = st_ref[0, g, :, ssl]
            intra, inter, bw = [], [], []
            for h in (g * HEADS_PER_GROUP + 2 * pr, g * HEADS_PER_GROUP + 2 * pr + 1):
                a_col = cum[:, h:h + 1]
                seg = a_col - cum_t[h:h + 1, :]
                dec = jnp.exp(jnp.where(tri, seg, neg_inf)).astype(BF16)
                intra.append(cb * dec * dt_t[h:h + 1, :].astype(BF16))
                ea = jnp.broadcast_to(jnp.exp(a_col), (CHUNK, D_STATE)).astype(BF16)
                inter.append(cm16 * ea)
                bw.append(bm_t * w_t[h:h + 1, :].astype(BF16))
            xs2 = split_heads(xs_p.astype(BF16))
            y_lhs = jnp.concatenate(intra + inter, axis=1)
            y_rhs = jnp.concatenate([xs2, split_heads(hin.astype(BF16))], axis=0)
            y_pair = jnp.dot(y_lhs, y_rhs, preferred_element_type=F32)
            y_ref[0, :, psl] = (y_pair + dskip_ref[:, psl] * xs_p).astype(y_ref.dtype)
            st_ref[0, g, :, ssl] = hin * cd_e[:, psl] + jnp.dot(
                jnp.concatenate(bw, axis=1), xs2, preferred_element_type=F32)


def _conv_shift_matrix():
    m = np.zeros(((CONV_K - 1) * CHUNK, TAIL_ROWS + CHUNK), np.float32)
    for k in range(CONV_K - 1):
        for t in range(CHUNK):
            m[k * CHUNK + t, TAIL_ROWS + t - (CONV_K - 1) + k] = 1.0
    return jnp.asarray(m, BF16)


def _ssd(p1, small, convw, convb, brow, alog_row, dskip_row, expand_mat, st0, tail0, c0,
         first_valid, name):
    b, s, _ = p1.shape
    nc = s // CHUNK
    const = lambda shape: pl.BlockSpec(shape, lambda bi, ci: (0,) * len(shape))
    in_specs = [
        pl.BlockSpec((1, CHUNK, 1024), lambda bi, ci: (bi, ci, 2)),
        pl.BlockSpec((1, CHUNK, 1024), lambda bi, ci: (bi, ci, 3)),
        pl.BlockSpec((1, CHUNK, 1024), lambda bi, ci: (bi, ci, 4)),
        pl.BlockSpec((1, CHUNK, LANES), lambda bi, ci: (bi, ci, 0)),
        const((CONV_K, CONV_DIM)), const((1, CONV_DIM)), const((1, LANES)), const((1, LANES)),
        const((1, D_SSD)), const((LANES, D_SSD)), const(((CONV_K - 1) * CHUNK, TAIL_ROWS + CHUNK)),
        const((3 * LANES, LANES)),
        const((SSD_GROUPS, D_STATE, GROUP_WIDTH)), const((TAIL_ROWS, CONV_DIM)), const((1, LANES)),
    ]
    out_shape = [
        jax.ShapeDtypeStruct((b, s, D_SSD), BF16),
        jax.ShapeDtypeStruct((b, s, LANES), BF16),
        jax.ShapeDtypeStruct((b, SSD_GROUPS, D_STATE, GROUP_WIDTH), F32),
        jax.ShapeDtypeStruct((b, TAIL_ROWS, CONV_DIM), BF16),
        jax.ShapeDtypeStruct((b, 1, LANES), F32),
    ]
    out_specs = [
        pl.BlockSpec((1, CHUNK, D_SSD), lambda bi, ci: (bi, ci, 0)),
        pl.BlockSpec((1, CHUNK, LANES), lambda bi, ci: (bi, ci, 0)),
        pl.BlockSpec((1, SSD_GROUPS, D_STATE, GROUP_WIDTH), lambda bi, ci: (bi, 0, 0, 0)),
        pl.BlockSpec((1, TAIL_ROWS, CONV_DIM), lambda bi, ci: (bi, 0, 0)),
        pl.BlockSpec((1, 1, LANES), lambda bi, ci: (bi, 0, 0)),
    ]
    return pl.pallas_call(
        functools.partial(_ssd_kernel, first_valid=first_valid),
        grid=(b, nc),
        in_specs=in_specs,
        out_specs=out_specs,
        out_shape=out_shape,
        scratch_shapes=[pltpu.VMEM((TAIL_ROWS + CHUNK, CONV_DIM), BF16),
                        pltpu.VMEM((CHUNK, CONV_DIM), F32)],
        compiler_params=pltpu.CompilerParams(
            dimension_semantics=("parallel", "arbitrary"),
            vmem_limit_bytes=VMEM_LIMIT),
        name=name,
    )(p1, p1, p1, small, convw, convb, brow, alog_row, dskip_row, expand_mat,
      _conv_shift_matrix(), _bias_placement_matrix(), st0, tail0, c0)


V_ROWS = 80
L_ROW = ATT_HEAD_DIM
STAT_ROWS = 8


def _placement_matrices():
    npairs = H_ATT // 2
    pq = np.zeros((npairs, 2, 2 * LANES, LANES), np.float32)
    pk = np.zeros((npairs, LANES, LANES), np.float32)
    pv = np.zeros((2, V_ROWS, LANES), np.float32)
    for hp in range(npairs):
        for hh in range(2):
            own0 = hh * ATT_HEAD_DIM
            base = (1 - hh) * ATT_HEAD_DIM
            for l in range(own0, own0 + ATT_HEAD_DIM):
                pq[hp, hh, l, l] = 1.0
            for t in range(3):
                pq[hp, hh, LANES + 3 * (2 * hp + hh) + t, base + t] = 1.0
                pq[hp, hh, LANES + BIAS_ONE_LANE, base + 3 + t] = 1.0
                pk[hp, BIAS_ONE_LANE, base + t] = 1.0
                pk[hp, 3 * (2 * hp + hh) + t, base + 3 + t] = -1.0
    for hh in range(2):
        for r in range(ATT_HEAD_DIM):
            pv[hh, r, hh * ATT_HEAD_DIM + r] = 1.0
    return (jnp.asarray(pq.transpose(0, 1, 3, 2), BF16), jnp.asarray(pk, BF16),
            jnp.asarray(pv, BF16))


def _bias_placement_matrix():
    m = np.zeros((3 * LANES, LANES), np.float32)
    for h in range(H_ATT):
        for t in range(3):
            m[t * LANES + F_LANE0 + h, 3 * h + t] = 1.0
    return jnp.asarray(m, BF16)


def _pair_schedule(nq, unroll):
    assert nq % (2 * unroll) == 0
    slots = [[] for _ in range(unroll)]
    for i in range(nq // 2):
        slots[i % unroll] += [i, nq - 1 - i]
    full = [[(qb, li, kb) for li, qb in enumerate(qbs) for kb in range(1, qb + 1)]
            for qbs in slots]
    diag = [[(qb, li, qb + 1) for li, qb in enumerate(qbs)] for qbs in slots]
    g1, g2 = len(full[0]), len(diag[0])
    assert all(len(f) == g1 for f in full) and all(len(d) == g2 for d in diag)
    parts = []
    for sched in (full, diag):
        parts.append([qb for s in sched for (qb, _, _) in s])
        parts.append([li for s in sched for (_, li, _) in s])
        parts.append([kb for s in sched for (_, _, kb) in s])
    table = np.concatenate([np.asarray(p, np.int32) for p in parts])
    return slots, g1, g2, table


def _attn_kernel(tab_ref, q_ref, k_ref, v_ref, bias_ref, km_ref, vm_ref, bm_ref, pq_ref, pk_ref,
                 pv_ref, o_ref, kaug_scr, vsel_scr, qaug_scr, s_scr, p_scr, alpha_scr, m_scr,
                 acc_scr, *, slots, g1, g2):
    seq = q_ref.shape[1]
    blk = ATT_BLOCK
    unroll = len(slots)
    neg_inf = jnp.float32(-jnp.inf)
    vrow = lax.broadcasted_iota(jnp.int32, (V_ROWS, 1), 0)

    lane = lax.broadcasted_iota(jnp.int32, (1, LANES), 1)
    own = (lane < ATT_HEAD_DIM, lane >= ATT_HEAD_DIM)

    def keys_operands(kb, bias):
        placed = jnp.dot(bias, pk_ref[0], preferred_element_type=F32).astype(BF16)
        return [jnp.where(own[hh], kb, placed) for hh in range(2)]

    def values_operand(vb, hh):
        vt = lax.dot_general(pv_ref[hh], vb, _NT, preferred_element_type=F32)
        return jnp.where(vrow == L_ROW, 1.0, vt).astype(BF16)

    def prep_chunk(r, carry):
        rows = pl.ds(pl.multiple_of(r * blk, blk), blk)
        krows = pl.ds(pl.multiple_of((r + 1) * blk, blk), blk)
        bias = bias_ref[0, rows, :]
        xq = jnp.concatenate([q_ref[0, rows, :], bias], axis=1)
        k_ops = keys_operands(k_ref[0, rows, :], bias)
        vb = v_ref[0, rows, :]
        for hh in range(2):
            qaug_scr[hh, r] = lax.dot_general(pq_ref[0, hh], xq, _NT,
                                              preferred_element_type=F32).astype(BF16)
            kaug_scr[hh, krows, :] = k_ops[hh]
            vsel_scr[hh, r + 1] = values_operand(vb, hh)
        return carry

    lax.fori_loop(0, seq // blk, prep_chunk, 0, unroll=8)

    meta_rows = slice(CHUNK - N_META, CHUNK)
    where = {qb: (u, li) for u, qbs in enumerate(slots) for li, qb in enumerate(qbs)}
    k_metas = keys_operands(km_ref[meta_rows, :], bm_ref[meta_rows, :])
    for hh in range(2):
        v_meta = values_operand(vm_ref[meta_rows, :], hh)
        q_t = jnp.concatenate([qaug_scr[hh, qb] for qb in range(seq // blk)], axis=1)
        s = jnp.dot(k_metas[hh], q_t, preferred_element_type=F32)
        m0 = jnp.max(s, axis=0, keepdims=True)
        acc0 = jnp.dot(v_meta, jnp.exp2(s - m0).astype(BF16), preferred_element_type=F32)
        for qb, (u, li) in where.items():
            cols = slice(qb * blk, (qb + 1) * blk)
            m_scr[u, hh, li] = jnp.broadcast_to(m0[:, cols], m_scr.shape[-2:])
            acc_scr[u, hh, li] = acc0[:, cols]

    key_idx = lax.broadcasted_iota(jnp.int32, (blk, blk), 0)
    query_idx = lax.broadcasted_iota(jnp.int32, (blk, blk), 1)
    causal = key_idx <= query_idx
    stat_rows = m_scr.shape[-2]

    def run(base, ngroups, masked, s_scr, p_scr, alpha_scr):
        def entry(field, u, g):
            return tab_ref[base + (field * unroll + u) * ngroups + g]

        def stage_a(g):
            for u in range(unroll):
                qb = entry(0, u, g)
                krows = pl.ds(pl.multiple_of(entry(2, u, g) * blk, blk), blk)
                for hh in range(2):
                    s_scr[u, hh] = jnp.dot(kaug_scr[hh, krows, :], qaug_scr[hh, qb],
                                           preferred_element_type=F32)

        def stage_b(g):
            for u in range(unroll):
                li = entry(1, u, g)
                for hh in range(2):
                    s = s_scr[u, hh]
                    if masked:
                        s = jnp.where(causal, s, neg_inf)
                    m_prev = m_scr[u, hh, li][0:1]
                    m_next = jnp.maximum(m_prev, jnp.max(s, axis=0, keepdims=True))
                    alpha_scr[u, hh] = jnp.broadcast_to(jnp.exp2(m_prev - m_next),
                                                        (stat_rows, blk))
                    m_scr[u, hh, li] = jnp.broadcast_to(m_next, (stat_rows, blk))
                    p_scr[u, hh] = jnp.exp2(s - m_next).astype(BF16)

        def stage_c(g):
            for u in range(unroll):
                li = entry(1, u, g)
                kb = entry(2, u, g)
                for hh in range(2):
                    pv = jnp.dot(vsel_scr[hh, kb], p_scr[u, hh], preferred_element_type=F32)
                    acc_scr[u, hh, li] = acc_scr[u, hh, li] * alpha_scr[u, hh][0:1] + pv

        stage_a(0)
        stage_b(0)
        stage_a(min(1, ngroups - 1))

        def body(g, carry):
            stage_c(g - 1)
            stage_b(g)
            stage_a(jnp.minimum(g + 1, ngroups - 1))
            return carry

        lax.fori_loop(1, ngroups, body, 0)
        stage_c(ngroups - 1)

    run(0, g1, False, s_scr.at[0], p_scr.at[0], alpha_scr.at[0])
    run(3 * unroll * g1, g2, True, s_scr.at[1], p_scr.at[1], alpha_scr.at[1])

    for u, qbs in enumerate(slots):
        for li, qb in enumerate(qbs):
            outs = []
            for hh in range(2):
                a = acc_scr[u, hh, li]
                outs.append(a[0:ATT_HEAD_DIM] / a[L_ROW:L_ROW + 1])
            o_ref[0, qb * blk:(qb + 1) * blk, :] = jnp.concatenate(outs, axis=0).T.astype(o_ref.dtype)


def _attention(qkv, bias, qkv_meta, bias_meta, name):
    b, s, _ = qkv.shape
    npairs = H_ATT // 2
    blk = ATT_BLOCK
    slots, g1, g2, table = _pair_schedule(s // blk, ATT_UNROLL)
    nloc = len(slots[0])
    seq_blk = lambda col0: pl.BlockSpec((1, s, LANES), lambda bi, hp, tab: (bi, 0, col0 + hp))
    meta_blk = lambda col0: pl.BlockSpec((CHUNK, LANES), lambda bi, hp, tab: (0, col0 + hp))
    grid_spec = pltpu.PrefetchScalarGridSpec(
        num_scalar_prefetch=1,
        grid=(b, npairs),
        in_specs=[
            seq_blk(Q_BLOCK0), seq_blk(Q_BLOCK0 + npairs), seq_blk(Q_BLOCK0 + 2 * npairs),
            pl.BlockSpec((1, s, LANES), lambda bi, hp, tab: (bi, 0, 0)),
            meta_blk(Q_BLOCK0 + npairs), meta_blk(Q_BLOCK0 + 2 * npairs),
            pl.BlockSpec((CHUNK, LANES), lambda bi, hp, tab: (0, 0)),
            pl.BlockSpec((1, 2, LANES, 2 * LANES), lambda bi, hp, tab: (hp, 0, 0, 0)),
            pl.BlockSpec((1, LANES, LANES), lambda bi, hp, tab: (hp, 0, 0)),
            pl.BlockSpec((2, V_ROWS, LANES), lambda bi, hp, tab: (0, 0, 0)),
        ],
        out_specs=pl.BlockSpec((1, s, LANES), lambda bi, hp, tab: (bi, 0, hp)),
        scratch_shapes=[
            pltpu.VMEM((2, s + blk, LANES), BF16),
            pltpu.VMEM((2, s // blk + 1, V_ROWS, blk), BF16),
            pltpu.VMEM((2, s // blk, LANES, blk), BF16),
            pltpu.VMEM((2, ATT_UNROLL, 2, blk, blk), F32),
            pltpu.VMEM((2, ATT_UNROLL, 2, blk, blk), BF16),
            pltpu.VMEM((2, ATT_UNROLL, 2, STAT_ROWS, blk), F32),
            pltpu.VMEM((ATT_UNROLL, 2, nloc, STAT_ROWS, blk), F32),
            pltpu.VMEM((ATT_UNROLL, 2, nloc, V_ROWS, blk), F32),
        ])
    return pl.pallas_call(
        functools.partial(_attn_kernel, slots=slots, g1=g1, g2=g2),
        grid_spec=grid_spec,
        out_shape=jax.ShapeDtypeStruct((b, s, D_ATT), BF16),
        compiler_params=pltpu.CompilerParams(
            dimension_semantics=("parallel", "arbitrary"),
            vmem_limit_bytes=VMEM_LIMIT),
        name=name,
    )(jnp.asarray(table), qkv, qkv, qkv, bias, qkv_meta, qkv_meta, bias_meta,
      *_placement_matrices())


def _merge_kernel(y_ref, zs_ref, o_ref, za_ref, gs_ref, ga_ref, x_ref, wps_ref, wpa_ref,
                  wout_ref, snorm_ref, gbias_ref, npost_ref, out_ref):
    zs = zs_ref[...].astype(F32)
    u = y_ref[...].astype(F32) * (zs * jax.nn.sigmoid(zs))
    parts = []
    for g in range(SSD_GROUPS):
        ug = u[:, g * GROUP_WIDTH:(g + 1) * GROUP_WIDTH]
        ms = jnp.mean(ug * ug, axis=-1, keepdims=True)
        parts.append((ug * lax.rsqrt(ms + EPS)
                      * snorm_ref[:, g * GROUP_WIDTH:(g + 1) * GROUP_WIDTH]).astype(BF16))
    y_ssd = jnp.concatenate(parts, axis=1)
    t_ssd = jnp.dot(y_ssd, wps_ref[...], preferred_element_type=F32)
    za = za_ref[...].astype(F32)
    y_att = (o_ref[...].astype(F32) * (za * jax.nn.sigmoid(za))).astype(BF16)
    t_att = jnp.dot(y_att, wpa_ref[...], preferred_element_type=F32)
    g_ssd = jax.nn.sigmoid(gs_ref[...].astype(F32) + gbias_ref[:, 0:D_MODEL])
    g_att = jax.nn.sigmoid(ga_ref[...].astype(F32) + gbias_ref[:, D_MODEL:2 * D_MODEL])
    merged = (g_ssd * t_ssd + g_att * t_att).astype(BF16)
    t = jnp.dot(merged, wout_ref[...], preferred_element_type=F32)
    ms = jnp.mean(t * t, axis=-1, keepdims=True)
    out_ref[...] = x_ref[...] + t * lax.rsqrt(ms + EPS) * npost_ref[...]


def _merge(y2d, p1, o2d, x2d, wps, wpa, wout, snorm, gbias, npost, tm, name):
    m = x2d.shape[0]
    row_blk = lambda width, col: pl.BlockSpec((tm, width), lambda i: (i, col))
    const = lambda shape: pl.BlockSpec(shape, lambda i: (0, 0), pipeline_mode=pl.Buffered(1))
    in_specs = [
        row_blk(D_SSD, 0),
        row_blk(D_SSD, 0),
        row_blk(D_ATT, 0),
        row_blk(D_ATT, 5),
        row_blk(D_MODEL, 6),
        row_blk(D_MODEL, 7),
        row_blk(D_MODEL, 0),
        const((D_SSD, D_MODEL)), const((D_ATT, D_MODEL)), const((D_MODEL, D_MODEL)),
        const((1, D_SSD)), const((1, 2 * D_MODEL)), const((1, D_MODEL)),
    ]
    return pl.pallas_call(
        _merge_kernel,
        grid=(m // tm,),
        in_specs=in_specs,
        out_specs=pl.BlockSpec((tm, D_MODEL), lambda i: (i, 0)),
        out_shape=jax.ShapeDtypeStruct((m, D_MODEL), F32),
        compiler_params=pltpu.CompilerParams(
            dimension_semantics=("parallel",),
            vmem_limit_bytes=VMEM_LIMIT),
        name=name,
    )(y2d, p1, o2d, p1, p1, p1, x2d, wps, wpa, wout, snorm, gbias, npost)


def kernel(x, meta_tokens, norm_pre, w_in, conv_w, conv_b, dt_bias, a_log, d_skip, ssd_norm,
           fgate_bias, gate_bias, w_proj_ssd, w_proj_att, w_out, norm_post):
    bsz, seq, d = x.shape
    assert d == D_MODEL and seq % 1024 == 0 and norm_pre.shape[0] == 1

    w = jnp.swapaxes(w_in[0], 0, 1)
    o_xbc = D_SSD
    o_dt = o_xbc + CONV_DIM
    o_za = o_dt + H_SSD
    o_q = o_za + D_ATT
    o_k = o_q + D_ATT
    o_v = o_k + D_ATT
    o_f = o_v + D_ATT
    o_g = o_f + H_ATT
    att_scale = LOG2E / (ATT_HEAD_DIM ** 0.5)
    w_all = jnp.concatenate(
        [w[0:o_dt].astype(BF16), w[o_za:o_q].astype(BF16), w[o_g:].astype(BF16),
         (w[o_q:o_k] * att_scale).astype(BF16), w[o_k:o_f].astype(BF16)], axis=0)
    w_small = jnp.concatenate(
        [w[o_dt:o_za], w[o_f:o_g],
         jnp.zeros((LANES - H_SSD - H_ATT, d), F32)], axis=0).astype(BF16)
    g_pre = norm_pre[0][None, :]
    pad_lanes = jnp.zeros((LANES - H_SSD - H_ATT,), F32)
    brow = jnp.concatenate([dt_bias[0], fgate_bias[0], pad_lanes])[None, :]
    alog_row = jnp.concatenate([a_log[0], jnp.zeros((LANES - H_SSD,), F32)])[None, :]
    dskip_row = jnp.repeat(d_skip[0], SSD_HEAD_DIM)[None, :]
    head_of_lane = jnp.arange(D_SSD, dtype=jnp.int32) // SSD_HEAD_DIM
    expand_mat = (jnp.arange(LANES, dtype=jnp.int32)[:, None] == head_of_lane[None, :]).astype(BF16)
    convw = conv_w[0]
    convb = conv_b[0][None, :]
    wps = w_proj_ssd[0].astype(BF16)
    wpa = w_proj_att[0].astype(BF16)
    wout = w_out[0].astype(BF16)
    snorm = ssd_norm[0][None, :]
    gbias = gate_bias[0][None, :]
    npost = norm_post[0][None, :]

    meta_blk = jnp.concatenate([jnp.zeros((CHUNK - N_META, d), F32), meta_tokens.astype(F32)], axis=0)
    p_m, small_m = _in_proj(meta_blk, g_pre, w_all, w_small, BF16, CHUNK, P_TILE_N, "in_proj_meta")
    st_zero = jnp.zeros((SSD_GROUPS, D_STATE, GROUP_WIDTH), F32)
    tail_zero = jnp.zeros((TAIL_ROWS, CONV_DIM), BF16)
    c_zero = jnp.zeros((1, LANES), F32)
    _, bias_m, st_m, tail_m, cc_m = _ssd(
        p_m[None], small_m[None], convw, convb, brow, alog_row, dskip_row, expand_mat,
        st_zero, tail_zero, c_zero, CHUNK - N_META, "ssd_meta")

    x2d = x.reshape(bsz * seq, d)
    p, small = _in_proj(x2d, g_pre, w_all, w_small, BF16, 1024, P_TILE_N, "in_proj")
    p3d = p.reshape(bsz, seq, P_COLS)
    y, bias, _, _, _ = _ssd(
        p3d, small.reshape(bsz, seq, LANES), convw, convb, brow,
        alog_row, dskip_row, expand_mat, st_m[0], tail_m[0], cc_m[0], 0, "ssd")
    o = _attention(p3d, bias, p_m, bias_m[0], "fox_attention")
    out = _merge(y.reshape(bsz * seq, D_SSD), p, o.reshape(bsz * seq, D_ATT), x2d,
                 wps, wpa, wout, snorm, gbias, npost, 512, "merge_out")
    return out.reshape(bsz, seq, d)
```

```python
import functools

import numpy as np

import jax
import jax.numpy as jnp
from jax import lax
from jax.experimental import pallas as pl
from jax.experimental.pallas import tpu as pltpu

F32 = jnp.float32
BF16 = jnp.bfloat16

D_MODEL = 1024
N_META = 16
CHUNK = 128
LANES = 128
D_SSD = 2 * D_MODEL
SSD_HEAD_DIM = 64
H_SSD = D_SSD // SSD_HEAD_DIM
SSD_GROUPS = 4
HEADS_PER_GROUP = H_SSD // SSD_GROUPS
D_STATE = 128
GROUP_WIDTH = HEADS_PER_GROUP * SSD_HEAD_DIM
CONV_K = 4
CONV_DIM = D_SSD + 2 * SSD_GROUPS * D_STATE
H_ATT = 16
ATT_HEAD_DIM = 64
D_ATT = H_ATT * ATT_HEAD_DIM
EPS = 1e-6
LOG2E = 1.4426950408889634
TAIL_ROWS = 16
F_LANE0 = H_SSD
P_COLS = 11 * 1024
P_TILE_N = P_COLS // 4
Q_BLOCK0 = 8 * (1024 // LANES)
VMEM_LIMIT = 56 * 1024 * 1024
ATT_BLOCK = 256
ATT_UNROLL = 4
BIAS_ONE_LANE = 3 * H_ATT

_NT = (((1,), (1,)), ((), ()))


def _split3(v):
    hi = v.astype(BF16)
    r1 = v - hi.astype(F32)
    mid = r1.astype(BF16)
    lo = (r1 - mid.astype(F32)).astype(BF16)
    return hi, mid, lo


def _dot_exact_lhs(a_bf16, v):
    hi, mid, lo = _split3(v)
    d = functools.partial(jnp.dot, preferred_element_type=F32)
    return d(a_bf16, hi) + d(a_bf16, mid) + d(a_bf16, lo)


def _dot_exact_rhs(v, b_bf16):
    hi, mid, lo = _split3(v)
    d = functools.partial(jnp.dot, preferred_element_type=F32)
    return d(hi, b_bf16) + d(mid, b_bf16) + d(lo, b_bf16)


def _in_proj_kernel(x_ref, g_ref, w_ref, ws_ref, o_ref, os_ref, u_ref):
    j = pl.program_id(1)

    @pl.when(j == 0)
    def _():
        x = x_ref[...]
        ms = jnp.mean(x * x, axis=-1, keepdims=True)
        u = (x * lax.rsqrt(ms + EPS) * g_ref[...]).astype(BF16)
        u_ref[...] = u
        os_ref[...] = lax.dot_general(u, ws_ref[...], _NT, preferred_element_type=F32)

    o_ref[...] = lax.dot_general(u_ref[...], w_ref[...], _NT,
                                 preferred_element_type=F32).astype(o_ref.dtype)


def _in_proj(x2d, g_row, w_t, w_small_t, tm, tn, name):
    m, d = x2d.shape
    n = w_t.shape[0]
    return pl.pallas_call(
        _in_proj_kernel,
        grid=(m // tm, n // tn),
        in_specs=[pl.BlockSpec((tm, d), lambda i, j: (i, 0)),
                  pl.BlockSpec((1, d), lambda i, j: (0, 0)),
                  pl.BlockSpec((tn, d), lambda i, j: (j, 0)),
                  pl.BlockSpec((LANES, d), lambda i, j: (0, 0))],
        out_specs=[pl.BlockSpec((tm, tn), lambda i, j: (i, j)),
                   pl.BlockSpec((tm, LANES), lambda i, j: (i, 0))],
        out_shape=[jax.ShapeDtypeStruct((m, n), BF16),
                   jax.ShapeDtypeStruct((m, LANES), F32)],
        scratch_shapes=[pltpu.VMEM((tm, d), BF16)],
        compiler_params=pltpu.CompilerParams(
            dimension_semantics=("parallel", "arbitrary"),
            vmem_limit_bytes=VMEM_LIMIT),
        name=name,
    )(x2d, g_row, w_t, w_small_t)


def _ssd_kernel(x0_ref, x1_ref, bc_ref, s_ref, convw_ref, convb_ref, brow_ref, alog_ref,
                dskip_ref, e_ref, shift_ref, pb_ref, st0_ref, tail0_ref, c0_ref,
                y_ref, bias_ref, st_ref, tail_ref, cc_ref,
                ext_ref, act_ref, *, first_valid):
    ci = pl.program_id(1)

    @pl.when(ci == 0)
    def _():
        st_ref[0] = st0_ref[...]
        tail_ref[0] = tail0_ref[...]
        cc_ref[0] = c0_ref[...]

    ext_ref[0:TAIL_ROWS, :] = tail_ref[0]
    ext_ref[TAIL_ROWS:TAIL_ROWS + CHUNK, 0:1024] = x0_ref[0]
    ext_ref[TAIL_ROWS:TAIL_ROWS + CHUNK, 1024:2048] = x1_ref[0]
    ext_ref[TAIL_ROWS:TAIL_ROWS + CHUNK, 2048:3072] = bc_ref[0]
    tail_ref[0] = ext_ref[CHUNK:CHUNK + TAIL_ROWS, :]

    row = lax.broadcasted_iota(jnp.int32, (CHUNK, 1), 0)
    valid_row = row >= first_valid
    shift = shift_ref[...]
    for j in range(CONV_DIM // GROUP_WIDTH):
        sl = slice(j * GROUP_WIDTH, (j + 1) * GROUP_WIDTH)
        taps = jnp.dot(shift, ext_ref[:, sl], preferred_element_type=F32)
        acc = (convb_ref[:, sl] + convw_ref[CONV_K - 1:CONV_K, sl]
               * ext_ref[TAIL_ROWS:TAIL_ROWS + CHUNK, sl].astype(F32))
        for k in range(CONV_K - 1):
            acc = acc + convw_ref[k:k + 1, sl] * taps[k * CHUNK:(k + 1) * CHUNK]
        a = acc * jax.nn.sigmoid(acc)
        if first_valid:
            a = jnp.where(valid_row, a, 0.0)
        act_ref[:, sl] = a

    lane = lax.broadcasted_iota(jnp.int32, (1, LANES), 1)
    is_dt = lane < H_SSD
    is_f = (lane >= F_LANE0) & (lane < F_LANE0 + H_ATT)
    sv = s_ref[0] + brow_ref[...]
    z = jnp.where(is_dt, sv, -sv)
    sp = jnp.maximum(z, 0.0) + jnp.log(1.0 + jnp.exp(-jnp.abs(z)))
    a_row = -jnp.exp(alog_ref[...])
    vals = jnp.where(is_dt, sp * a_row, jnp.where(is_f, -sp, 0.0))
    dt = jnp.where(is_dt, sp, 0.0)
    if first_valid:
        vals = jnp.where(valid_row, vals, 0.0)
        dt = jnp.where(valid_row, dt, 0.0)

    ri = lax.broadcasted_iota(jnp.int32, (CHUNK, CHUNK), 0)
    cj = lax.broadcasted_iota(jnp.int32, (CHUNK, CHUNK), 1)
    tri = cj <= ri
    cum = _dot_exact_lhs(tri.astype(BF16), vals)
    cfull = cum + cc_ref[0]
    terms = jnp.concatenate(_split3(LOG2E * cfull), axis=1)
    placed = jnp.dot(terms, pb_ref[...], preferred_element_type=F32)
    bias_ref[0] = jnp.where(lane == BIAS_ONE_LANE, 1.0, placed).astype(BF16)
    cum_t = cfull.T
    cc_ref[0] = jnp.where(is_f, cfull[CHUNK - 1:CHUNK, :], 0.0)
    dt_t = dt.T

    a_last = cum[CHUNK - 1:CHUNK, :]
    w_t = (dt * jnp.exp(a_last - cum)).T
    cd = jnp.where(is_dt, jnp.exp(a_last), 0.0)
    cd_e = _dot_exact_rhs(jnp.broadcast_to(cd, (8, LANES)), e_ref[...])[0:1]

    lo_half = lane < SSD_HEAD_DIM
    neg_inf = jnp.float32(-jnp.inf)

    def split_heads(x16):
        zero16 = jnp.zeros_like(x16)
        return jnp.concatenate([jnp.where(lo_half, x16, zero16),
                                jnp.where(lo_half, zero16, x16)], axis=0)

    for g in range(SSD_GROUPS):
        bm = act_ref[:, D_SSD + g * D_STATE:D_SSD + (g + 1) * D_STATE]
        cm = act_ref[:, D_SSD + SSD_GROUPS * D_STATE + g * D_STATE:
                     D_SSD + SSD_GROUPS * D_STATE + (g + 1) * D_STATE]
        cm16 = cm.astype(BF16)
        cb = lax.dot_general(cm16, bm.astype(BF16), _NT,
                             preferred_element_type=F32).astype(BF16)
        bm_t = bm.T.astype(BF16)
        for pr in range(HEADS_PER_GROUP // 2):
            psl = slice(g * GROUP_WIDTH + pr * LANES, g * GROUP_WIDTH + (pr + 1) * LANES)
            ssl = slice(pr * LANES, (pr + 1) * LANES)
            xs_p = act_ref[:, psl]
            hin = st_ref[0, g, :, ssl]
            intra, inter, bw = [], [], []
            for h in (g * HEADS_PER_GROUP + 2 * pr, g * HEADS_PER_GROUP + 2 * pr + 1):
                a_col = cum[:, h:h + 1]
                seg = a_col - cum_t[h:h + 1, :]
                dec = jnp.exp(jnp.where(tri, seg, neg_inf)).astype(BF16)
                intra.append(cb * dec * dt_t[h:h + 1, :].astype(BF16))
                ea = jnp.broadcast_to(jnp.exp(a_col), (CHUNK, D_STATE)).astype(BF16)
                inter.append(cm16 * ea)
                bw.append(bm_t * w_t[h:h + 1, :].astype(BF16))
            xs2 = split_heads(xs_p.astype(BF16))
            y_lhs = jnp.concatenate(intra + inter, axis=1)
            y_rhs = jnp.concatenate([xs2, split_heads(hin.astype(BF16))], axis=0)
            y_pair = jnp.dot(y_lhs, y_rhs, preferred_element_type=F32)
            y_ref[0, :, psl] = (y_pair + dskip_ref[:, psl] * xs_p).astype(y_ref.dtype)
            st_ref[0, g, :, ssl] = hin * cd_e[:, psl] + jnp.dot(
                jnp.concatenate(bw, axis=1), xs2, preferred_element_type=F32)


def _conv_shift_matrix():
    m = np.zeros(((CONV_K - 1) * CHUNK, TAIL_ROWS + CHUNK), np.float32)
    for k in range(CONV_K - 1):
        for t in range(CHUNK):
            m[k * CHUNK + t, TAIL_ROWS + t - (CONV_K - 1) + k] = 1.0
    return jnp.asarray(m, BF16)


def _ssd(p1, small, convw, convb, brow, alog_row, dskip_row, expand_mat, st0, tail0, c0,
         first_valid, name):
    b, s, _ = p1.shape
    nc = s // CHUNK
    const = lambda shape: pl.BlockSpec(shape, lambda bi, ci: (0,) * len(shape))
    in_specs = [
        pl.BlockSpec((1, CHUNK, 1024), lambda bi, ci: (bi, ci, 2)),
        pl.BlockSpec((1, CHUNK, 1024), lambda bi, ci: (bi, ci, 3)),
        pl.BlockSpec((1, CHUNK, 1024), lambda bi, ci: (bi, ci, 4)),
        pl.BlockSpec((1, CHUNK, LANES), lambda bi, ci: (bi, ci, 0)),
        const((CONV_K, CONV_DIM)), const((1, CONV_DIM)), const((1, LANES)), const((1, LANES)),
        const((1, D_SSD)), const((LANES, D_SSD)), const(((CONV_K - 1) * CHUNK, TAIL_ROWS + CHUNK)),
        const((3 * LANES, LANES)),
        const((SSD_GROUPS, D_STATE, GROUP_WIDTH)), const((TAIL_ROWS, CONV_DIM)), const((1, LANES)),
    ]
    out_shape = [
        jax.ShapeDtypeStruct((b, s, D_SSD), BF16),
        jax.ShapeDtypeStruct((b, s, LANES), BF16),
        jax.ShapeDtypeStruct((b, SSD_GROUPS, D_STATE, GROUP_WIDTH), F32),
        jax.ShapeDtypeStruct((b, TAIL_ROWS, CONV_DIM), BF16),
        jax.ShapeDtypeStruct((b, 1, LANES), F32),
    ]
    out_specs = [
        pl.BlockSpec((1, CHUNK, D_SSD), lambda bi, ci: (bi, ci, 0)),
        pl.BlockSpec((1, CHUNK, LANES), lambda bi, ci: (bi, ci, 0)),
        pl.BlockSpec((1, SSD_GROUPS, D_STATE, GROUP_WIDTH), lambda bi, ci: (bi, 0, 0, 0)),
        pl.BlockSpec((1, TAIL_ROWS, CONV_DIM), lambda bi, ci: (bi, 0, 0)),
        pl.BlockSpec((1, 1, LANES), lambda bi, ci: (bi, 0, 0)),
    ]
    return pl.pallas_call(
        functools.partial(_ssd_kernel, first_valid=first_valid),
        grid=(b, nc),
        in_specs=in_specs,
        out_specs=out_specs,
        out_shape=out_shape,
        scratch_shapes=[pltpu.VMEM((TAIL_ROWS + CHUNK, CONV_DIM), BF16),
                        pltpu.VMEM((CHUNK, CONV_DIM), F32)],
        compiler_params=pltpu.CompilerParams(
            dimension_semantics=("parallel", "arbitrary"),
            vmem_limit_bytes=VMEM_LIMIT),
        name=name,
    )(p1, p1, p1, small, convw, convb, brow, alog_row, dskip_row, expand_mat,
      _conv_shift_matrix(), _bias_placement_matrix(), st0, tail0, c0)


V_ROWS = 80
L_ROW = ATT_HEAD_DIM
STAT_ROWS = 8


def _placement_matrices():
    npairs = H_ATT // 2
    pq = np.zeros((npairs, 2, 2 * LANES, LANES), np.float32)
    pk = np.zeros((npairs, LANES, LANES), np.float32)
    pv = np.zeros((2, V_ROWS, LANES), np.float32)
    for hp in range(npairs):
        for hh in range(2):
            own0 = hh * ATT_HEAD_DIM
            base = (1 - hh) * ATT_HEAD_DIM
            for l in range(own0, own0 + ATT_HEAD_DIM):
                pq[hp, hh, l, l] = 1.0
            for t in range(3):
                pq[hp, hh, LANES + 3 * (2 * hp + hh) + t, base + t] = 1.0
                pq[hp, hh, LANES + BIAS_ONE_LANE, base + 3 + t] = 1.0
                pk[hp, BIAS_ONE_LANE, base + t] = 1.0
                pk[hp, 3 * (2 * hp + hh) + t, base + 3 + t] = -1.0
    for hh in range(2):
        for r in range(ATT_HEAD_DIM):
            pv[hh, r, hh * ATT_HEAD_DIM + r] = 1.0
    return (jnp.asarray(pq.transpose(0, 1, 3, 2), BF16), jnp.asarray(pk, BF16),
            jnp.asarray(pv, BF16))


def _bias_placement_matrix():
    m = np.zeros((3 * LANES, LANES), np.float32)
    for h in range(H_ATT):
        for t in range(3):
            m[t * LANES + F_LANE0 + h, 3 * h + t] = 1.0
    return jnp.asarray(m, BF16)


def _pair_schedule(nq, unroll):
    assert nq % (2 * unroll) == 0
    slots = [[] for _ in range(unroll)]
    for i in range(nq // 2):
        slots[i % unroll] += [i, nq - 1 - i]
    full = [[(qb, li, kb) for li, qb in enumerate(qbs) for kb in range(1, qb + 1)]
            for qbs in slots]
    diag = [[(qb, li, qb + 1) for li, qb in enumerate(qbs)] for qbs in slots]
    g1, g2 = len(full[0]), len(diag[0])
    assert all(len(f) == g1 for f in full) and all(len(d) == g2 for d in diag)
    parts = []
    for sched in (full, diag):
        parts.append([qb for s in sched for (qb, _, _) in s])
        parts.append([li for s in sched for (_, li, _) in s])
        parts.append([kb for s in sched for (_, _, kb) in s])
    table = np.concatenate([np.asarray(p, np.int32) for p in parts])
    return slots, g1, g2, table


def _attn_kernel(tab_ref, q_ref, k_ref, v_ref, bias_ref, km_ref, vm_ref, bm_ref, pq_ref, pk_ref,
                 pv_ref, o_ref, kaug_scr, vsel_scr, qaug_scr, s_scr, p_scr, alpha_scr, m_scr,
                 acc_scr, *, slots, g1, g2):
    seq = q_ref.shape[1]
    blk = ATT_BLOCK
    unroll = len(slots)
    neg_inf = jnp.float32(-jnp.inf)
    vrow = lax.broadcasted_iota(jnp.int32, (V_ROWS, 1), 0)

    lane = lax.broadcasted_iota(jnp.int32, (1, LANES), 1)
    own = (lane < ATT_HEAD_DIM, lane >= ATT_HEAD_DIM)

    def keys_operands(kb, bias):
        placed = jnp.dot(bias, pk_ref[0], preferred_element_type=F32).astype(BF16)
        return [jnp.where(own[hh], kb, placed) for hh in range(2)]

    def values_operand(vb, hh):
        vt = lax.dot_general(pv_ref[hh], vb, _NT, preferred_element_type=F32)
        return jnp.where(vrow == L_ROW, 1.0, vt).astype(BF16)

    def prep_chunk(r, carry):
        rows = pl.ds(pl.multiple_of(r * blk, blk), blk)
        krows = pl.ds(pl.multiple_of((r + 1) * blk, blk), blk)
        bias = bias_ref[0, rows, :]
        xq = jnp.concatenate([q_ref[0, rows, :], bias], axis=1)
        k_ops = keys_operands(k_ref[0, rows, :], bias)
        vb = v_ref[0, rows, :]
        for hh in range(2):
            qaug_scr[hh, r] = lax.dot_general(pq_ref[0, hh], xq, _NT,
                                              preferred_element_type=F32).astype(BF16)
            kaug_scr[hh, krows, :] = k_ops[hh]
            vsel_scr[hh, r + 1] = values_operand(vb, hh)
        return carry

    lax.fori_loop(0, seq // blk, prep_chunk, 0, unroll=8)

    meta_rows = slice(CHUNK - N_META, CHUNK)
    where = {qb: (u, li) for u, qbs in enumerate(slots) for li, qb in enumerate(qbs)}
    k_metas = keys_operands(km_ref[meta_rows, :], bm_ref[meta_rows, :])
    for hh in range(2):
        v_meta = values_operand(vm_ref[meta_rows, :], hh)
        q_t = jnp.concatenate([qaug_scr[hh, qb] for qb in range(seq // blk)], axis=1)
        s = jnp.dot(k_metas[hh], q_t, preferred_element_type=F32)
        m0 = jnp.max(s, axis=0, keepdims=True)
        acc0 = jnp.dot(v_meta, jnp.exp2(s - m0).astype(BF16), preferred_element_type=F32)
        for qb, (u, li) in where.items():
            cols = slice(qb * blk, (qb + 1) * blk)
            m_scr[u, hh, li] = jnp.broadcast_to(m0[:, cols], m_scr.shape[-2:])
            acc_scr[u, hh, li] = acc0[:, cols]

    key_idx = lax.broadcasted_iota(jnp.int32, (blk, blk), 0)
    query_idx = lax.broadcasted_iota(jnp.int32, (blk, blk), 1)
    causal = key_idx <= query_idx
    stat_rows = m_scr.shape[-2]

    def run(base, ngroups, masked, s_scr, p_scr, alpha_scr):
        def entry(field, u, g):
            return tab_ref[base + (field * unroll + u) * ngroups + g]

        def stage_a(g):
            for u in range(unroll):
                qb = entry(0, u, g)
                krows = pl.ds(pl.multiple_of(entry(2, u, g) * blk, blk), blk)
                for hh in range(2):
                    s_scr[u, hh] = jnp.dot(kaug_scr[hh, krows, :], qaug_scr[hh, qb],
                                           preferred_element_type=F32)

        def stage_b(g):
            for u in range(unroll):
                li = entry(1, u, g)
                for hh in range(2):
                    s = s_scr[u, hh]
                    if masked:
                        s = jnp.where(causal, s, neg_inf)
                    m_prev = m_scr[u, hh, li][0:1]
                    m_next = jnp.maximum(m_prev, jnp.max(s, axis=0, keepdims=True))
                    alpha_scr[u, hh] = jnp.broadcast_to(jnp.exp2(m_prev - m_next),
                                                        (stat_rows, blk))
                    m_scr[u, hh, li] = jnp.broadcast_to(m_next, (stat_rows, blk))
                    p_scr[u, hh] = jnp.exp2(s - m_next).astype(BF16)

        def stage_c(g):
            for u in range(unroll):
                li = entry(1, u, g)
                kb = entry(2, u, g)
                for hh in range(2):
                    pv = jnp.dot(vsel_scr[hh, kb], p_scr[u, hh], preferred_element_type=F32)
                    acc_scr[u, hh, li] = acc_scr[u, hh, li] * alpha_scr[u, hh][0:1] + pv

        stage_a(0)
        stage_b(0)
        stage_a(min(1, ngroups - 1))

        def body(g, carry):
            stage_c(g - 1)
            stage_b(g)
            stage_a(jnp.minimum(g + 1, ngroups - 1))
            return carry

        lax.fori_loop(1, ngroups, body, 0)
        stage_c(ngroups - 1)

    run(0, g1, False, s_scr.at[0], p_scr.at[0], alpha_scr.at[0])
    run(3 * unroll * g1, g2, True, s_scr.at[1], p_scr.at[1], alpha_scr.at[1])

    for u, qbs in enumerate(slots):
        for li, qb in enumerate(qbs):
            outs = []
            for hh in range(2):
                a = acc_scr[u, hh, li]
                outs.append(a[0:ATT_HEAD_DIM] / a[L_ROW:L_ROW + 1])
            o_ref[0, qb * blk:(qb + 1) * blk, :] = jnp.concatenate(outs, axis=0).T.astype(o_ref.dtype)


def _attention(qkv, bias, qkv_meta, bias_meta, name):
    b, s, _ = qkv.shape
    npairs = H_ATT // 2
    blk = ATT_BLOCK
    slots, g1, g2, table = _pair_schedule(s // blk, ATT_UNROLL)
    nloc = len(slots[0])
    seq_blk = lambda col0: pl.BlockSpec((1, s, LANES), lambda bi, hp, tab: (bi, 0, col0 + hp))
    meta_blk = lambda col0: pl.BlockSpec((CHUNK, LANES), lambda bi, hp, tab: (0, col0 + hp))
    grid_spec = pltpu.PrefetchScalarGridSpec(
        num_scalar_prefetch=1,
        grid=(b, npairs),
        in_specs=[
            seq_blk(Q_BLOCK0), seq_blk(Q_BLOCK0 + npairs), seq_blk(Q_BLOCK0 + 2 * npairs),
            pl.BlockSpec((1, s, LANES), lambda bi, hp, tab: (bi, 0, 0)),
            meta_blk(Q_BLOCK0 + npairs), meta_blk(Q_BLOCK0 + 2 * npairs),
            pl.BlockSpec((CHUNK, LANES), lambda bi, hp, tab: (0, 0)),
            pl.BlockSpec((1, 2, LANES, 2 * LANES), lambda bi, hp, tab: (hp, 0, 0, 0)),
            pl.BlockSpec((1, LANES, LANES), lambda bi, hp, tab: (hp, 0, 0)),
            pl.BlockSpec((2, V_ROWS, LANES), lambda bi, hp, tab: (0, 0, 0)),
        ],
        out_specs=pl.BlockSpec((1, s, LANES), lambda bi, hp, tab: (bi, 0, hp)),
        scratch_shapes=[
            pltpu.VMEM((2, s + blk, LANES), BF16),
            pltpu.VMEM((2, s // blk + 1, V_ROWS, blk), BF16),
            pltpu.VMEM((2, s // blk, LANES, blk), BF16),
            pltpu.VMEM((2, ATT_UNROLL, 2, blk, blk), F32),
            pltpu.VMEM((2, ATT_UNROLL, 2, blk, blk), BF16),
            pltpu.VMEM((2, ATT_UNROLL, 2, STAT_ROWS, blk), F32),
            pltpu.VMEM((ATT_UNROLL, 2, nloc, STAT_ROWS, blk), F32),
            pltpu.VMEM((ATT_UNROLL, 2, nloc, V_ROWS, blk), F32),
        ])
    return pl.pallas_call(
        functools.partial(_attn_kernel, slots=slots, g1=g1, g2=g2),
        grid_spec=grid_spec,
        out_shape=jax.ShapeDtypeStruct((b, s, D_ATT), BF16),
        compiler_params=pltpu.CompilerParams(
            dimension_semantics=("parallel", "arbitrary"),
            vmem_limit_bytes=VMEM_LIMIT),
        name=name,
    )(jnp.asarray(table), qkv, qkv, qkv, bias, qkv_meta, qkv_meta, bias_meta,
      *_placement_matrices())


def _merge_kernel(y_ref, zs_ref, o_ref, za_ref, gs_ref, ga_ref, x_ref, wps_ref, wpa_ref,
                  wout_ref, snorm_ref, gbias_ref, npost_ref, out_ref):
    zs = zs_ref[...].astype(F32)
    u = y_ref[...].astype(F32) * (zs * jax.nn.sigmoid(zs))
    parts = []
    for g in range(SSD_GROUPS):
        ug = u[:, g * GROUP_WIDTH:(g + 1) * GROUP_WIDTH]
        ms = jnp.mean(ug * ug, axis=-1, keepdims=True)
        parts.append((ug * lax.rsqrt(ms + EPS)
                      * snorm_ref[:, g * GROUP_WIDTH:(g + 1) * GROUP_WIDTH]).astype(BF16))
    y_ssd = jnp.concatenate(parts, axis=1)
    t_ssd = jnp.dot(y_ssd, wps_ref[...], preferred_element_type=F32)
    za = za_ref[...].astype(F32)
    y_att = (o_ref[...].astype(F32) * (za * jax.nn.sigmoid(za))).astype(BF16)
    t_att = jnp.dot(y_att, wpa_ref[...], preferred_element_type=F32)
    g_ssd = jax.nn.sigmoid(gs_ref[...].astype(F32) + gbias_ref[:, 0:D_MODEL])
    g_att = jax.nn.sigmoid(ga_ref[...].astype(F32) + gbias_ref[:, D_MODEL:2 * D_MODEL])
    merged = (g_ssd * t_ssd + g_att * t_att).astype(BF16)
    t = jnp.dot(merged, wout_ref[...], preferred_element_type=F32)
    ms = jnp.mean(t * t, axis=-1, keepdims=True)
    out_ref[...] = x_ref[...] + t * lax.rsqrt(ms + EPS) * npost_ref[...]


def _merge(y2d, p1, o2d, x2d, wps, wpa, wout, snorm, gbias, npost, tm, name):
    m = x2d.shape[0]
    row_blk = lambda width, col: pl.BlockSpec((tm, width), lambda i: (i, col))
    const = lambda shape: pl.BlockSpec(shape, lambda i: (0, 0), pipeline_mode=pl.Buffered(1))
    in_specs = [
        row_blk(D_SSD, 0),
        row_blk(D_SSD, 0),
        row_blk(D_ATT, 0),
        row_blk(D_ATT, 5),
        row_blk(D_MODEL, 6),
        row_blk(D_MODEL, 7),
        row_blk(D_MODEL, 0),
        const((D_SSD, D_MODEL)), const((D_ATT, D_MODEL)), const((D_MODEL, D_MODEL)),
        const((1, D_SSD)), const((1, 2 * D_MODEL)), const((1, D_MODEL)),
    ]
    return pl.pallas_call(
        _merge_kernel,
        grid=(m // tm,),
        in_specs=in_specs,
        out_specs=pl.BlockSpec((tm, D_MODEL), lambda i: (i, 0)),
        out_shape=jax.ShapeDtypeStruct((m, D_MODEL), F32),
        compiler_params=pltpu.CompilerParams(
            dimension_semantics=("parallel",),
            vmem_limit_bytes=VMEM_LIMIT),
        name=name,
    )(y2d, p1, o2d, p1, p1, p1, x2d, wps, wpa, wout, snorm, gbias, npost)


def kernel(x, meta_tokens, norm_pre, w_in, conv_w, conv_b, dt_bias, a_log, d_skip, ssd_norm,
           fgate_bias, gate_bias, w_proj_ssd, w_proj_att, w_out, norm_post):
    bsz, seq, d = x.shape
    assert d == D_MODEL and seq % 1024 == 0 and norm_pre.shape[0] == 1

    w = jnp.swapaxes(w_in[0], 0, 1)
    o_xbc = D_SSD
    o_dt = o_xbc + CONV_DIM
    o_za = o_dt + H_SSD
    o_q = o_za + D_ATT
    o_k = o_q + D_ATT
    o_v = o_k + D_ATT
    o_f = o_v + D_ATT
    o_g = o_f + H_ATT
    att_scale = LOG2E / (ATT_HEAD_DIM ** 0.5)
    w_all = jnp.concatenate(
        [w[0:o_dt].astype(BF16), w[o_za:o_q].astype(BF16), w[o_g:].astype(BF16),
         (w[o_q:o_k] * att_scale).astype(BF16), w[o_k:o_f].astype(BF16)], axis=0)
    w_small = jnp.concatenate(
        [w[o_dt:o_za], w[o_f:o_g],
         jnp.zeros((LANES - H_SSD - H_ATT, d), F32)], axis=0).astype(BF16)
    g_pre = norm_pre[0][None, :]
    pad_lanes = jnp.zeros((LANES - H_SSD - H_ATT,), F32)
    brow = jnp.concatenate([dt_bias[0], fgate_bias[0], pad_lanes])[None, :]
    alog_row = jnp.concatenate([a_log[0], jnp.zeros((LANES - H_SSD,), F32)])[None, :]
    dskip_row = jnp.repeat(d_skip[0], SSD_HEAD_DIM)[None, :]
    head_of_lane = jnp.arange(D_SSD, dtype=jnp.int32) // SSD_HEAD_DIM
    expand_mat = (jnp.arange(LANES, dtype=jnp.int32)[:, None] == head_of_lane[None, :]).astype(BF16)
    convw = conv_w[0]
    convb = conv_b[0][None, :]
    wps = w_proj_ssd[0].astype(BF16)
    wpa = w_proj_att[0].astype(BF16)
    wout = w_out[0].astype(BF16)
    snorm = ssd_norm[0][None, :]
    gbias = gate_bias[0][None, :]
    npost = norm_post[0][None, :]

    meta_blk = jnp.concatenate([jnp.zeros((CHUNK - N_META, d), F32), meta_tokens.astype(F32)], axis=0)
    p_m, small_m = _in_proj(meta_blk, g_pre, w_all, w_small, CHUNK, P_TILE_N, "in_proj_meta")
    st_zero = jnp.zeros((SSD_GROUPS, D_STATE, GROUP_WIDTH), F32)
    tail_zero = jnp.zeros((TAIL_ROWS, CONV_DIM), BF16)
    c_zero = jnp.zeros((1, LANES), F32)
    _, bias_m, st_m, tail_m, cc_m = _ssd(
        p_m[None], small_m[None], convw, convb, brow, alog_row, dskip_row, expand_mat,
        st_zero, tail_zero, c_zero, CHUNK - N_META, "ssd_meta")

    x2d = x.reshape(bsz * seq, d)
    p, small = _in_proj(x2d, g_pre, w_all, w_small, 1024, P_TILE_N, "in_proj")
    p3d = p.reshape(bsz, seq, P_COLS)
    y, bias, _, _, _ = _ssd(
        p3d, small.reshape(bsz, seq, LANES), convw, convb, brow,
        alog_row, dskip_row, expand_mat, st_m[0], tail_m[0], cc_m[0], 0, "ssd")
    o = _attention(p3d, bias, p_m, bias_m[0], "fox_attention")
    out = _merge(y.reshape(bsz * seq, D_SSD), p, o.reshape(bsz * seq, D_ATT), x2d,
                 wps, wpa, wout, snorm, gbias, npost, 512, "merge_out")
    return out.reshape(bsz, seq, d)
```

```python
import functools

import numpy as np

import jax
import jax.numpy as jnp
from jax import lax
from jax.experimental import pallas as pl
from jax.experimental.pallas import tpu as pltpu

F32 = jnp.float32
BF16 = jnp.bfloat16

D_MODEL = 1024
N_META = 16
CHUNK = 128
LANES = 128
D_SSD = 2 * D_MODEL
SSD_HEAD_DIM = 64
H_SSD = D_SSD // SSD_HEAD_DIM
SSD_GROUPS = 4
HEADS_PER_GROUP = H_SSD // SSD_GROUPS
D_STATE = 128
GROUP_WIDTH = HEADS_PER_GROUP * SSD_HEAD_DIM
CONV_K = 4
CONV_DIM = D_SSD + 2 * SSD_GROUPS * D_STATE
H_ATT = 16
ATT_HEAD_DIM = 64
D_ATT = H_ATT * ATT_HEAD_DIM
EPS = 1e-6
LOG2E = 1.4426950408889634
TAIL_ROWS = 16
SSD_ROWS_PER_STEP = 2
F_LANE0 = H_SSD
P_COLS = 11 * 1024
P_TILE_N = P_COLS // 4
Q_BLOCK0 = 8 * (1024 // LANES)
VMEM_LIMIT = 56 * 1024 * 1024
ATT_BLOCK = 256
ATT_UNROLL = 4
BIAS_ONE_LANE = 3 * H_ATT

_NT = (((1,), (1,)), ((), ()))


def _split3(v):
    hi = v.astype(BF16)
    r1 = v - hi.astype(F32)
    mid = r1.astype(BF16)
    lo = (r1 - mid.astype(F32)).astype(BF16)
    return hi, mid, lo


def _dot_exact_lhs(a_bf16, v):
    hi, mid, lo = _split3(v)
    d = functools.partial(jnp.dot, preferred_element_type=F32)
    return d(a_bf16, hi) + d(a_bf16, mid) + d(a_bf16, lo)


def _dot_exact_rhs(v, b_bf16):
    hi, mid, lo = _split3(v)
    d = functools.partial(jnp.dot, preferred_element_type=F32)
    return d(hi, b_bf16) + d(mid, b_bf16) + d(lo, b_bf16)


def _in_proj_kernel(x_ref, g_ref, w_ref, ws_ref, o_ref, os_ref, u_ref):
    j = pl.program_id(1)

    @pl.when(j == 0)
    def _():
        x = x_ref[...]
        ms = jnp.mean(x * x, axis=-1, keepdims=True)
        u = (x * lax.rsqrt(ms + EPS) * g_ref[...]).astype(BF16)
        u_ref[...] = u
        os_ref[...] = lax.dot_general(u, ws_ref[...], _NT, preferred_element_type=F32)

    o_ref[...] = lax.dot_general(u_ref[...], w_ref[...], _NT,
                                 preferred_element_type=F32).astype(o_ref.dtype)


def _in_proj(x2d, g_row, w_t, w_small_t, tm, tn, name):
    m, d = x2d.shape
    n = w_t.shape[0]
    return pl.pallas_call(
        _in_proj_kernel,
        grid=(m // tm, n // tn),
        in_specs=[pl.BlockSpec((tm, d), lambda i, j: (i, 0)),
                  pl.BlockSpec((1, d), lambda i, j: (0, 0)),
                  pl.BlockSpec((tn, d), lambda i, j: (j, 0)),
                  pl.BlockSpec((LANES, d), lambda i, j: (0, 0))],
        out_specs=[pl.BlockSpec((tm, tn), lambda i, j: (i, j)),
                   pl.BlockSpec((tm, LANES), lambda i, j: (i, 0))],
        out_shape=[jax.ShapeDtypeStruct((m, n), BF16),
                   jax.ShapeDtypeStruct((m, LANES), F32)],
        scratch_shapes=[pltpu.VMEM((tm, d), BF16)],
        compiler_params=pltpu.CompilerParams(
            dimension_semantics=("parallel", "arbitrary"),
            vmem_limit_bytes=VMEM_LIMIT),
        name=name,
    )(x2d, g_row, w_t, w_small_t)


def _ssd_kernel(x0_ref, x1_ref, bc_ref, s_ref, convw_ref, convb_ref, brow_ref, alog_ref,
                dskip_ref, e_ref, shift_ref, pb_ref, st0_ref, tail0_ref, c0_ref,
                y_ref, bias_ref, st_ref, tail_ref, cc_ref,
                ext_ref, act_ref, *, first_valid):
    for bb in range(x0_ref.shape[0]):
        one = pl.ds(bb, 1)
        _ssd_row(x0_ref.at[one], x1_ref.at[one], bc_ref.at[one], s_ref.at[one], convw_ref,
                 convb_ref, brow_ref, alog_ref, dskip_ref, e_ref, shift_ref, pb_ref, st0_ref,
                 tail0_ref, c0_ref, y_ref.at[one], bias_ref.at[one], st_ref.at[one],
                 tail_ref.at[one], cc_ref.at[one], ext_ref.at[bb], act_ref.at[bb],
                 first_valid=first_valid)


def _ssd_row(x0_ref, x1_ref, bc_ref, s_ref, convw_ref, convb_ref, brow_ref, alog_ref,
             dskip_ref, e_ref, shift_ref, pb_ref, st0_ref, tail0_ref, c0_ref,
             y_ref, bias_ref, st_ref, tail_ref, cc_ref,
             ext_ref, act_ref, *, first_valid):
    ci = pl.program_id(1)

    @pl.when(ci == 0)
    def _():
        st_ref[0] = st0_ref[...]
        tail_ref[0] = tail0_ref[...]
        cc_ref[0] = c0_ref[...]

    ext_ref[0:TAIL_ROWS, :] = tail_ref[0]
    ext_ref[TAIL_ROWS:TAIL_ROWS + CHUNK, 0:1024] = x0_ref[0]
    ext_ref[TAIL_ROWS:TAIL_ROWS + CHUNK, 1024:2048] = x1_ref[0]
    ext_ref[TAIL_ROWS:TAIL_ROWS + CHUNK, 2048:3072] = bc_ref[0]
    tail_ref[0] = ext_ref[CHUNK:CHUNK + TAIL_ROWS, :]

    row = lax.broadcasted_iota(jnp.int32, (CHUNK, 1), 0)
    valid_row = row >= first_valid
    shift = shift_ref[...]
    for j in range(CONV_DIM // GROUP_WIDTH):
        sl = slice(j * GROUP_WIDTH, (j + 1) * GROUP_WIDTH)
        taps = jnp.dot(shift, ext_ref[:, sl], preferred_element_type=F32)
        acc = (convb_ref[:, sl] + convw_ref[CONV_K - 1:CONV_K, sl]
               * ext_ref[TAIL_ROWS:TAIL_ROWS + CHUNK, sl].astype(F32))
        for k in range(CONV_K - 1):
            acc = acc + convw_ref[k:k + 1, sl] * taps[k * CHUNK:(k + 1) * CHUNK]
        a = acc * jax.nn.sigmoid(acc)
        if first_valid:
            a = jnp.where(valid_row, a, 0.0)
        act_ref[:, sl] = a

    lane = lax.broadcasted_iota(jnp.int32, (1, LANES), 1)
    is_dt = lane < H_SSD
    is_f = (lane >= F_LANE0) & (lane < F_LANE0 + H_ATT)
    sv = s_ref[0] + brow_ref[...]
    z = jnp.where(is_dt, sv, -sv)
    sp = jnp.maximum(z, 0.0) + jnp.log(1.0 + jnp.exp(-jnp.abs(z)))
    a_row = -jnp.exp(alog_ref[...])
    vals = jnp.where(is_dt, sp * a_row, jnp.where(is_f, -sp, 0.0))
    dt = jnp.where(is_dt, sp, 0.0)
    if first_valid:
        vals = jnp.where(valid_row, vals, 0.0)
        dt = jnp.where(valid_row, dt, 0.0)

    ri = lax.broadcasted_iota(jnp.int32, (CHUNK, CHUNK), 0)
    cj = lax.broadcasted_iota(jnp.int32, (CHUNK, CHUNK), 1)
    tri = cj <= ri
    cum = _dot_exact_lhs(tri.astype(BF16), vals)
    cfull = cum + cc_ref[0]
    terms = jnp.concatenate(_split3(LOG2E * cfull), axis=1)
    placed = jnp.dot(terms, pb_ref[...], preferred_element_type=F32)
    bias_ref[0] = jnp.where(lane == BIAS_ONE_LANE, 1.0, placed).astype(BF16)
    cum_t = cfull.T
    cc_ref[0] = jnp.where(is_f, cfull[CHUNK - 1:CHUNK, :], 0.0)
    dt_t = dt.T

    a_last = cum[CHUNK - 1:CHUNK, :]
    w_t = (dt * jnp.exp(a_last - cum)).T
    cd = jnp.where(is_dt, jnp.exp(a_last), 0.0)
    cd_e = _dot_exact_rhs(jnp.broadcast_to(cd, (8, LANES)), e_ref[...])[0:1]

    lo_half = lane < SSD_HEAD_DIM
    neg_inf = jnp.float32(-jnp.inf)

    def split_heads(x16):
        zero16 = jnp.zeros_like(x16)
        return jnp.concatenate([jnp.where(lo_half, x16, zero16),
                                jnp.where(lo_half, zero16, x16)], axis=0)

    for g in range(SSD_GROUPS):
        bm = act_ref[:, D_SSD + g * D_STATE:D_SSD + (g + 1) * D_STATE]
        cm = act_ref[:, D_SSD + SSD_GROUPS * D_STATE + g * D_STATE:
                     D_SSD + SSD_GROUPS * D_STATE + (g + 1) * D_STATE]
        cm16 = cm.astype(BF16)
        cb = lax.dot_general(cm16, bm.astype(BF16), _NT,
                             preferred_element_type=F32).astype(BF16)
        bm_t = bm.T.astype(BF16)
        for pr in range(HEADS_PER_GROUP // 2):
            psl = slice(g * GROUP_WIDTH + pr * LANES, g * GROUP_WIDTH + (pr + 1) * LANES)
            ssl = slice(pr * LANES, (pr + 1) * LANES)
            xs_p = act_ref[:, psl]
            hin = st_ref[0, g, :, ssl]
            intra, inter, bw = [], [], []
            for h in (g * HEADS_PER_GROUP + 2 * pr, g * HEADS_PER_GROUP + 2 * pr + 1):
                a_col = cum[:, h:h + 1]
                seg = a_col - cum_t[h:h + 1, :]
                dec = jnp.exp(jnp.where(tri, seg, neg_inf)).astype(BF16)
                intra.append(cb * dec * dt_t[h:h + 1, :].astype(BF16))
                ea = jnp.broadcast_to(jnp.exp(a_col), (CHUNK, D_STATE)).astype(BF16)
                inter.append(cm16 * ea)
                bw.append(bm_t * w_t[h:h + 1, :].astype(BF16))
            xs2 = split_heads(xs_p.astype(BF16))
            y_lhs = jnp.concatenate(intra + inter, axis=1)
            y_rhs = jnp.concatenate([xs2, split_heads(hin.astype(BF16))], axis=0)
            y_pair = jnp.dot(y_lhs, y_rhs, preferred_element_type=F32)
            y_ref[0, :, psl] = (y_pair + dskip_ref[:, psl] * xs_p).astype(y_ref.dtype)
            st_ref[0, g, :, ssl] = hin * cd_e[:, psl] + jnp.dot(
                jnp.concatenate(bw, axis=1), xs2, preferred_element_type=F32)


def _conv_shift_matrix():
    m = np.zeros(((CONV_K - 1) * CHUNK, TAIL_ROWS + CHUNK), np.float32)
    for k in range(CONV_K - 1):
        for t in range(CHUNK):
            m[k * CHUNK + t, TAIL_ROWS + t - (CONV_K - 1) + k] = 1.0
    return jnp.asarray(m, BF16)


def _ssd(p1, small, convw, convb, brow, alog_row, dskip_row, expand_mat, st0, tail0, c0,
         first_valid, name):
    b, s, _ = p1.shape
    nc = s // CHUNK
    rows = min(b, SSD_ROWS_PER_STEP)
    const = lambda shape: pl.BlockSpec(shape, lambda bi, ci: (0,) * len(shape))
    in_specs = [
        pl.BlockSpec((rows, CHUNK, 1024), lambda bi, ci: (bi, ci, 2)),
        pl.BlockSpec((rows, CHUNK, 1024), lambda bi, ci: (bi, ci, 3)),
        pl.BlockSpec((rows, CHUNK, 1024), lambda bi, ci: (bi, ci, 4)),
        pl.BlockSpec((rows, CHUNK, LANES), lambda bi, ci: (bi, ci, 0)),
        const((CONV_K, CONV_DIM)), const((1, CONV_DIM)), const((1, LANES)), const((1, LANES)),
        const((1, D_SSD)), const((LANES, D_SSD)), const(((CONV_K - 1) * CHUNK, TAIL_ROWS + CHUNK)),
        const((3 * LANES, LANES)),
        const((SSD_GROUPS, D_STATE, GROUP_WIDTH)), const((TAIL_ROWS, CONV_DIM)), const((1, LANES)),
    ]
    out_shape = [
        jax.ShapeDtypeStruct((b, s, D_SSD), BF16),
        jax.ShapeDtypeStruct((b, s, LANES), BF16),
        jax.ShapeDtypeStruct((b, SSD_GROUPS, D_STATE, GROUP_WIDTH), F32),
        jax.ShapeDtypeStruct((b, TAIL_ROWS, CONV_DIM), BF16),
        jax.ShapeDtypeStruct((b, 1, LANES), F32),
    ]
    out_specs = [
        pl.BlockSpec((rows, CHUNK, D_SSD), lambda bi, ci: (bi, ci, 0)),
        pl.BlockSpec((rows, CHUNK, LANES), lambda bi, ci: (bi, ci, 0)),
        pl.BlockSpec((rows, SSD_GROUPS, D_STATE, GROUP_WIDTH), lambda bi, ci: (bi, 0, 0, 0)),
        pl.BlockSpec((rows, TAIL_ROWS, CONV_DIM), lambda bi, ci: (bi, 0, 0)),
        pl.BlockSpec((rows, 1, LANES), lambda bi, ci: (bi, 0, 0)),
    ]
    return pl.pallas_call(
        functools.partial(_ssd_kernel, first_valid=first_valid),
        grid=(b // rows, nc),
        in_specs=in_specs,
        out_specs=out_specs,
        out_shape=out_shape,
        scratch_shapes=[pltpu.VMEM((rows, TAIL_ROWS + CHUNK, CONV_DIM), BF16),
                        pltpu.VMEM((rows, CHUNK, CONV_DIM), F32)],
        compiler_params=pltpu.CompilerParams(
            dimension_semantics=("parallel", "arbitrary"),
            vmem_limit_bytes=VMEM_LIMIT),
        name=name,
    )(p1, p1, p1, small, convw, convb, brow, alog_row, dskip_row, expand_mat,
      _conv_shift_matrix(), _bias_placement_matrix(), st0, tail0, c0)


V_ROWS = 80
L_ROW = ATT_HEAD_DIM
STAT_ROWS = 8


def _placement_matrices():
    npairs = H_ATT // 2
    pq = np.zeros((npairs, 2, 2 * LANES, LANES), np.float32)
    pk = np.zeros((npairs, LANES, LANES), np.float32)
    pv = np.zeros((2, V_ROWS, LANES), np.float32)
    for hp in range(npairs):
        for hh in range(2):
            own0 = hh * ATT_HEAD_DIM
            base = (1 - hh) * ATT_HEAD_DIM
            for l in range(own0, own0 + ATT_HEAD_DIM):
                pq[hp, hh, l, l] = 1.0
            for t in range(3):
                pq[hp, hh, LANES + 3 * (2 * hp + hh) + t, base + t] = 1.0
                pq[hp, hh, LANES + BIAS_ONE_LANE, base + 3 + t] = 1.0
                pk[hp, BIAS_ONE_LANE, base + t] = 1.0
                pk[hp, 3 * (2 * hp + hh) + t, base + 3 + t] = -1.0
    for hh in range(2):
        for r in range(ATT_HEAD_DIM):
            pv[hh, r, hh * ATT_HEAD_DIM + r] = 1.0
    return (jnp.asarray(pq.transpose(0, 1, 3, 2), BF16), jnp.asarray(pk, BF16),
            jnp.asarray(pv, BF16))


def _bias_placement_matrix():
    m = np.zeros((3 * LANES, LANES), np.float32)
    for h in range(H_ATT):
        for t in range(3):
            m[t * LANES + F_LANE0 + h, 3 * h + t] = 1.0
    return jnp.asarray(m, BF16)


def _pair_schedule(nq, unroll):
    assert nq % (2 * unroll) == 0
    slots = [[] for _ in range(unroll)]
    for i in range(nq // 2):
        slots[i % unroll] += [i, nq - 1 - i]
    full = [[(qb, li, kb) for li, qb in enumerate(qbs) for kb in range(1, qb + 1)]
            for qbs in slots]
    diag = [[(qb, li, qb + 1) for li, qb in enumerate(qbs)] for qbs in slots]
    g1, g2 = len(full[0]), len(diag[0])
    assert all(len(f) == g1 for f in full) and all(len(d) == g2 for d in diag)
    parts = []
    for sched in (full, diag):
        parts.append([qb for s in sched for (qb, _, _) in s])
        parts.append([li for s in sched for (_, li, _) in s])
        parts.append([kb for s in sched for (_, _, kb) in s])
    table = np.concatenate([np.asarray(p, np.int32) for p in parts])
    return slots, g1, g2, table


def _attn_kernel(tab_ref, q_ref, k_ref, v_ref, bias_ref, km_ref, vm_ref, bm_ref, pq_ref, pk_ref,
                 pv_ref, o_ref, kaug_scr, vsel_scr, qaug_scr, s_scr, p_scr, alpha_scr, m_scr,
                 acc_scr, *, slots, g1, g2):
    seq = q_ref.shape[1]
    blk = ATT_BLOCK
    unroll = len(slots)
    neg_inf = jnp.float32(-jnp.inf)
    vrow = lax.broadcasted_iota(jnp.int32, (V_ROWS, 1), 0)

    lane = lax.broadcasted_iota(jnp.int32, (1, LANES), 1)
    own = (lane < ATT_HEAD_DIM, lane >= ATT_HEAD_DIM)

    def keys_operands(kb, bias):
        placed = jnp.dot(bias, pk_ref[0], preferred_element_type=F32).astype(BF16)
        return [jnp.where(own[hh], kb, placed) for hh in range(2)]

    def values_operand(vb, hh):
        vt = lax.dot_general(pv_ref[hh], vb, _NT, preferred_element_type=F32)
        return jnp.where(vrow == L_ROW, 1.0, vt).astype(BF16)

    def prep_chunk(r, carry):
        rows = pl.ds(pl.multiple_of(r * blk, blk), blk)
        krows = pl.ds(pl.multiple_of((r + 1) * blk, blk), blk)
        bias = bias_ref[0, rows, :]
        xq = jnp.concatenate([q_ref[0, rows, :], bias], axis=1)
        k_ops = keys_operands(k_ref[0, rows, :], bias)
        vb = v_ref[0, rows, :]
        for hh in range(2):
            qaug_scr[hh, r] = lax.dot_general(pq_ref[0, hh], xq, _NT,
                                              preferred_element_type=F32).astype(BF16)
            kaug_scr[hh, krows, :] = k_ops[hh]
            vsel_scr[hh, r + 1] = values_operand(vb, hh)
        return carry

    lax.fori_loop(0, seq // blk, prep_chunk, 0, unroll=8)

    meta_rows = slice(CHUNK - N_META, CHUNK)
    where = {qb: (u, li) for u, qbs in enumerate(slots) for li, qb in enumerate(qbs)}
    k_metas = keys_operands(km_ref[meta_rows, :], bm_ref[meta_rows, :])
    for hh in range(2):
        v_meta = values_operand(vm_ref[meta_rows, :], hh)
        q_t = jnp.concatenate([qaug_scr[hh, qb] for qb in range(seq // blk)], axis=1)
        s = jnp.dot(k_metas[hh], q_t, preferred_element_type=F32)
        m0 = jnp.max(s, axis=0, keepdims=True)
        acc0 = jnp.dot(v_meta, jnp.exp2(s - m0).astype(BF16), preferred_element_type=F32)
        for qb, (u, li) in where.items():
            cols = slice(qb * blk, (qb + 1) * blk)
            m_scr[u, hh, li] = jnp.broadcast_to(m0[:, cols], m_scr.shape[-2:])
            acc_scr[u, hh, li] = acc0[:, cols]

    key_idx = lax.broadcasted_iota(jnp.int32, (blk, blk), 0)
    query_idx = lax.broadcasted_iota(jnp.int32, (blk, blk), 1)
    causal = key_idx <= query_idx
    stat_rows = m_scr.shape[-2]

    def run(base, ngroups, masked, s_scr, p_scr, alpha_scr):
        def entry(field, u, g):
            return tab_ref[base + (field * unroll + u) * ngroups + g]

        def stage_a(g):
            for u in range(unroll):
                qb = entry(0, u, g)
                krows = pl.ds(pl.multiple_of(entry(2, u, g) * blk, blk), blk)
                for hh in range(2):
                    s_scr[u, hh] = jnp.dot(kaug_scr[hh, krows, :], qaug_scr[hh, qb],
                                           preferred_element_type=F32)

        def stage_b(g):
            for u in range(unroll):
                li = entry(1, u, g)
                for hh in range(2):
                    s = s_scr[u, hh]
                    if masked:
                        s = jnp.where(causal, s, neg_inf)
                    m_prev = m_scr[u, hh, li][0:1]
                    m_next = jnp.maximum(m_prev, jnp.max(s, axis=0, keepdims=True))
                    alpha_scr[u, hh] = jnp.broadcast_to(jnp.exp2(m_prev - m_next),
                                                        (stat_rows, blk))
                    m_scr[u, hh, li] = jnp.broadcast_to(m_next, (stat_rows, blk))
                    p_scr[u, hh] = jnp.exp2(s - m_next).astype(BF16)

        def stage_c(g):
            for u in range(unroll):
                li = entry(1, u, g)
                kb = entry(2, u, g)
                for hh in range(2):
                    pv = jnp.dot(vsel_scr[hh, kb], p_scr[u, hh], preferred_element_type=F32)
                    acc_scr[u, hh, li] = acc_scr[u, hh, li] * alpha_scr[u, hh][0:1] + pv

        stage_a(0)
        stage_b(0)
        stage_a(min(1, ngroups - 1))

        def body(g, carry):
            stage_c(g - 1)
            stage_b(g)
            stage_a(jnp.minimum(g + 1, ngroups - 1))
            return carry

        lax.fori_loop(1, ngroups, body, 0)
        stage_c(ngroups - 1)

    run(0, g1, False, s_scr.at[0], p_scr.at[0], alpha_scr.at[0])
    run(3 * unroll * g1, g2, True, s_scr.at[1], p_scr.at[1], alpha_scr.at[1])

    for u, qbs in enumerate(slots):
        for li, qb in enumerate(qbs):
            outs = []
            for hh in range(2):
                a = acc_scr[u, hh, li]
                outs.append(a[0:ATT_HEAD_DIM] / a[L_ROW:L_ROW + 1])
            o_ref[0, qb * blk:(qb + 1) * blk, :] = jnp.concatenate(outs, axis=0).T.astype(o_ref.dtype)


def _attention(qkv, bias, qkv_meta, bias_meta, name):
    b, s, _ = qkv.shape
    npairs = H_ATT // 2
    blk = ATT_BLOCK
    slots, g1, g2, table = _pair_schedule(s // blk, ATT_UNROLL)
    nloc = len(slots[0])
    seq_blk = lambda col0: pl.BlockSpec((1, s, LANES), lambda bi, hp, tab: (bi, 0, col0 + hp))
    meta_blk = lambda col0: pl.BlockSpec((CHUNK, LANES), lambda bi, hp, tab: (0, col0 + hp))
    grid_spec = pltpu.PrefetchScalarGridSpec(
        num_scalar_prefetch=1,
        grid=(b, npairs),
        in_specs=[
            seq_blk(Q_BLOCK0), seq_blk(Q_BLOCK0 + npairs), seq_blk(Q_BLOCK0 + 2 * npairs),
            pl.BlockSpec((1, s, LANES), lambda bi, hp, tab: (bi, 0, 0)),
            meta_blk(Q_BLOCK0 + npairs), meta_blk(Q_BLOCK0 + 2 * npairs),
            pl.BlockSpec((CHUNK, LANES), lambda bi, hp, tab: (0, 0)),
            pl.BlockSpec((1, 2, LANES, 2 * LANES), lambda bi, hp, tab: (hp, 0, 0, 0)),
            pl.BlockSpec((1, LANES, LANES), lambda bi, hp, tab: (hp, 0, 0)),
            pl.BlockSpec((2, V_ROWS, LANES), lambda bi, hp, tab: (0, 0, 0)),
        ],
        out_specs=pl.BlockSpec((1, s, LANES), lambda bi, hp, tab: (bi, 0, hp)),
        scratch_shapes=[
            pltpu.VMEM((2, s + blk, LANES), BF16),
            pltpu.VMEM((2, s // blk + 1, V_ROWS, blk), BF16),
            pltpu.VMEM((2, s // blk, LANES, blk), BF16),
            pltpu.VMEM((2, ATT_UNROLL, 2, blk, blk), F32),
            pltpu.VMEM((2, ATT_UNROLL, 2, blk, blk), BF16),
            pltpu.VMEM((2, ATT_UNROLL, 2, STAT_ROWS, blk), F32),
            pltpu.VMEM((ATT_UNROLL, 2, nloc, STAT_ROWS, blk), F32),
            pltpu.VMEM((ATT_UNROLL, 2, nloc, V_ROWS, blk), F32),
        ])
    return pl.pallas_call(
        functools.partial(_attn_kernel, slots=slots, g1=g1, g2=g2),
        grid_spec=grid_spec,
        out_shape=jax.ShapeDtypeStruct((b, s, D_ATT), BF16),
        compiler_params=pltpu.CompilerParams(
            dimension_semantics=("parallel", "arbitrary"),
            vmem_limit_bytes=VMEM_LIMIT),
        name=name,
    )(jnp.asarray(table), qkv, qkv, qkv, bias, qkv_meta, qkv_meta, bias_meta,
      *_placement_matrices())


def _merge_kernel(y_ref, zs_ref, o_ref, za_ref, gs_ref, ga_ref, x_ref, wps_ref, wpa_ref,
                  wout_ref, snorm_ref, gbias_ref, npost_ref, out_ref):
    zs = zs_ref[...].astype(F32)
    u = y_ref[...].astype(F32) * (zs * jax.nn.sigmoid(zs))
    parts = []
    for g in range(SSD_GROUPS):
        ug = u[:, g * GROUP_WIDTH:(g + 1) * GROUP_WIDTH]
        ms = jnp.mean(ug * ug, axis=-1, keepdims=True)
        parts.append((ug * lax.rsqrt(ms + EPS)
                      * snorm_ref[:, g * GROUP_WIDTH:(g + 1) * GROUP_WIDTH]).astype(BF16))
    y_ssd = jnp.concatenate(parts, axis=1)
    t_ssd = jnp.dot(y_ssd, wps_ref[...], preferred_element_type=F32)
    za = za_ref[...].astype(F32)
    y_att = (o_ref[...].astype(F32) * (za * jax.nn.sigmoid(za))).astype(BF16)
    t_att = jnp.dot(y_att, wpa_ref[...], preferred_element_type=F32)
    g_ssd = jax.nn.sigmoid(gs_ref[...].astype(F32) + gbias_ref[:, 0:D_MODEL])
    g_att = jax.nn.sigmoid(ga_ref[...].astype(F32) + gbias_ref[:, D_MODEL:2 * D_MODEL])
    merged = (g_ssd * t_ssd + g_att * t_att).astype(BF16)
    t = jnp.dot(merged, wout_ref[...], preferred_element_type=F32)
    ms = jnp.mean(t * t, axis=-1, keepdims=True)
    out_ref[...] = x_ref[...] + t * lax.rsqrt(ms + EPS) * npost_ref[...]


def _merge(y2d, p1, o2d, x2d, wps, wpa, wout, snorm, gbias, npost, tm, name):
    m = x2d.shape[0]
    row_blk = lambda width, col: pl.BlockSpec((tm, width), lambda i: (i, col))
    const = lambda shape: pl.BlockSpec(shape, lambda i: (0, 0), pipeline_mode=pl.Buffered(1))
    in_specs = [
        row_blk(D_SSD, 0),
        row_blk(D_SSD, 0),
        row_blk(D_ATT, 0),
        row_blk(D_ATT, 5),
        row_blk(D_MODEL, 6),
        row_blk(D_MODEL, 7),
        row_blk(D_MODEL, 0),
        const((D_SSD, D_MODEL)), const((D_ATT, D_MODEL)), const((D_MODEL, D_MODEL)),
        const((1, D_SSD)), const((1, 2 * D_MODEL)), const((1, D_MODEL)),
    ]
    return pl.pallas_call(
        _merge_kernel,
        grid=(m // tm,),
        in_specs=in_specs,
        out_specs=pl.BlockSpec((tm, D_MODEL), lambda i: (i, 0)),
        out_shape=jax.ShapeDtypeStruct((m, D_MODEL), F32),
        compiler_params=pltpu.CompilerParams(
            dimension_semantics=("parallel",),
            vmem_limit_bytes=VMEM_LIMIT),
        name=name,
    )(y2d, p1, o2d, p1, p1, p1, x2d, wps, wpa, wout, snorm, gbias, npost)


def kernel(x, meta_tokens, norm_pre, w_in, conv_w, conv_b, dt_bias, a_log, d_skip, ssd_norm,
           fgate_bias, gate_bias, w_proj_ssd, w_proj_att, w_out, norm_post):
    bsz, seq, d = x.shape
    assert d == D_MODEL and seq % 1024 == 0 and norm_pre.shape[0] == 1

    w = jnp.swapaxes(w_in[0], 0, 1)
    o_xbc = D_SSD
    o_dt = o_xbc + CONV_DIM
    o_za = o_dt + H_SSD
    o_q = o_za + D_ATT
    o_k = o_q + D_ATT
    o_v = o_k + D_ATT
    o_f = o_v + D_ATT
    o_g = o_f + H_ATT
    att_scale = LOG2E / (ATT_HEAD_DIM ** 0.5)
    w_all = jnp.concatenate(
        [w[0:o_dt].astype(BF16), w[o_za:o_q].astype(BF16), w[o_g:].astype(BF16),
         (w[o_q:o_k] * att_scale).astype(BF16), w[o_k:o_f].astype(BF16)], axis=0)
    w_small = jnp.concatenate(
        [w[o_dt:o_za], w[o_f:o_g],
         jnp.zeros((LANES - H_SSD - H_ATT, d), F32)], axis=0).astype(BF16)
    g_pre = norm_pre[0][None, :]
    pad_lanes = jnp.zeros((LANES - H_SSD - H_ATT,), F32)
    brow = jnp.concatenate([dt_bias[0], fgate_bias[0], pad_lanes])[None, :]
    alog_row = jnp.concatenate([a_log[0], jnp.zeros((LANES - H_SSD,), F32)])[None, :]
    dskip_row = jnp.repeat(d_skip[0], SSD_HEAD_DIM)[None, :]
    head_of_lane = jnp.arange(D_SSD, dtype=jnp.int32) // SSD_HEAD_DIM
    expand_mat = (jnp.arange(LANES, dtype=jnp.int32)[:, None] == head_of_lane[None, :]).astype(BF16)
    convw = conv_w[0]
    convb = conv_b[0][None, :]
    wps = w_proj_ssd[0].astype(BF16)
    wpa = w_proj_att[0].astype(BF16)
    wout = w_out[0].astype(BF16)
    snorm = ssd_norm[0][None, :]
    gbias = gate_bias[0][None, :]
    npost = norm_post[0][None, :]

    meta_blk = jnp.concatenate([jnp.zeros((CHUNK - N_META, d), F32), meta_tokens.astype(F32)], axis=0)
    p_m, small_m = _in_proj(meta_blk, g_pre, w_all, w_small, CHUNK, P_TILE_N, "in_proj_meta")
    st_zero = jnp.zeros((SSD_GROUPS, D_STATE, GROUP_WIDTH), F32)
    tail_zero = jnp.zeros((TAIL_ROWS, CONV_DIM), BF16)
    c_zero = jnp.zeros((1, LANES), F32)
    _, bias_m, st_m, tail_m, cc_m = _ssd(
        p_m[None], small_m[None], convw, convb, brow, alog_row, dskip_row, expand_mat,
        st_zero, tail_zero, c_zero, CHUNK - N_META, "ssd_meta")

    x2d = x.reshape(bsz * seq, d)
    p, small = _in_proj(x2d, g_pre, w_all, w_small, 1024, P_TILE_N, "in_proj")
    p3d = p.reshape(bsz, seq, P_COLS)
    y, bias, _, _, _ = _ssd(
        p3d, small.reshape(bsz, seq, LANES), convw, convb, brow,
        alog_row, dskip_row, expand_mat, st_m[0], tail_m[0], cc_m[0], 0, "ssd")
    o = _attention(p3d, bias, p_m, bias_m[0], "fox_attention")
    out = _merge(y.reshape(bsz * seq, D_SSD), p, o.reshape(bsz * seq, D_ATT), x2d,
                 wps, wpa, wout, snorm, gbias, npost, 512, "merge_out")
    return out.reshape(bsz, seq, d)
```
